```python
import jax, jax.numpy as jnp
from jax import lax
import numpy as np

D_MODEL = 1024
BATCH = 4
SEQ = 4096
DEPTH = 2
DEC_BATCH = 128
DEC_SEQ = 8
PAST_LEN = 2048
PAGE_SIZE = 128

N_META = 16
N_A_LAYERS = DEPTH // 2
N_B_LAYERS = DEPTH - N_A_LAYERS
H_A = 8
DK_A = D_MODEL // (2 * H_A)
DV_A = D_MODEL // H_A
CHUNK_A = 64
GATE_CAP = 15.0
H_B = 16
DH_B = D_MODEL // H_B
Q_BLOCK = 128
D_FF = -(-8 * D_MODEL // (3 * 256)) * 256
EPS = 1e-6

kernel_name = 'fox_mlstm_yoco_step'


def rmsnorm(x, g):
    xf = x.astype(jnp.float32)
    r = lax.rsqrt(jnp.mean(xf * xf, axis=-1, keepdims=True) + EPS)
    return (xf * r * g.astype(jnp.float32)).astype(x.dtype)


def softcap(x):
    return GATE_CAP * jnp.tanh(x / GATE_CAP)


def mlstm_chunk(state, inp):
    C, n, m = state
    q, k, v, li, lf = inp
    L = q.shape[2]
    b = jnp.cumsum(lf, axis=-1)
    causal = jnp.tril(jnp.ones((L, L), dtype=bool))
    dmat = jnp.where(causal, b[..., :, None] - b[..., None, :] + li[..., None, :], -jnp.inf)
    inter = b + m[..., None]
    m_t = jnp.maximum(inter, jnp.max(dmat, axis=-1))
    w_inter = jnp.exp(inter - m_t)
    s = jnp.einsum('bhtk,bhsk->bhts', q, k) * jnp.exp(dmat - m_t[..., None])
    num = w_inter[..., None] * jnp.einsum('bhtk,bhkv->bhtv', q, C) + jnp.einsum('bhts,bhsv->bhtv', s, v)
    den = w_inter * jnp.einsum('bhtk,bhk->bht', q, n) + jnp.sum(s, axis=-1)
    h = num / jnp.maximum(jnp.abs(den), jnp.exp(-m_t))[..., None]
    b_end = b[..., -1]
    g = b_end[..., None] - b + li
    m_new = jnp.maximum(b_end + m, jnp.max(g, axis=-1))
    w_c = jnp.exp(b_end + m - m_new)
    w_g = jnp.exp(g - m_new[..., None])
    C_new = w_c[..., None, None] * C + jnp.einsum('bhs,bhsk,bhsv->bhkv', w_g, k, v)
    n_new = w_c[..., None] * n + jnp.einsum('bhs,bhsk->bhk', w_g, k)
    return (C_new, n_new, m_new), h


def mlstm_sequence(q, k, v, li, lf, state, lead):
    T = q.shape[2]
    state, h_lead = mlstm_chunk(state, (q[:, :, :lead], k[:, :, :lead], v[:, :, :lead], li[:, :, :lead], lf[:, :, :lead]))
    rest = T - lead
    if rest == 0:
        return h_lead, state
    nc = rest // CHUNK_A

    def to_chunks(a):
        a = a[:, :, lead:]
        return jnp.moveaxis(a.reshape(a.shape[:2] + (nc, CHUNK_A) + a.shape[3:]), 2, 0)

    state, h_rest = lax.scan(mlstm_chunk, state, (to_chunks(q), to_chunks(k), to_chunks(v), to_chunks(li), to_chunks(lf)))
    Bq, H = q.shape[:2]
    h_rest = jnp.moveaxis(h_rest, 0, 2).reshape(Bq, H, rest, DV_A)
    return jnp.concatenate([h_lead, h_rest], axis=2), state


def mlstm_mixer(xn, C0, n0, m0, lead, w_in, b_ig, b_fg, mh_g, w_out):
    B, T, _ = xn.shape
    HK, HV = H_A * DK_A, H_A * DV_A
    p = xn @ w_in
    f32 = jnp.float32

    def heads(a, d):
        return a.reshape(B, T, H_A, d).transpose(0, 2, 1, 3).astype(f32)

    q = heads(p[..., :HK], DK_A) * DK_A ** -0.5
    k = heads(p[..., HK:2 * HK], DK_A)
    v = heads(p[..., 2 * HK:2 * HK + HV], DV_A)
    og = p[..., 2 * HK + HV:2 * HK + 2 * HV]
    gi = p[..., 2 * HK + 2 * HV:2 * HK + 2 * HV + H_A]
    gf = p[..., 2 * HK + 2 * HV + H_A:]
    li = softcap(gi.astype(f32) + b_ig.astype(f32)).transpose(0, 2, 1)
    lf = jax.nn.log_sigmoid(softcap(gf.astype(f32) + b_fg.astype(f32))).transpose(0, 2, 1)
    state0 = (C0.astype(f32), n0.astype(f32), m0.astype(f32))
    h, state = mlstm_sequence(q, k, v, li, lf, state0, lead)
    h = rmsnorm(h.transpose(0, 2, 1, 3), mh_g)
    out = (h.reshape(B, T, HV).astype(xn.dtype) * jax.nn.sigmoid(og)) @ w_out
    return out, state


def fox_attention(q, cq, pos_q, k, v, ck, pos_k):
    B, Tq, H, Dh = q.shape
    f32 = jnp.float32
    blk = min(Q_BLOCK, Tq)
    nblk = -(-Tq // blk)
    pad = nblk * blk - Tq
    if pad:
        q = jnp.pad(q, ((0, 0), (0, pad), (0, 0), (0, 0)))
        cq = jnp.pad(cq, ((0, 0), (0, pad), (0, 0)))
        pos_q = jnp.pad(pos_q, (0, pad), mode='edge')
    qb = q.reshape(B, nblk, blk, H, Dh).swapaxes(0, 1)
    cqb = cq.reshape(B, nblk, blk, H).swapaxes(0, 1)
    pqb = pos_q.reshape(nblk, blk)
    ckT = ck.astype(f32).transpose(0, 2, 1)

    def one_block(args):
        qi, ci, pi = args
        s = jnp.einsum('bqhd,bkhd->bhqk', qi, k, preferred_element_type=f32)
        s = s + ci.astype(f32).transpose(0, 2, 1)[..., None] - ckT[:, :, None, :]
        s = jnp.where(pos_k[None, :] <= pi[:, None], s, -jnp.inf)
        p = jax.nn.softmax(s, axis=-1)
        return jnp.einsum('bhqk,bkhd->bqhd', p.astype(v.dtype), v)

    o = lax.map(one_block, (qb, cqb, pqb))
    return o.swapaxes(0, 1).reshape(B, nblk * blk, H, Dh)[:, :Tq]


def fox_mixer(xn, shared, attend, w_qo, q_g, w_out):
    B, T, _ = xn.shape
    HD = H_B * DH_B
    p = xn @ w_qo
    q = rmsnorm(p[..., :HD].reshape(B, T, H_B, DH_B), q_g) * DH_B ** -0.5
    og = p[..., HD:]
    k, v, lf = shared
    o = attend(q, k, v, lf).reshape(B, T, HD)
    return (o * jax.nn.sigmoid(og)) @ w_out


def shared_kv(h, g_kv, w_kvf, b_f, k_g):
    B, T, _ = h.shape
    HD = H_B * DH_B
    xs = rmsnorm(h, g_kv)
    p = xs @ w_kvf
    k = rmsnorm(p[..., :HD].reshape(B, T, H_B, DH_B), k_g)
    v = p[..., HD:2 * HD].reshape(B, T, H_B, DH_B)
    lf = jax.nn.log_sigmoid((p[..., 2 * HD:] + b_f).astype(jnp.float32))
    return k, v, lf


def swiglu(xn, w_gu, w_d):
    gu = xn @ w_gu
    return (jax.nn.silu(gu[..., :D_FF]) * gu[..., D_FF:]) @ w_d


def run_trunk(h, C0, n0, m0, lead, attend, prm):
    Cs, ns, ms = [], [], []
    shared = None
    for l in range(DEPTH):
        if l < N_A_LAYERS:
            i = l
            a, (C, n, m) = mlstm_mixer(rmsnorm(h, prm['norm_a'][i]), C0[i], n0[i], m0[i], lead,
                                       prm['w_in_a'][i], prm['b_ig_a'][i], prm['b_fg_a'][i],
                                       prm['mh_norm_a'][i], prm['w_out_a'][i])
            h = h + a
            Cs.append(C); ns.append(n); ms.append(m)
        else:
            j = l - N_A_LAYERS
            h = h + fox_mixer(rmsnorm(h, prm['norm_b'][j]), shared, attend,
                              prm['w_qo_b'][j], prm['q_norm_b'][j], prm['w_out_b'][j])
        h = h + swiglu(rmsnorm(h, prm['norm_ffn'][l]), prm['w_gate_up'][l], prm['w_down'][l])
        if l == N_A_LAYERS - 1:
            shared = shared_kv(h, prm['norm_kv'], prm['w_kvf'], prm['b_fg_b'], prm['k_norm_b'])
    y = rmsnorm(h, prm['norm_final'])
    return y, jnp.stack(Cs), jnp.stack(ns), jnp.stack(ms), shared


def setup_inputs(seed: int = 0) -> dict:
    key = jax.random.key(seed)
    ks = jax.random.split(key, 32)
    f32 = jnp.float32
    n_pages = PAST_LEN // PAGE_SIZE
    n_phys = (DEC_BATCH * n_pages * 5) // 4
    HK, HV, HD = H_A * DK_A, H_A * DV_A, H_B * DH_B

    def nrm(k, shape, scale=1.0):
        return jax.random.normal(k, shape, f32) * scale

    def gain(k, shape):
        return 1.0 + 0.05 * jax.random.normal(k, shape, f32)

    page_table = jax.random.permutation(ks[8], n_phys)[:DEC_BATCH * n_pages].reshape(DEC_BATCH, n_pages).astype(jnp.int32)
    return {
        'x_prompt': nrm(ks[0], (BATCH, SEQ, D_MODEL)),
        'x_sample': nrm(ks[1], (DEC_BATCH, DEC_SEQ, D_MODEL)),
        'state_C': nrm(ks[2], (N_A_LAYERS, DEC_BATCH, H_A, DK_A, DV_A), 0.5),
        'state_n': nrm(ks[3], (N_A_LAYERS, DEC_BATCH, H_A, DK_A), 0.5),
        'state_m': nrm(ks[4], (N_A_LAYERS, DEC_BATCH, H_A)),
        'cache_k': nrm(ks[5], (n_phys, PAGE_SIZE, H_B, DH_B)),
        'cache_v': nrm(ks[6], (n_phys, PAGE_SIZE, H_B, DH_B)),
        'cache_logf': jax.nn.log_sigmoid(3.0 + nrm(ks[7], (n_phys, PAGE_SIZE, H_B))),
        'page_table': page_table,
        'meta_tokens': nrm(ks[9], (N_META, D_MODEL)),
        'norm_a': gain(ks[10], (N_A_LAYERS, D_MODEL)),
        'w_in_a': nrm(ks[11], (N_A_LAYERS, D_MODEL, 2 * HK + 2 * HV + 2 * H_A), D_MODEL ** -0.5),
        'b_ig_a': nrm(ks[12], (N_A_LAYERS, H_A), 0.5),
        'b_fg_a': 3.0 + nrm(ks[13], (N_A_LAYERS, H_A), 0.5),
        'mh_norm_a': gain(ks[14], (N_A_LAYERS, H_A, DV_A)),
        'w_out_a': nrm(ks[15], (N_A_LAYERS, HV, D_MODEL), HV ** -0.5),
        'norm_kv': gain(ks[16], (D_MODEL,)),
        'w_kvf': nrm(ks[17], (D_MODEL, 2 * HD + H_B), D_MODEL ** -0.5),
        'b_fg_b': 3.0 + nrm(ks[18], (H_B,), 0.5),
        'k_norm_b': gain(ks[19], (DH_B,)),
        'norm_b': gain(ks[20], (N_B_LAYERS, D_MODEL)),
        'w_qo_b': nrm(ks[21], (N_B_LAYERS, D_MODEL, HD + D_MODEL), D_MODEL ** -0.5),
        'q_norm_b': gain(ks[22], (N_B_LAYERS, DH_B)),
        'w_out_b': nrm(ks[23], (N_B_LAYERS, HD, D_MODEL), HD ** -0.5),
        'norm_ffn': gain(ks[24], (DEPTH, D_MODEL)),
        'w_gate_up': nrm(ks[25], (DEPTH, D_MODEL, 2 * D_FF), D_MODEL ** -0.5),
        'w_down': nrm(ks[26], (DEPTH, D_FF, D_MODEL), D_FF ** -0.5),
        'norm_final': gain(ks[27], (D_MODEL,)),
    }


def reference(x_prompt, x_sample, state_C, state_n, state_m, cache_k, cache_v, cache_logf, page_table,
              meta_tokens, norm_a, w_in_a, b_ig_a, b_fg_a, mh_norm_a, w_out_a, norm_kv, w_kvf, b_fg_b,
              k_norm_b, norm_b, w_qo_b, q_norm_b, w_out_b, norm_ffn, w_gate_up, w_down, norm_final):
    prm = {'norm_a': norm_a, 'w_in_a': w_in_a, 'b_ig_a': b_ig_a, 'b_fg_a': b_fg_a,
           'mh_norm_a': mh_norm_a, 'w_out_a': w_out_a, 'norm_kv': norm_kv, 'w_kvf': w_kvf,
           'b_fg_b': b_fg_b, 'k_norm_b': k_norm_b, 'norm_b': norm_b, 'w_qo_b': w_qo_b,
           'q_norm_b': q_norm_b, 'w_out_b': w_out_b, 'norm_ffn': norm_ffn, 'w_gate_up': w_gate_up,
           'w_down': w_down, 'norm_final': norm_final}
    f32 = jnp.float32

    meta = jnp.broadcast_to(meta_tokens[None].astype(x_prompt.dtype), (BATCH, N_META, D_MODEL))
    h0 = jnp.concatenate([meta, x_prompt], axis=1)
    C0 = jnp.zeros((N_A_LAYERS, BATCH, H_A, DK_A, DV_A), f32)
    n0 = jnp.zeros((N_A_LAYERS, BATCH, H_A, DK_A), f32)
    m0 = jnp.zeros((N_A_LAYERS, BATCH, H_A), f32)

    def attend_prompt(q, k, v, lf):
        c = jnp.cumsum(lf, axis=1)
        pos = jnp.arange(q.shape[1], dtype=jnp.int32)
        return fox_attention(q, c, pos, k, v, c, pos)

    yp, p_C, p_n, p_m, (p_k, p_v, p_logf) = run_trunk(h0, C0, n0, m0, N_META, attend_prompt, prm)
    y_prompt = yp[:, N_META:]

    def attend_sample(q, k, v, lf):
        n_pages = PAST_LEN // PAGE_SIZE
        Bd, T = q.shape[:2]
        kp = cache_k[page_table].reshape(Bd, n_pages * PAGE_SIZE, H_B, DH_B)
        vp = cache_v[page_table].reshape(Bd, n_pages * PAGE_SIZE, H_B, DH_B)
        lp = cache_logf[page_table].reshape(Bd, n_pages * PAGE_SIZE, H_B)
        kc = jnp.concatenate([kp, k.astype(kp.dtype)], axis=1)
        vc = jnp.concatenate([vp, v.astype(vp.dtype)], axis=1)
        c = jnp.cumsum(jnp.concatenate([lp.astype(f32), lf], axis=1), axis=1)
        pos_k = jnp.arange(PAST_LEN + T, dtype=jnp.int32)
        pos_q = PAST_LEN + jnp.arange(T, dtype=jnp.int32)
        return fox_attention(q, c[:, PAST_LEN:], pos_q, kc, vc, c, pos_k)

    y_sample, s_C, s_n, s_m, (s_k, s_v, s_logf) = run_trunk(x_sample, state_C, state_n, state_m, DEC_SEQ,
                                                           attend_sample, prm)
    return (y_prompt, y_sample, p_C, p_n, p_m, p_k, p_v, p_logf, s_C, s_n, s_m, s_k, s_v, s_logf)
```

```python
import functools

import jax
import jax.numpy as jnp
from jax import lax
from jax.experimental import pallas as pl
from jax.experimental.pallas import tpu as pltpu

F32 = jnp.float32
BF16 = jnp.bfloat16

N_META = 16
H_A = 8
DK_A = 64
DV_A = 128
GATE_CAP = 15.0
H_B = 16
DH_B = 64
EPS = 1e-6

LANES = 128
SUBLANES = 8
VMEM_LIMIT_BYTES = 56 * 1024 * 1024

MLSTM_CHUNK = 128
MLSTM_HEADS_PER_STEP = 2
ATTN_TILE = 256
FF_CHUNK = 256
MASKED_GATE = -1e30
NEG_INIT = -1e30


def _cparams(*sem):
    return pltpu.CompilerParams(dimension_semantics=sem, vmem_limit_bytes=VMEM_LIMIT_BYTES)


def _const_spec(shape):
    nd = len(shape)
    return pl.BlockSpec(shape, lambda *_: (0,) * nd, pipeline_mode=pl.Buffered(1))


def _rms_scale(x):
    return lax.rsqrt(jnp.mean(x * x, axis=-1, keepdims=True) + EPS)


def _log_sigmoid(x):
    return jnp.minimum(x, 0.0) - jnp.log1p(jnp.exp(-jnp.abs(x)))


def _sigmoid(x):
    return 1.0 / (1.0 + jnp.exp(-x))


def _dot(a, b):
    return jnp.dot(a, b, preferred_element_type=F32)


def _dot_nt(a, b):
    return lax.dot_general(a, b, (((1,), (1,)), ((), ())), preferred_element_type=F32)


def _split3(x):
    hi = x.astype(BF16)
    r1 = x - hi.astype(F32)
    mid = r1.astype(BF16)
    lo = (r1 - mid.astype(F32)).astype(BF16)
    return hi, mid, lo


def _dot_by_01(x, m01):
    hi, mid, lo = _split3(x)
    return _dot(hi, m01) + _dot(mid, m01) + _dot(lo, m01)


def _iota(shape, dim):
    return lax.broadcasted_iota(jnp.int32, shape, dim)


def _proj_in_body(x_ref, g_ref, wqk_ref, wv_ref, wog_ref, wg_ref, bg_ref, qs_ref,
                  qk_ref, v_ref, og_ref, gt_ref):
    x = x_ref[...]
    xn = (x * _rms_scale(x) * g_ref[...]).astype(BF16)
    qk_ref[...] = (_dot(xn, wqk_ref[...]) * qs_ref[...]).astype(BF16)
    v_ref[...] = _dot(xn, wv_ref[...]).astype(BF16)
    og_ref[...] = _dot(xn, wog_ref[...]).astype(BF16)
    z = _dot(xn, wg_ref[...]) + bg_ref[...]
    cap = GATE_CAP * jnp.tanh(z / GATE_CAP)
    lane = _iota(cap.shape, 1)
    gt_ref[...] = jnp.where(lane < H_A, cap, _log_sigmoid(cap))


def _proj_in(x, g, wqk, wv, wog, wg, bg, qs, tm):
    n, d = x.shape
    row = lambda i: (i, 0)
    return pl.pallas_call(
        _proj_in_body,
        grid=(pl.cdiv(n, tm),),
        in_specs=[pl.BlockSpec((tm, d), row), _const_spec(g.shape), _const_spec(wqk.shape),
                  _const_spec(wv.shape), _const_spec(wog.shape), _const_spec(wg.shape),
                  _const_spec(bg.shape), _const_spec(qs.shape)],
        out_specs=[pl.BlockSpec((tm, wqk.shape[1]), row), pl.BlockSpec((tm, wv.shape[1]), row),
                   pl.BlockSpec((tm, wog.shape[1]), row), pl.BlockSpec((tm, LANES), row)],
        out_shape=[jax.ShapeDtypeStruct((n, wqk.shape[1]), BF16),
                   jax.ShapeDtypeStruct((n, wv.shape[1]), BF16),
                   jax.ShapeDtypeStruct((n, wog.shape[1]), BF16),
                   jax.ShapeDtypeStruct((n, LANES), F32)],
        compiler_params=_cparams("parallel"),
        name="proj_in",
    )(x, g, wqk, wv, wog, wg, bg, qs)


def _mix_ffn_body(a_ref, h_ref, wo_ref, gf_ref, wgu_ref, wd_ref, gout_ref, o_ref, *, d_ff, final):
    h1 = h_ref[...] + _dot(a_ref[...], wo_ref[...])
    xn = (h1 * _rms_scale(h1) * gf_ref[...]).astype(BF16)
    acc = h1
    for c in range(d_ff // FF_CHUNK):
        lo = c * FF_CHUNK
        gate = _dot(xn, wgu_ref[:, lo:lo + FF_CHUNK])
        up = _dot(xn, wgu_ref[:, d_ff + lo:d_ff + lo + FF_CHUNK])
        act = (gate * _sigmoid(gate) * up).astype(BF16)
        acc = acc + _dot(act, wd_ref[lo:lo + FF_CHUNK, :])
    if final:
        acc = acc * _rms_scale(acc) * gout_ref[...]
    o_ref[...] = acc


def _mix_ffn(a, h, wo, gf, wgu, wd, gout, tm, final):
    n, d = h.shape
    d_ff = wd.shape[0]
    row = lambda i: (i, 0)
    return pl.pallas_call(
        functools.partial(_mix_ffn_body, d_ff=d_ff, final=final),
        grid=(pl.cdiv(n, tm),),
        in_specs=[pl.BlockSpec((tm, a.shape[1]), row), pl.BlockSpec((tm, d), row),
                  _const_spec(wo.shape), _const_spec(gf.shape), _const_spec(wgu.shape),
                  _const_spec(wd.shape), _const_spec(gout.shape)],
        out_specs=pl.BlockSpec((tm, d), row),
        out_shape=jax.ShapeDtypeStruct((n, d), F32),
        compiler_params=_cparams("parallel"),
        name="mix_ffn_final" if final else "mix_ffn",
    )(a, h, wo, gf, wgu, wd, gout)


def _head_rmsnorm64(x):
    outs = []
    for j in range(x.shape[1] // LANES):
        blk = x[:, j * LANES:(j + 1) * LANES]
        sq = blk * blk
        lane = _iota(blk.shape, 1)
        s_all = jnp.sum(sq, axis=1, keepdims=True)
        s_lo = jnp.sum(jnp.where(lane < DH_B, sq, 0.0), axis=1, keepdims=True)
        ms = jnp.where(lane < DH_B, s_lo, s_all - s_lo) / DH_B
        outs.append(blk * lax.rsqrt(ms + EPS))
    return jnp.concatenate(outs, axis=1)


def _kvq_body(h_ref, gkv_ref, wk_ref, wv_ref, wf_ref, bf_ref, kg_ref, gb_ref, wq_ref, wog_ref,
              qg_ref, k_ref, v_ref, lf_ref, kb_ref, vb_ref, qb_ref, og_ref):
    h = h_ref[...]
    hr = h * _rms_scale(h)
    xs = (hr * gkv_ref[...]).astype(BF16)
    k = _head_rmsnorm64(_dot(xs, wk_ref[...])) * kg_ref[...]
    v = _dot(xs, wv_ref[...])
    f = _dot(xs, wf_ref[...]) + bf_ref[...]
    k_ref[...] = k
    v_ref[...] = v
    kb_ref[...] = k.astype(BF16)
    vb_ref[...] = v.astype(BF16)
    lf_ref[...] = _log_sigmoid(f)[:, :H_B]
    xq = (hr * gb_ref[...]).astype(BF16)
    qb_ref[...] = (_head_rmsnorm64(_dot(xq, wq_ref[...])) * qg_ref[...]).astype(BF16)
    og_ref[...] = _dot(xq, wog_ref[...]).astype(BF16)


def _kvq_proj(h, gkv, wk, wv, wf, bf, kg, gb, wq, wog, qg, tm):
    n, d = h.shape
    hd = wk.shape[1]
    row = lambda i: (i, 0)
    wide = pl.BlockSpec((tm, hd), row)
    return pl.pallas_call(
        _kvq_body,
        grid=(pl.cdiv(n, tm),),
        in_specs=[pl.BlockSpec((tm, d), row)] + [_const_spec(w.shape) for w in
                                                 (gkv, wk, wv, wf, bf, kg, gb, wq, wog, qg)],
        out_specs=[wide, wide, pl.BlockSpec((tm, H_B), row), wide, wide, wide,
                   pl.BlockSpec((tm, wog.shape[1]), row)],
        out_shape=[jax.ShapeDtypeStruct((n, hd), F32), jax.ShapeDtypeStruct((n, hd), F32),
                   jax.ShapeDtypeStruct((n, H_B), F32), jax.ShapeDtypeStruct((n, hd), BF16),
                   jax.ShapeDtypeStruct((n, hd), BF16), jax.ShapeDtypeStruct((n, hd), BF16),
                   jax.ShapeDtypeStruct((n, wog.shape[1]), BF16)],
        compiler_params=_cparams("parallel"),
        name="kvq_proj",
    )(h, gkv, wk, wv, wf, bf, kg, gb, wq, wog, qg)


def _col_to_row(col):
    n = col.shape[0]
    eye = _iota((n, n), 0) == _iota((n, n), 1)
    return jnp.sum(jnp.where(eye, col, 0.0), axis=0, keepdims=True)


def _row_to_col(row):
    n = row.shape[1]
    eye = _iota((n, n), 0) == _iota((n, n), 1)
    return jnp.sum(jnp.where(eye, row, 0.0), axis=1, keepdims=True)


def _mlstm_chunk(qk, v, li, lf, cfull, m):
    L = qk.shape[0]
    r = _iota((L, L), 0)
    c = _iota((L, L), 1)
    tril = r >= c
    lf_row = _col_to_row(lf)
    li_row = _col_to_row(li)
    b = jnp.sum(jnp.where(tril, lf_row, 0.0), axis=1, keepdims=True)
    b_row = _col_to_row(b)
    dmat = jnp.where(tril, b - b_row + li_row, -jnp.inf)
    inter = b + m
    m_t = jnp.maximum(inter, jnp.max(dmat, axis=1, keepdims=True))
    w_inter = jnp.exp(inter - m_t)

    lane = _iota((L, LANES), 1)
    q_lo = jnp.where(lane < DK_A, qk, 0.0).astype(BF16)
    kq = pltpu.roll(qk, DK_A, 1)
    s = _dot_nt(q_lo, kq.astype(BF16)) * jnp.exp(dmat - m_t)
    ones_col = jnp.where(lane == 0, 1.0, 0.0)
    vaug = jnp.concatenate([v, ones_col], axis=1).astype(BF16)
    tot = w_inter * _dot(q_lo, cfull.astype(BF16)) + _dot(s.astype(BF16), vaug)
    num = tot[:, :DV_A]
    den = tot[:, DV_A:DV_A + 1]
    h = num / jnp.maximum(jnp.abs(den), jnp.exp(-m_t))

    b_end = b[L - 1:L, :]
    g = b_end - b + li
    m_new = jnp.maximum(b_end + m, jnp.max(g, axis=0, keepdims=True))
    w_c = jnp.exp(b_end + m - m_new)
    w_g = jnp.exp(g - m_new)
    kg_t = (w_g * kq).T.astype(BF16)
    upd = _dot(kg_t, vaug)
    keep = _iota(cfull.shape, 0) < DK_A
    cfull_new = jnp.where(keep, w_c * cfull + upd, 0.0)
    return h, cfull_new, m_new


def _mlstm_head_out(h, og, gain):
    return h * _rms_scale(h) * gain * _sigmoid(og)


def _mlstm_prompt_body(qk_ref, v_ref, og_ref, gt_ref, mhg_ref, hg_ref, c_ref, n_ref, m_ref,
                       cst, mst, *, seq, hb):
    grp = pl.program_id(1)
    ch = MLSTM_CHUNK
    nfull, tail = seq // ch, seq % ch
    cst[...] = jnp.zeros(cst.shape, F32)
    mst[...] = jnp.zeros(mst.shape, F32)

    def chunk(r0, first_valid):
        gt = gt_ref[0, pl.ds(r0, ch), :]
        lane = _iota(gt.shape, 1)
        rowi = _iota((ch, 1), 0)
        outs = []
        for hh in range(hb):
            head = grp * hb + hh
            li = jnp.sum(jnp.where(lane == head, gt, 0.0), axis=1, keepdims=True)
            lf = jnp.sum(jnp.where(lane == head + H_A, gt, 0.0), axis=1, keepdims=True)
            if first_valid:
                li = jnp.where(rowi >= first_valid, li, MASKED_GATE)
                lf = jnp.where(rowi >= first_valid, lf, 0.0)
            sl = slice(hh * LANES, (hh + 1) * LANES)
            qk = qk_ref[0, pl.ds(r0, ch), sl].astype(F32)
            v = v_ref[0, pl.ds(r0, ch), sl].astype(F32)
            og = og_ref[0, pl.ds(r0, ch), sl].astype(F32)
            h, cnew, mnew = _mlstm_chunk(qk, v, li, lf, cst[hh], mst[hh, 0:1, 0:1])
            cst[hh] = cnew
            mst[hh] = jnp.broadcast_to(mnew, mst.shape[1:])
            outs.append(_mlstm_head_out(h, og, mhg_ref[hh]))
        return jnp.concatenate(outs, axis=1).astype(BF16)

    def loop_body(j, carry):
        r0 = pl.multiple_of(j * ch, ch)
        hg_ref[0, pl.ds(r0, ch), :] = chunk(r0, 0)
        return carry

    lax.fori_loop(0, nfull, loop_body, 0)
    if tail:
        out = chunk(seq - ch, ch - tail)
        hg_ref[0, seq - tail:seq, :] = out[ch - tail:, :]

    for hh in range(hb):
        cfull = cst[hh]
        c_ref[0, hh] = cfull[:DK_A, :DV_A]
        n_ref[0, hh] = _col_to_row(cfull[:DK_A, DV_A:DV_A + 1])
        m_ref[0, hh] = mst[hh, 0:1, 0:1]


def _mlstm_prompt(qk, v, og, gt, mhg, batch, seq):
    hb = MLSTM_HEADS_PER_STEP
    w = hb * LANES
    qk3, v3, og3 = (a.reshape(batch, seq, a.shape[-1]) for a in (qk, v, og))
    gt3 = gt.reshape(batch, seq, LANES)
    seq_blk = lambda b, g: (b, 0, g)
    hg, c, n, m = pl.pallas_call(
        functools.partial(_mlstm_prompt_body, seq=seq, hb=hb),
        grid=(batch, H_A // hb),
        in_specs=[pl.BlockSpec((1, seq, w), seq_blk), pl.BlockSpec((1, seq, w), seq_blk),
                  pl.BlockSpec((1, seq, w), seq_blk),
                  pl.BlockSpec((1, seq, LANES), lambda b, g: (b, 0, 0)),
                  pl.BlockSpec((hb, 1, DV_A), lambda b, g: (g, 0, 0))],
        out_specs=[pl.BlockSpec((1, seq, w), seq_blk),
                   pl.BlockSpec((1, hb, DK_A, DV_A), lambda b, g: (b, g, 0, 0)),
                   pl.BlockSpec((1, hb, 1, DK_A), lambda b, g: (b, g, 0, 0)),
                   pl.BlockSpec((1, hb, 1, 1), lambda b, g: (b, g, 0, 0))],
        out_shape=[jax.ShapeDtypeStruct((batch, seq, H_A * DV_A), BF16),
                   jax.ShapeDtypeStruct((batch, H_A, DK_A, DV_A), F32),
                   jax.ShapeDtypeStruct((batch, H_A, 1, DK_A), F32),
                   jax.ShapeDtypeStruct((batch, H_A, 1, 1), F32)],
        scratch_shapes=[pltpu.VMEM((hb, LANES, 2 * LANES), F32), pltpu.VMEM((hb, SUBLANES, LANES), F32)],
        compiler_params=_cparams("parallel", "arbitrary"),
        name="mlstm_prompt",
    )(qk3, v3, og3, gt3, mhg.reshape(H_A, 1, DV_A))
    return (hg.reshape(batch * seq, H_A * DV_A), c, n.reshape(batch, H_A, DK_A),
            m.reshape(batch, H_A))


def _mlstm_sample_body(qk_ref, v_ref, og_ref, gt_ref, mhg_ref, c0_ref, n0_ref, m0_ref,
                       hg_ref, c_ref, n_ref, m_ref, *, steps):
    ch = MLSTM_CHUNK
    pad = jnp.zeros((ch - steps, LANES), F32)
    rowi = _iota((ch, 1), 0)
    gt = jnp.concatenate([gt_ref[0], pad], axis=0)
    lane1 = _iota((1, LANES), 1)
    lane_c = _iota((DK_A, LANES), 1)
    m_out = jnp.zeros((1, LANES), F32)
    outs = []
    for hh in range(H_A):
        sl = slice(hh * LANES, (hh + 1) * LANES)
        li = jnp.where(rowi < steps, gt[:, hh:hh + 1], MASKED_GATE)
        lf = jnp.where(rowi < steps, gt[:, H_A + hh:H_A + hh + 1], 0.0)
        qk = jnp.concatenate([qk_ref[0, :, sl].astype(F32), pad], axis=0)
        v = jnp.concatenate([v_ref[0, :, sl].astype(F32), pad], axis=0)
        ncol = _row_to_col(n0_ref[0, hh:hh + 1, :])
        top = jnp.concatenate([c0_ref[0, hh], jnp.where(lane_c == 0, ncol, 0.0)], axis=1)
        cfull = jnp.concatenate([top, jnp.zeros((LANES - DK_A, 2 * LANES), F32)], axis=0)
        m0 = m0_ref[0, :, hh:hh + 1]
        h, cnew, mnew = _mlstm_chunk(qk, v, li, lf, cfull, m0)
        og = og_ref[0, :, sl].astype(F32)
        outs.append(_mlstm_head_out(h[:steps, :], og, mhg_ref[hh]))
        c_ref[0, hh] = cnew[:DK_A, :DV_A]
        n_ref[0, hh:hh + 1, :] = _col_to_row(cnew[:DK_A, DV_A:DV_A + 1])
        m_out = jnp.where(lane1 == hh, mnew, m_out)
    hg_ref[0] = jnp.concatenate(outs, axis=1).astype(BF16)
    m_ref[0] = m_out[:, :H_A]


def _mlstm_sample(qk, v, og, gt, mhg, c0, n0, m0, batch, steps):
    wide = H_A * LANES
    blk3 = lambda b: (b, 0, 0)
    hg, c, n, m = pl.pallas_call(
        functools.partial(_mlstm_sample_body, steps=steps),
        grid=(batch,),
        in_specs=[pl.BlockSpec((1, steps, wide), blk3), pl.BlockSpec((1, steps, wide), blk3),
                  pl.BlockSpec((1, steps, wide), blk3), pl.BlockSpec((1, steps, LANES), blk3),
                  _const_spec((H_A, 1, DV_A)),
                  pl.BlockSpec((1, H_A, DK_A, DV_A), lambda b: (b, 0, 0, 0)),
                  pl.BlockSpec((1, H_A, DK_A), blk3), pl.BlockSpec((1, 1, H_A), blk3)],
        out_specs=[pl.BlockSpec((1, steps, wide), blk3),
                   pl.BlockSpec((1, H_A, DK_A, DV_A), lambda b: (b, 0, 0, 0)),
                   pl.BlockSpec((1, H_A, DK_A), blk3), pl.BlockSpec((1, 1, H_A), blk3)],
        out_shape=[jax.ShapeDtypeStruct((batch, steps, wide), BF16),
                   jax.ShapeDtypeStruct((batch, H_A, DK_A, DV_A), F32),
                   jax.ShapeDtypeStruct((batch, H_A, DK_A), F32),
                   jax.ShapeDtypeStruct((batch, 1, H_A), F32)],
        compiler_params=_cparams("parallel"),
        name="mlstm_sample",
    )(qk.reshape(batch, steps, wide), v.reshape(batch, steps, wide), og.reshape(batch, steps, wide),
      gt.reshape(batch, steps, LANES), mhg.reshape(H_A, 1, DV_A), c0, n0, m0.reshape(batch, 1, H_A))
    return hg.reshape(batch * steps, wide), c, n, m.reshape(batch, H_A)


def _rows_to_lanes(x16, staging_ref):
    staging_ref[...] = jnp.zeros(staging_ref.shape, F32)
    staging_ref[:, 0:x16.shape[1]] = x16
    return staging_ref[...].T[0:x16.shape[1], :]


def _cumsum_body(lf_ref, c_ref, stage, carry, *, nchunk):
    carry[...] = jnp.zeros(carry.shape, F32)
    u = _iota((LANES, LANES), 0)
    s = _iota((LANES, LANES), 1)
    incl = jnp.where(u <= s, 1.0, 0.0).astype(BF16)

    def body(j, _):
        r0 = pl.multiple_of(j * LANES, LANES)
        lft = _rows_to_lanes(lf_ref[0, pl.ds(r0, LANES), :], stage)
        c_ref[0, :, pl.ds(r0, LANES)] = _dot_by_01(lft, incl) + carry[:, 0:1]
        carry[...] = carry[...] + jnp.sum(lft, axis=1, keepdims=True)
        return 0

    lax.fori_loop(0, nchunk, body, 0)


def _forget_cumsum(lf, batch, seq):
    tpad = pl.cdiv(seq, LANES) * LANES
    lf3 = jnp.pad(lf.reshape(batch, seq, H_B), ((0, 0), (0, tpad - seq), (0, 0)))
    return pl.pallas_call(
        functools.partial(_cumsum_body, nchunk=tpad // LANES),
        grid=(batch,),
        in_specs=[pl.BlockSpec((1, tpad, H_B), lambda b: (b, 0, 0))],
        out_specs=pl.BlockSpec((1, H_B, tpad), lambda b: (b, 0, 0)),
        out_shape=jax.ShapeDtypeStruct((batch, H_B, tpad), F32),
        scratch_shapes=[pltpu.VMEM((LANES, LANES), F32), pltpu.VMEM((H_B, LANES), F32)],
        compiler_params=_cparams("parallel"),
        name="forget_cumsum",
    )(lf3)


def _attn_prompt_body(q_ref, k_ref, v_ref, og_ref, c_ref, o_ref, acc, mrun, lrun, *, seq):
    pair = pl.program_id(1)
    t = ATTN_TILE
    nfull, tail = seq // t, seq % t
    lane = _iota((t, LANES), 1)
    lo_half = lane < DH_B

    def tile(qh, q0, k0, masked, min_key):
        kt = k_ref[0, pl.ds(k0, t), :]
        vt = v_ref[0, pl.ds(k0, t), :]
        pvs, alphas = [], []
        for hh in range(2):
            s = _dot_nt(qh[hh], kt) - c_ref[0, pl.ds(pair * 2 + hh, 1), pl.ds(k0, t)]
            if masked:
                qpos = q0 + _iota((t, t), 0)
                kpos = k0 + _iota((t, t), 1)
                s = jnp.where((kpos <= qpos) & (kpos >= min_key), s, -jnp.inf)
            m_prev = mrun[hh]
            m_new = jnp.maximum(m_prev, jnp.max(s, axis=1, keepdims=True))
            alpha = jnp.exp(m_prev - m_new)
            p = jnp.exp(s - m_new)
            lrun[hh] = alpha * lrun[hh] + jnp.sum(p, axis=1, keepdims=True)
            mrun[hh] = m_new
            pvs.append(_dot(p.astype(BF16), vt))
            alphas.append(alpha)
        acc[...] = jnp.where(lo_half, alphas[0], alphas[1]) * acc[...] + jnp.where(lo_half, pvs[0], pvs[1])

    def q_tile(q0, n_unmasked, diag_k0, min_key, first_row):
        q = q_ref[0, pl.ds(q0, t), :].astype(F32)
        qh = [jnp.where(lo_half, q, 0.0).astype(BF16), jnp.where(lo_half, 0.0, q).astype(BF16)]
        acc[...] = jnp.zeros(acc.shape, F32)
        mrun[...] = jnp.full(mrun.shape, NEG_INIT, F32)
        lrun[...] = jnp.zeros(lrun.shape, F32)

        def kbody(j, _):
            tile(qh, q0, pl.multiple_of(j * t, t), False, 0)
            return 0

        lax.fori_loop(0, n_unmasked, kbody, 0)
        tile(qh, q0, diag_k0, True, min_key)
        inv = jnp.where(lo_half, 1.0 / lrun[0], 1.0 / lrun[1])
        out = (acc[...] * inv * _sigmoid(og_ref[0, pl.ds(q0, t), :].astype(F32))).astype(BF16)
        if first_row:
            o_ref[0, q0 + first_row:q0 + t, :] = out[first_row:, :]
        else:
            o_ref[0, pl.ds(q0, t), :] = out

    def qbody(i, _):
        q0 = pl.multiple_of(i * t, t)
        q_tile(q0, i, q0, 0, 0)
        return 0

    lax.fori_loop(0, nfull, qbody, 0)
    if tail:
        q_tile(seq - t, nfull, seq - t, nfull * t, t - tail)


def _attn_prompt(q, k, v, og, c_row, batch, seq):
    wide = q.shape[1]
    blk = lambda b, p: (b, 0, p)
    q3, k3, v3, og3 = (a.reshape(batch, seq, wide) for a in (q, k, v, og))
    tpad = c_row.shape[2]
    t = ATTN_TILE
    out = pl.pallas_call(
        functools.partial(_attn_prompt_body, seq=seq),
        grid=(batch, H_B // 2),
        in_specs=[pl.BlockSpec((1, seq, LANES), blk)] * 4
        + [pl.BlockSpec((1, H_B, tpad), lambda b, p: (b, 0, 0))],
        out_specs=pl.BlockSpec((1, seq, LANES), blk),
        out_shape=jax.ShapeDtypeStruct((batch, seq, wide), BF16),
        scratch_shapes=[pltpu.VMEM((t, LANES), F32), pltpu.VMEM((2, t, 1), F32),
                        pltpu.VMEM((2, t, 1), F32)],
        compiler_params=_cparams("parallel", "arbitrary"),
        name="attn_prompt",
    )(q3, k3, v3, og3, c_row)
    return out.reshape(batch * seq, wide)


def _expand_heads(x16, reps):
    return jnp.concatenate([jnp.broadcast_to(x16[h:h + 1, :], (reps, x16.shape[1]))
                            for h in range(x16.shape[0])], axis=0)


def _attn_sample_body(pt_ref, q_ref, kn_ref, vn_ref, lfn_ref, og_ref, kp_ref, vp_ref, lfp_ref,
                      o_ref, qbd, acc, mrun, lrun, carry, stage, *, steps):
    j = pl.program_id(1)
    npages = pl.num_programs(1)
    rows = H_B * steps
    wide = H_B * DH_B
    u = _iota((LANES, LANES), 0)
    s_ = _iota((LANES, LANES), 1)

    def diag():
        return _iota((rows, wide), 0) // steps == _iota((rows, wide), 1) // DH_B

    def fold(k, v, bias, keys_minor):
        s = (_dot(qbd[...], k) if keys_minor else _dot_nt(qbd[...], k)) + bias
        m_prev = mrun[...]
        m_new = jnp.maximum(m_prev, jnp.max(s, axis=1, keepdims=True))
        alpha = jnp.exp(m_prev - m_new)
        p = jnp.exp(s - m_new)
        lrun[...] = alpha * lrun[...] + jnp.sum(p, axis=1, keepdims=True)
        mrun[...] = m_new
        pb = p.astype(BF16)
        acc[...] = alpha * acc[...] + (_dot_nt(pb, v) if keys_minor else _dot(pb, v))

    @pl.when(j == 0)
    def _():
        q = q_ref[0].astype(F32)
        qrep = jnp.concatenate([q] * H_B, axis=0)
        qbd[...] = jnp.where(diag(), qrep, 0.0).astype(BF16)
        acc[...] = jnp.zeros(acc.shape, F32)
        mrun[...] = jnp.full(mrun.shape, NEG_INIT, F32)
        lrun[...] = jnp.zeros(lrun.shape, F32)
        carry[...] = jnp.zeros(carry.shape, F32)
        zpad = jnp.zeros((LANES - steps, wide), F32)
        kt = jnp.concatenate([kn_ref[0].astype(F32), zpad], axis=0).astype(BF16)
        vt = jnp.concatenate([vn_ref[0].astype(F32), zpad], axis=0).astype(BF16)
        lfn = jnp.concatenate([lfn_ref[0], jnp.zeros((LANES - steps, H_B), F32)], axis=0)
        incl = jnp.where(u <= s_, 1.0, 0.0).astype(BF16)
        cnew = _dot_by_01(_rows_to_lanes(lfn, stage), incl)
        key = _iota((rows, LANES), 1)
        qry = _iota((rows, LANES), 0) % steps
        bias = jnp.where(key <= qry, -_expand_heads(cnew, steps), -jnp.inf)
        fold(kt, vt, bias, False)

    lft = lfp_ref[0]
    later = jnp.where(u > s_, 1.0, 0.0).astype(BF16)
    bias16 = _dot_by_01(lft, later) + carry[:, 0:1]
    carry[...] = carry[...] + jnp.sum(lft, axis=1, keepdims=True)
    fold(kp_ref[0].astype(BF16), vp_ref[0].astype(BF16), _expand_heads(bias16, steps), True)

    @pl.when(j == npages - 1)
    def _():
        full = jnp.where(diag(), acc[...] / lrun[...], 0.0)
        out = full[0:steps, :]
        for h in range(1, H_B):
            out = out + full[h * steps:(h + 1) * steps, :]
        o_ref[0] = (out * _sigmoid(og_ref[0].astype(F32))).astype(BF16)


def _attn_sample(q, kn, vn, lfn, og, cache_k, cache_v, cache_logf, page_table, batch, steps):
    wide = H_B * DH_B
    n_phys, page = cache_k.shape[0], cache_k.shape[1]
    npages = page_table.shape[1]
    assert page == LANES
    ck = jnp.transpose(cache_k, (0, 2, 3, 1)).reshape(n_phys, wide, page)
    cv = jnp.transpose(cache_v, (0, 2, 3, 1)).reshape(n_phys, wide, page)
    clf = jnp.transpose(cache_logf, (0, 2, 1))
    rows = H_B * steps
    tok = lambda b, j, pt: (b, 0, 0)
    pg = lambda b, j, pt: (pt[b, npages - 1 - j], 0, 0)
    out = pl.pallas_call(
        functools.partial(_attn_sample_body, steps=steps),
        grid_spec=pltpu.PrefetchScalarGridSpec(
            num_scalar_prefetch=1,
            grid=(batch, npages),
            in_specs=[pl.BlockSpec((1, steps, wide), tok), pl.BlockSpec((1, steps, wide), tok),
                      pl.BlockSpec((1, steps, wide), tok), pl.BlockSpec((1, steps, H_B), tok),
                      pl.BlockSpec((1, steps, wide), tok),
                      pl.BlockSpec((1, wide, page), pg), pl.BlockSpec((1, wide, page), pg),
                      pl.BlockSpec((1, H_B, page), pg)],
            out_specs=pl.BlockSpec((1, steps, wide), tok),
            scratch_shapes=[pltpu.VMEM((rows, wide), BF16), pltpu.VMEM((rows, wide), F32),
                            pltpu.VMEM((rows, 1), F32), pltpu.VMEM((rows, 1), F32),
                            pltpu.VMEM((H_B, LANES), F32), pltpu.VMEM((LANES, LANES), F32)]),
        out_shape=jax.ShapeDtypeStruct((batch, steps, wide), BF16),
        compiler_params=_cparams("parallel", "arbitrary"),
        name="attn_sample",
    )(page_table, q.reshape(batch, steps, wide), kn.reshape(batch, steps, wide),
      vn.reshape(batch, steps, wide), lfn.reshape(batch, steps, H_B), og.reshape(batch, steps, wide),
      ck, cv, clf)
    return out.reshape(batch * steps, wide)


def _prep_params(norm_a, w_in_a, b_ig_a, b_fg_a, mh_norm_a, w_out_a, norm_kv, w_kvf, b_fg_b,
                 k_norm_b, norm_b, w_qo_b, q_norm_b, w_out_b, norm_ffn, w_gate_up, w_down,
                 norm_final):
    d = w_in_a.shape[1]
    hk, hv, hd = H_A * DK_A, H_A * DV_A, H_B * DH_B
    w_in = w_in_a[0]
    wq = w_in[:, :hk].reshape(d, H_A, DK_A)
    wk = w_in[:, hk:2 * hk].reshape(d, H_A, DK_A)
    row = lambda a: a.reshape(1, -1).astype(F32)
    pad_cols = lambda a: jnp.pad(a, ((0, 0), (0, LANES - a.shape[1])))
    lane = jnp.arange(H_A * LANES) % LANES
    return dict(
        norm_a=row(norm_a[0]),
        wqk=jnp.concatenate([wq, wk], axis=2).reshape(d, H_A * LANES).astype(BF16),
        wv=w_in[:, 2 * hk:2 * hk + hv].astype(BF16),
        wog=w_in[:, 2 * hk + hv:2 * hk + 2 * hv].astype(BF16),
        wg=pad_cols(w_in[:, 2 * hk + 2 * hv:]).astype(BF16),
        bg=pad_cols(jnp.concatenate([b_ig_a[0], b_fg_a[0]]).reshape(1, -1).astype(F32)),
        qs=jnp.where(lane < DK_A, DK_A ** -0.5, 1.0).reshape(1, -1).astype(F32),
        mhg=mh_norm_a[0].astype(F32),
        wo_a=w_out_a[0].astype(BF16),
        gkv=row(norm_kv),
        wk=w_kvf[:, :hd].astype(BF16),
        wvs=w_kvf[:, hd:2 * hd].astype(BF16),
        wf=pad_cols(w_kvf[:, 2 * hd:]).astype(BF16),
        bf=pad_cols(b_fg_b.reshape(1, -1).astype(F32)),
        kg=row(jnp.tile(k_norm_b, H_B)),
        gb=row(norm_b[0]),
        wq=w_qo_b[0][:, :hd].astype(BF16),
        wog_b=w_qo_b[0][:, hd:].astype(BF16),
        qg=row(jnp.tile(q_norm_b[0], H_B)) * DH_B ** -0.5,
        wo_b=w_out_b[0].astype(BF16),
        gf=[row(norm_ffn[l]) for l in range(2)],
        wgu=[w_gate_up[l].astype(BF16) for l in range(2)],
        wd=[w_down[l].astype(BF16) for l in range(2)],
        gfin=row(norm_final),
    )


def _layer0(h, p, tm, mlstm):
    qk, v, og, gt = _proj_in(h, p["norm_a"], p["wqk"], p["wv"], p["wog"], p["wg"], p["bg"], p["qs"], tm)
    hg, c, n, m = mlstm(qk, v, og, gt)
    h2 = _mix_ffn(hg, h, p["wo_a"], p["gf"][0], p["wgu"][0], p["wd"][0], p["gfin"], tm, False)
    return h2, c, n, m


def _shared_and_q(h2, p, tm):
    return _kvq_proj(h2, p["gkv"], p["wk"], p["wvs"], p["wf"], p["bf"], p["kg"], p["gb"],
                     p["wq"], p["wog_b"], p["qg"], tm)


def _layer1_tail(o, h2, p, tm):
    return _mix_ffn(o, h2, p["wo_b"], p["gf"][1], p["wgu"][1], p["wd"][1], p["gfin"], tm, True)


def kernel(x_prompt, x_sample, state_C, state_n, state_m, cache_k, cache_v, cache_logf, page_table,
           meta_tokens, norm_a, w_in_a, b_ig_a, b_fg_a, mh_norm_a, w_out_a, norm_kv, w_kvf, b_fg_b,
           k_norm_b, norm_b, w_qo_b, q_norm_b, w_out_b, norm_ffn, w_gate_up, w_down, norm_final):
    assert w_in_a.shape[0] == 1 and w_qo_b.shape[0] == 1 and norm_ffn.shape[0] == 2
    p = _prep_params(norm_a, w_in_a, b_ig_a, b_fg_a, mh_norm_a, w_out_a, norm_kv, w_kvf, b_fg_b,
                     k_norm_b, norm_b, w_qo_b, q_norm_b, w_out_b, norm_ffn, w_gate_up, w_down,
                     norm_final)
    bp, sp, d = x_prompt.shape
    bs, ss, _ = x_sample.shape
    tp = sp + N_META
    hd = H_B * DH_B
    tm = 512

    meta = jnp.broadcast_to(meta_tokens[None].astype(F32), (bp, N_META, d))
    h0 = jnp.concatenate([meta, x_prompt], axis=1).reshape(bp * tp, d)
    h2, p_c, p_n, p_m = _layer0(h0, p, tm, functools.partial(_mlstm_prompt, mhg=p["mhg"], batch=bp, seq=tp))
    k, v, lf, kb, vb, qb, og = _shared_and_q(h2, p, tm)
    c_row = _forget_cumsum(lf, bp, tp)
    o = _attn_prompt(qb, kb, vb, og, c_row, bp, tp)
    y = _layer1_tail(o, h2, p, tm)
    y_prompt = y.reshape(bp, tp, d)[:, N_META:]
    p_k = k.reshape(bp, tp, H_B, DH_B)
    p_v = v.reshape(bp, tp, H_B, DH_B)
    p_lf = lf.reshape(bp, tp, H_B)

    hs0 = x_sample.reshape(bs * ss, d)
    hs2, s_c, s_n, s_m = _layer0(
        hs0, p, tm, functools.partial(_mlstm_sample, mhg=p["mhg"], c0=state_C[0], n0=state_n[0],
                                      m0=state_m[0], batch=bs, steps=ss))
    ks, vs, lfs, kbs, vbs, qbs, ogs = _shared_and_q(hs2, p, tm)
    os_ = _attn_sample(qbs, kbs, vbs, lfs, ogs, cache_k, cache_v, cache_logf, page_table, bs, ss)
    y_sample = _layer1_tail(os_, hs2, p, tm).reshape(bs, ss, d)

    return (y_prompt, y_sample, p_c[None], p_n[None], p_m[None], p_k, p_v, p_lf,
            s_c[None], s_n[None], s_m[None], ks.reshape(bs, ss, H_B, DH_B),
            vs.reshape(bs, ss, H_B, DH_B), lfs.reshape(bs, ss, H_B))
```

```python
import functools

import jax
import jax.numpy as jnp
from jax import lax
from jax.experimental import pallas as pl
from jax.experimental.pallas import tpu as pltpu

F32 = jnp.float32
BF16 = jnp.bfloat16

N_META = 16
H_A = 8
DK_A = 64
DV_A = 128
GATE_CAP = 15.0
H_B = 16
DH_B = 64
EPS = 1e-6

LANES = 128
SUBLANES = 8
VMEM_LIMIT_BYTES = 56 * 1024 * 1024

MLSTM_CHUNK = 128
MLSTM_HEADS_PER_STEP = 2
ATTN_TQ = 512
ATTN_TK = 256
KVQ_T_TILE = 384
N_SPLIT = 3
FF_CHUNK = 256
MASKED_GATE = -1e30
NEG_INIT = -1e30


def _cparams(*sem):
    return pltpu.CompilerParams(dimension_semantics=sem, vmem_limit_bytes=VMEM_LIMIT_BYTES)


def _const_spec(shape):
    nd = len(shape)
    return pl.BlockSpec(shape, lambda *_: (0,) * nd, pipeline_mode=pl.Buffered(1))


def _rms_scale(x):
    return lax.rsqrt(jnp.mean(x * x, axis=-1, keepdims=True) + EPS)


def _log_sigmoid(x):
    return jnp.minimum(x, 0.0) - jnp.log1p(jnp.exp(-jnp.abs(x)))


def _sigmoid(x):
    return 1.0 / (1.0 + jnp.exp(-x))


def _dot(a, b):
    return jnp.dot(a, b, preferred_element_type=F32)


def _dot_nt(a, b):
    return lax.dot_general(a, b, (((1,), (1,)), ((), ())), preferred_element_type=F32)


def _split3(x):
    hi = x.astype(BF16)
    r1 = x - hi.astype(F32)
    mid = r1.astype(BF16)
    lo = (r1 - mid.astype(F32)).astype(BF16)
    return hi, mid, lo


def _dot_by_01(x, m01):
    hi, mid, lo = _split3(x)
    return _dot(hi, m01) + _dot(mid, m01) + _dot(lo, m01)


def _iota(shape, dim):
    return lax.broadcasted_iota(jnp.int32, shape, dim)


def _proj_in_body(x_ref, g_ref, wqk_ref, wv_ref, wog_ref, wg_ref, bg_ref, qs_ref,
                  qk_ref, v_ref, og_ref, gt_ref):
    x = x_ref[...]
    xn = (x * _rms_scale(x) * g_ref[...]).astype(BF16)
    qk_ref[...] = (_dot(xn, wqk_ref[...]) * qs_ref[...]).astype(BF16)
    v_ref[...] = _dot(xn, wv_ref[...]).astype(BF16)
    og_ref[...] = _dot(xn, wog_ref[...]).astype(BF16)
    z = _dot(xn, wg_ref[...]) + bg_ref[...]
    cap = GATE_CAP * jnp.tanh(z / GATE_CAP)
    lane = _iota(cap.shape, 1)
    gt_ref[...] = jnp.where(lane < H_A, cap, _log_sigmoid(cap))


def _proj_in(x, g, wqk, wv, wog, wg, bg, qs, tm):
    n, d = x.shape
    row = lambda i: (i, 0)
    return pl.pallas_call(
        _proj_in_body,
        grid=(pl.cdiv(n, tm),),
        in_specs=[pl.BlockSpec((tm, d), row), _const_spec(g.shape), _const_spec(wqk.shape),
                  _const_spec(wv.shape), _const_spec(wog.shape), _const_spec(wg.shape),
                  _const_spec(bg.shape), _const_spec(qs.shape)],
        out_specs=[pl.BlockSpec((tm, wqk.shape[1]), row), pl.BlockSpec((tm, wv.shape[1]), row),
                   pl.BlockSpec((tm, wog.shape[1]), row), pl.BlockSpec((tm, LANES), row)],
        out_shape=[jax.ShapeDtypeStruct((n, wqk.shape[1]), BF16),
                   jax.ShapeDtypeStruct((n, wv.shape[1]), BF16),
                   jax.ShapeDtypeStruct((n, wog.shape[1]), BF16),
                   jax.ShapeDtypeStruct((n, LANES), F32)],
        compiler_params=_cparams("parallel"),
        name="proj_in",
    )(x, g, wqk, wv, wog, wg, bg, qs)


def _mix_ffn_body(a_ref, h_ref, wo_ref, gf_ref, wgu_ref, wd_ref, gout_ref, o_ref, *, d_ff, final):
    h1 = h_ref[...] + _dot(a_ref[...], wo_ref[...])
    xn = (h1 * _rms_scale(h1) * gf_ref[...]).astype(BF16)
    acc = h1
    for c in range(d_ff // FF_CHUNK):
        lo = c * FF_CHUNK
        gate = _dot(xn, wgu_ref[:, lo:lo + FF_CHUNK])
        up = _dot(xn, wgu_ref[:, d_ff + lo:d_ff + lo + FF_CHUNK])
        act = (gate * _sigmoid(gate) * up).astype(BF16)
        acc = acc + _dot(act, wd_ref[lo:lo + FF_CHUNK, :])
    if final:
        acc = acc * _rms_scale(acc) * gout_ref[...]
    o_ref[...] = acc


def _mix_ffn(a, h, wo, gf, wgu, wd, gout, tm, final):
    n, d = h.shape
    d_ff = wd.shape[0]
    row = lambda i: (i, 0)
    return pl.pallas_call(
        functools.partial(_mix_ffn_body, d_ff=d_ff, final=final),
        grid=(pl.cdiv(n, tm),),
        in_specs=[pl.BlockSpec((tm, a.shape[1]), row), pl.BlockSpec((tm, d), row),
                  _const_spec(wo.shape), _const_spec(gf.shape), _const_spec(wgu.shape),
                  _const_spec(wd.shape), _const_spec(gout.shape)],
        out_specs=pl.BlockSpec((tm, d), row),
        out_shape=jax.ShapeDtypeStruct((n, d), F32),
        compiler_params=_cparams("parallel"),
        name="mix_ffn_final" if final else "mix_ffn",
    )(a, h, wo, gf, wgu, wd, gout)


def _head_rmsnorm64(x):
    outs = []
    for j in range(x.shape[1] // LANES):
        blk = x[:, j * LANES:(j + 1) * LANES]
        sq = blk * blk
        lane = _iota(blk.shape, 1)
        s_all = jnp.sum(sq, axis=1, keepdims=True)
        s_lo = jnp.sum(jnp.where(lane < DH_B, sq, 0.0), axis=1, keepdims=True)
        ms = jnp.where(lane < DH_B, s_lo, s_all - s_lo) / DH_B
        outs.append(blk * lax.rsqrt(ms + EPS))
    return jnp.concatenate(outs, axis=1)


def _kvq_compute(h, gkv_ref, wk_ref, wv_ref, wf_ref, bf_ref, kg_ref, gb_ref, wq_ref, wog_ref, qg_ref):
    hr = h * _rms_scale(h)
    xs = (hr * gkv_ref[...]).astype(BF16)
    k = _head_rmsnorm64(_dot(xs, wk_ref[...])) * kg_ref[...]
    v = _dot(xs, wv_ref[...])
    lf = _log_sigmoid(_dot(xs, wf_ref[...]) + bf_ref[...])
    xq = (hr * gb_ref[...]).astype(BF16)
    q = _head_rmsnorm64(_dot(xq, wq_ref[...])) * qg_ref[...]
    og = _dot(xq, wog_ref[...])
    return k, v, lf, q, og


def _kvq_body(h_ref, gkv_ref, wk_ref, wv_ref, wf_ref, bf_ref, kg_ref, gb_ref, wq_ref, wog_ref,
              qg_ref, k_ref, v_ref, lf_ref, kb_ref, vb_ref, qb_ref, og_ref):
    k, v, lf, q, og = _kvq_compute(h_ref[...], gkv_ref, wk_ref, wv_ref, wf_ref, bf_ref, kg_ref,
                                   gb_ref, wq_ref, wog_ref, qg_ref)
    k_ref[...] = k
    v_ref[...] = v
    kb_ref[...] = k.astype(BF16)
    vb_ref[...] = v.astype(BF16)
    lf_ref[...] = lf[:, :H_B]
    qb_ref[...] = q.astype(BF16)
    og_ref[...] = og.astype(BF16)


def _kvq_t_body(h_ref, gkv_ref, wk_ref, wv_ref, wf_ref, bf_ref, kg_ref, gb_ref, wq_ref, wog_ref,
                qg_ref, kt_ref, vt_ref, lft_ref, lf_ref, kb_ref, vtb_ref, qt_ref, og_ref, *, seq):
    tm = h_ref.shape[1]
    k, v, lf, q, og = _kvq_compute(h_ref[0], gkv_ref, wk_ref, wv_ref, wf_ref, bf_ref, kg_ref,
                                   gb_ref, wq_ref, wog_ref, qg_ref)
    valid = pl.program_id(1) * tm + _iota((tm, 1), 0) < seq
    k = jnp.where(valid, k, 0.0)
    v = jnp.where(valid, v, 0.0)
    lf = jnp.where(valid, lf, 0.0)
    vt = v.T
    kt_ref[0] = k.T
    vt_ref[0] = vt
    vtb_ref[0] = vt.astype(BF16)
    qt_ref[0] = jnp.where(valid, q, 0.0).T.astype(BF16)
    kb_ref[0] = k.astype(BF16)
    lf_ref[0] = lf[:, :H_B]
    lft_ref[0] = lf.T[:H_B, :]
    og_ref[0] = og.astype(BF16)


def _kvq_proj_t(h, gkv, wk, wv, wf, bf, kg, gb, wq, wog, qg, batch, seq):
    d = h.shape[1]
    hd = wk.shape[1]
    tm = KVQ_T_TILE
    nt = pl.cdiv(seq, tm)
    tpad = nt * tm
    rows = lambda b, i: (b, i, 0)
    cols = lambda b, i: (b, 0, i)
    return pl.pallas_call(
        functools.partial(_kvq_t_body, seq=seq),
        grid=(batch, nt),
        in_specs=[pl.BlockSpec((1, tm, d), rows)] + [_const_spec(w.shape) for w in
                                                      (gkv, wk, wv, wf, bf, kg, gb, wq, wog, qg)],
        out_specs=[pl.BlockSpec((1, hd, tm), cols), pl.BlockSpec((1, hd, tm), cols),
                   pl.BlockSpec((1, H_B, tm), cols), pl.BlockSpec((1, tm, H_B), rows),
                   pl.BlockSpec((1, tm, hd), rows), pl.BlockSpec((1, hd, tm), cols),
                   pl.BlockSpec((1, hd, tm), cols), pl.BlockSpec((1, tm, wog.shape[1]), rows)],
        out_shape=[jax.ShapeDtypeStruct((batch, hd, seq), F32), jax.ShapeDtypeStruct((batch, hd, seq), F32),
                   jax.ShapeDtypeStruct((batch, H_B, seq), F32), jax.ShapeDtypeStruct((batch, tpad, H_B), F32),
                   jax.ShapeDtypeStruct((batch, tpad, hd), BF16), jax.ShapeDtypeStruct((batch, hd, tpad), BF16),
                   jax.ShapeDtypeStruct((batch, hd, tpad), BF16),
                   jax.ShapeDtypeStruct((batch, seq, wog.shape[1]), BF16)],
        compiler_params=_cparams("parallel", "parallel"),
        name="kvq_proj_t",
    )(h.reshape(batch, seq, d), gkv, wk, wv, wf, bf, kg, gb, wq, wog, qg)


def _kvq_proj(h, gkv, wk, wv, wf, bf, kg, gb, wq, wog, qg, tm):
    n, d = h.shape
    hd = wk.shape[1]
    row = lambda i: (i, 0)
    wide = pl.BlockSpec((tm, hd), row)
    return pl.pallas_call(
        _kvq_body,
        grid=(pl.cdiv(n, tm),),
        in_specs=[pl.BlockSpec((tm, d), row)] + [_const_spec(w.shape) for w in
                                                 (gkv, wk, wv, wf, bf, kg, gb, wq, wog, qg)],
        out_specs=[wide, wide, pl.BlockSpec((tm, H_B), row), wide, wide, wide,
                   pl.BlockSpec((tm, wog.shape[1]), row)],
        out_shape=[jax.ShapeDtypeStruct((n, hd), F32), jax.ShapeDtypeStruct((n, hd), F32),
                   jax.ShapeDtypeStruct((n, H_B), F32), jax.ShapeDtypeStruct((n, hd), BF16),
                   jax.ShapeDtypeStruct((n, hd), BF16), jax.ShapeDtypeStruct((n, hd), BF16),
                   jax.ShapeDtypeStruct((n, wog.shape[1]), BF16)],
        compiler_params=_cparams("parallel"),
        name="kvq_proj",
    )(h, gkv, wk, wv, wf, bf, kg, gb, wq, wog, qg)


def _col_to_row(col):
    n = col.shape[0]
    eye = _iota((n, n), 0) == _iota((n, n), 1)
    return jnp.sum(jnp.where(eye, col, 0.0), axis=0, keepdims=True)


def _row_to_col(row):
    n = row.shape[1]
    eye = _iota((n, n), 0) == _iota((n, n), 1)
    return jnp.sum(jnp.where(eye, row, 0.0), axis=1, keepdims=True)


def _mlstm_chunk(qk, v, li, lf, cfull, m):
    L = qk.shape[0]
    r = _iota((L, L), 0)
    c = _iota((L, L), 1)
    tril = r >= c
    lf_row = _col_to_row(lf)
    li_row = _col_to_row(li)
    b = jnp.sum(jnp.where(tril, lf_row, 0.0), axis=1, keepdims=True)
    b_row = _col_to_row(b)
    dmat = jnp.where(tril, b - b_row + li_row, -jnp.inf)
    inter = b + m
    m_t = jnp.maximum(inter, jnp.max(dmat, axis=1, keepdims=True))
    w_inter = jnp.exp(inter - m_t)

    lane = _iota((L, LANES), 1)
    q_lo = jnp.where(lane < DK_A, qk, 0.0).astype(BF16)
    kq = pltpu.roll(qk, DK_A, 1)
    s = _dot_nt(q_lo, kq.astype(BF16)) * jnp.exp(dmat - m_t)
    ones_col = jnp.where(lane == 0, 1.0, 0.0)
    vaug = jnp.concatenate([v, ones_col], axis=1).astype(BF16)
    tot = w_inter * _dot(q_lo, cfull.astype(BF16)) + _dot(s.astype(BF16), vaug)
    num = tot[:, :DV_A]
    den = tot[:, DV_A:DV_A + 1]
    h = num / jnp.maximum(jnp.abs(den), jnp.exp(-m_t))

    b_end = b[L - 1:L, :]
    g = b_end - b + li
    m_new = jnp.maximum(b_end + m, jnp.max(g, axis=0, keepdims=True))
    w_c = jnp.exp(b_end + m - m_new)
    w_g = jnp.exp(g - m_new)
    kg_t = (w_g * kq).T.astype(BF16)
    upd = _dot(kg_t, vaug)
    keep = _iota(cfull.shape, 0) < DK_A
    cfull_new = jnp.where(keep, w_c * cfull + upd, 0.0)
    return h, cfull_new, m_new


def _mlstm_head_out(h, og, gain):
    return h * _rms_scale(h) * gain * _sigmoid(og)


def _mlstm_prompt_body(qk_ref, v_ref, og_ref, gt_ref, mhg_ref, hg_ref, c_ref, n_ref, m_ref,
                       cst, mst, *, seq, hb):
    grp = pl.program_id(1)
    ch = MLSTM_CHUNK
    nfull, tail = seq // ch, seq % ch
    cst[...] = jnp.zeros(cst.shape, F32)
    mst[...] = jnp.zeros(mst.shape, F32)

    def chunk(r0, first_valid):
        gt = gt_ref[0, pl.ds(r0, ch), :]
        lane = _iota(gt.shape, 1)
        rowi = _iota((ch, 1), 0)
        outs = []
        for hh in range(hb):
            head = grp * hb + hh
            li = jnp.sum(jnp.where(lane == head, gt, 0.0), axis=1, keepdims=True)
            lf = jnp.sum(jnp.where(lane == head + H_A, gt, 0.0), axis=1, keepdims=True)
            if first_valid:
                li = jnp.where(rowi >= first_valid, li, MASKED_GATE)
                lf = jnp.where(rowi >= first_valid, lf, 0.0)
            sl = slice(hh * LANES, (hh + 1) * LANES)
            qk = qk_ref[0, pl.ds(r0, ch), sl].astype(F32)
            v = v_ref[0, pl.ds(r0, ch), sl].astype(F32)
            og = og_ref[0, pl.ds(r0, ch), sl].astype(F32)
            h, cnew, mnew = _mlstm_chunk(qk, v, li, lf, cst[hh], mst[hh, 0:1, 0:1])
            cst[hh] = cnew
            mst[hh] = jnp.broadcast_to(mnew, mst.shape[1:])
            outs.append(_mlstm_head_out(h, og, mhg_ref[hh]))
        return jnp.concatenate(outs, axis=1).astype(BF16)

    def loop_body(j, carry):
        r0 = pl.multiple_of(j * ch, ch)
        hg_ref[0, pl.ds(r0, ch), :] = chunk(r0, 0)
        return carry

    lax.fori_loop(0, nfull, loop_body, 0)
    if tail:
        out = chunk(seq - ch, ch - tail)
        hg_ref[0, seq - tail:seq, :] = out[ch - tail:, :]

    for hh in range(hb):
        cfull = cst[hh]
        c_ref[0, hh] = cfull[:DK_A, :DV_A]
        n_ref[0, hh] = _col_to_row(cfull[:DK_A, DV_A:DV_A + 1])
        m_ref[0, hh] = mst[hh, 0:1, 0:1]


def _mlstm_prompt(qk, v, og, gt, mhg, batch, seq):
    hb = MLSTM_HEADS_PER_STEP
    w = hb * LANES
    qk3, v3, og3 = (a.reshape(batch, seq, a.shape[-1]) for a in (qk, v, og))
    gt3 = gt.reshape(batch, seq, LANES)
    seq_blk = lambda b, g: (b, 0, g)
    hg, c, n, m = pl.pallas_call(
        functools.partial(_mlstm_prompt_body, seq=seq, hb=hb),
        grid=(batch, H_A // hb),
        in_specs=[pl.BlockSpec((1, seq, w), seq_blk), pl.BlockSpec((1, seq, w), seq_blk),
                  pl.BlockSpec((1, seq, w), seq_blk),
                  pl.BlockSpec((1, seq, LANES), lambda b, g: (b, 0, 0)),
                  pl.BlockSpec((hb, 1, DV_A), lambda b, g: (g, 0, 0))],
        out_specs=[pl.BlockSpec((1, seq, w), seq_blk),
                   pl.BlockSpec((1, hb, DK_A, DV_A), lambda b, g: (b, g, 0, 0)),
                   pl.BlockSpec((1, hb, 1, DK_A), lambda b, g: (b, g, 0, 0)),
                   pl.BlockSpec((1, hb, 1, 1), lambda b, g: (b, g, 0, 0))],
        out_shape=[jax.ShapeDtypeStruct((batch, seq, H_A * DV_A), BF16),
                   jax.ShapeDtypeStruct((batch, H_A, DK_A, DV_A), F32),
                   jax.ShapeDtypeStruct((batch, H_A, 1, DK_A), F32),
                   jax.ShapeDtypeStruct((batch, H_A, 1, 1), F32)],
        scratch_shapes=[pltpu.VMEM((hb, LANES, 2 * LANES), F32), pltpu.VMEM((hb, SUBLANES, LANES), F32)],
        compiler_params=_cparams("parallel", "arbitrary"),
        name="mlstm_prompt",
    )(qk3, v3, og3, gt3, mhg.reshape(H_A, 1, DV_A))
    return (hg.reshape(batch * seq, H_A * DV_A), c, n.reshape(batch, H_A, DK_A),
            m.reshape(batch, H_A))


def _mlstm_sample_body(qk_ref, v_ref, og_ref, gt_ref, mhg_ref, c0_ref, n0_ref, m0_ref,
                       hg_ref, c_ref, n_ref, m_ref, *, steps):
    ch = MLSTM_CHUNK
    pad = jnp.zeros((ch - steps, LANES), F32)
    rowi = _iota((ch, 1), 0)
    gt = jnp.concatenate([gt_ref[0], pad], axis=0)
    lane1 = _iota((1, LANES), 1)
    lane_c = _iota((DK_A, LANES), 1)
    m_out = jnp.zeros((1, LANES), F32)
    outs = []
    for hh in range(H_A):
        sl = slice(hh * LANES, (hh + 1) * LANES)
        li = jnp.where(rowi < steps, gt[:, hh:hh + 1], MASKED_GATE)
        lf = jnp.where(rowi < steps, gt[:, H_A + hh:H_A + hh + 1], 0.0)
        qk = jnp.concatenate([qk_ref[0, :, sl].astype(F32), pad], axis=0)
        v = jnp.concatenate([v_ref[0, :, sl].astype(F32), pad], axis=0)
        ncol = _row_to_col(n0_ref[0, hh:hh + 1, :])
        top = jnp.concatenate([c0_ref[0, hh], jnp.where(lane_c == 0, ncol, 0.0)], axis=1)
        cfull = jnp.concatenate([top, jnp.zeros((LANES - DK_A, 2 * LANES), F32)], axis=0)
        m0 = m0_ref[0, :, hh:hh + 1]
        h, cnew, mnew = _mlstm_chunk(qk, v, li, lf, cfull, m0)
        og = og_ref[0, :, sl].astype(F32)
        outs.append(_mlstm_head_out(h[:steps, :], og, mhg_ref[hh]))
        c_ref[0, hh] = cnew[:DK_A, :DV_A]
        n_ref[0, hh:hh + 1, :] = _col_to_row(cnew[:DK_A, DV_A:DV_A + 1])
        m_out = jnp.where(lane1 == hh, mnew, m_out)
    hg_ref[0] = jnp.concatenate(outs, axis=1).astype(BF16)
    m_ref[0] = m_out[:, :H_A]


def _mlstm_sample(qk, v, og, gt, mhg, c0, n0, m0, batch, steps):
    wide = H_A * LANES
    blk3 = lambda b: (b, 0, 0)
    hg, c, n, m = pl.pallas_call(
        functools.partial(_mlstm_sample_body, steps=steps),
        grid=(batch,),
        in_specs=[pl.BlockSpec((1, steps, wide), blk3), pl.BlockSpec((1, steps, wide), blk3),
                  pl.BlockSpec((1, steps, wide), blk3), pl.BlockSpec((1, steps, LANES), blk3),
                  _const_spec((H_A, 1, DV_A)),
                  pl.BlockSpec((1, H_A, DK_A, DV_A), lambda b: (b, 0, 0, 0)),
                  pl.BlockSpec((1, H_A, DK_A), blk3), pl.BlockSpec((1, 1, H_A), blk3)],
        out_specs=[pl.BlockSpec((1, steps, wide), blk3),
                   pl.BlockSpec((1, H_A, DK_A, DV_A), lambda b: (b, 0, 0, 0)),
                   pl.BlockSpec((1, H_A, DK_A), blk3), pl.BlockSpec((1, 1, H_A), blk3)],
        out_shape=[jax.ShapeDtypeStruct((batch, steps, wide), BF16),
                   jax.ShapeDtypeStruct((batch, H_A, DK_A, DV_A), F32),
                   jax.ShapeDtypeStruct((batch, H_A, DK_A), F32),
                   jax.ShapeDtypeStruct((batch, 1, H_A), F32)],
        compiler_params=_cparams("parallel"),
        name="mlstm_sample",
    )(qk.reshape(batch, steps, wide), v.reshape(batch, steps, wide), og.reshape(batch, steps, wide),
      gt.reshape(batch, steps, LANES), mhg.reshape(H_A, 1, DV_A), c0, n0, m0.reshape(batch, 1, H_A))
    return hg.reshape(batch * steps, wide), c, n, m.reshape(batch, H_A)


def _rows_to_lanes(x16, staging_ref):
    staging_ref[...] = jnp.zeros(staging_ref.shape, F32)
    staging_ref[:, 0:x16.shape[1]] = x16
    return staging_ref[...].T[0:x16.shape[1], :]


def _placement():
    h = jnp.arange(H_B)[None, :, None]
    part = jnp.arange(N_SPLIT)[:, None, None]
    col = jnp.arange(H_B * DH_B)[None, None, :]
    lane = jnp.where(h % 2 == 0, DH_B, 0) + part
    return (col == (h // 2) * LANES + lane).astype(BF16)


def _key_aug_body(lf_ref, kb_ref, place_ref, k0_ref, k1_ref, carry):
    @pl.when(pl.program_id(1) == 0)
    def _():
        carry[...] = jnp.zeros(carry.shape, F32)

    tc, wide = kb_ref.shape[1], kb_ref.shape[2]
    tril = jnp.where(_iota((LANES, LANES), 0) >= _iota((LANES, LANES), 1), 1.0, 0.0).astype(BF16)
    lo_half = (_iota((LANES, wide), 1) & (LANES - 1)) < DH_B
    for sub in range(tc // LANES):
        rs = slice(sub * LANES, (sub + 1) * LANES)
        hi, mid, lo = _split3(lf_ref[0, rs, :])
        c = _dot(tril, hi) + _dot(tril, mid) + _dot(tril, lo) + carry[0:1, 0:H_B]
        carry[0:1, 0:H_B] = c[LANES - 1:LANES, :]
        parts = _split3(c)
        bias = sum(_dot(parts[j], place_ref[j]) for j in range(N_SPLIT))
        k = kb_ref[0, rs, :].astype(F32)
        k0_ref[0, rs, :] = jnp.where(lo_half, k, bias).astype(BF16)
        k1_ref[0, rs, :] = jnp.where(lo_half, bias, k).astype(BF16)


def _key_aug(lf, kb):
    batch, tpad, wide = kb.shape
    tc = KVQ_T_TILE
    blk = lambda b, i: (b, i, 0)
    place = _placement()
    return pl.pallas_call(
        _key_aug_body,
        grid=(batch, tpad // tc),
        in_specs=[pl.BlockSpec((1, tc, H_B), blk), pl.BlockSpec((1, tc, wide), blk),
                  _const_spec(place.shape)],
        out_specs=[pl.BlockSpec((1, tc, wide), blk), pl.BlockSpec((1, tc, wide), blk)],
        out_shape=[jax.ShapeDtypeStruct((batch, tpad, wide), BF16)] * 2,
        scratch_shapes=[pltpu.VMEM((SUBLANES, LANES), F32)],
        compiler_params=_cparams("parallel", "arbitrary"),
        name="key_aug",
    )(lf, kb, place)


def _attn_prompt_body(qt_ref, k0_ref, k1_ref, vt_ref, og_ref, o_ref, *, seq):
    tq, tk = ATTN_TQ, ATTN_TK
    tpad = qt_ref.shape[2]
    nfull = seq // tq
    kaug = (k0_ref, k1_ref)

    def fold(state, qa, q0, k0, nk, width, masked):
        new = []
        scores = [_dot(kaug[hh][0, pl.ds(k0, nk), :], qa[hh]) for hh in range(2)]
        for hh in range(2):
            m, l, acc = state[hh]
            s = scores[hh]
            if masked:
                kpos = k0 + _iota((nk, width), 0)
                qpos = q0 + _iota((nk, width), 1)
                s = jnp.where(kpos <= qpos, s, -jnp.inf)
            m_new = jnp.maximum(m, jnp.max(s, axis=0, keepdims=True))
            alpha = jnp.exp(m - m_new)
            p = jnp.exp(s - m_new)
            l = alpha * l + jnp.sum(p, axis=0, keepdims=True)
            vt = vt_ref[0, hh * DH_B:(hh + 1) * DH_B, pl.ds(k0, nk)]
            acc = alpha * acc + _dot(vt, p.astype(BF16))
            new.append((m_new, l, acc))
        return tuple(new)

    def q_tile(q0, width, n_unmasked, rows_out):
        qt = qt_ref[0, :, pl.ds(q0, width)].astype(F32)
        row = _iota(qt.shape, 0)
        qa = [jnp.where(row < DH_B, qt, jnp.where(row < DH_B + N_SPLIT, -1.0, 0.0)).astype(BF16),
              jnp.where(row >= DH_B, qt, jnp.where(row < N_SPLIT, -1.0, 0.0)).astype(BF16)]
        init = tuple((jnp.full((1, width), NEG_INIT, F32), jnp.zeros((1, width), F32),
                      jnp.zeros((DH_B, width), F32)) for _ in range(2))
        state = lax.fori_loop(
            0, n_unmasked,
            lambda j, st: fold(st, qa, q0, pl.multiple_of(j * tk, tk), tk, width, False), init)
        nk = min(tk, width)
        for d in range(width // nk):
            state = fold(state, qa, q0, q0 + d * nk, nk, width, True)
        out_t = jnp.concatenate([state[0][2] / state[0][1], state[1][2] / state[1][1]], axis=0)
        out = out_t.T[:rows_out, :]
        gate = _sigmoid(og_ref[0, pl.ds(q0, rows_out), :].astype(F32))
        o_ref[0, pl.ds(q0, rows_out), :] = (out * gate).astype(BF16)

    def qbody(i, _):
        q0 = pl.multiple_of(i * tq, tq)
        q_tile(q0, tq, i * (tq // tk), tq)
        return 0

    lax.fori_loop(0, nfull, qbody, 0)
    if seq > nfull * tq:
        q0 = nfull * tq
        q_tile(q0, tpad - q0, q0 // tk, seq - q0)


def _attn_prompt(qt, k0, k1, vt, og, seq):
    batch, wide, tpad = qt.shape
    tail_w = tpad - (seq // ATTN_TQ) * ATTN_TQ
    assert ATTN_TQ % ATTN_TK == 0 and tail_w <= ATTN_TQ and tail_w % min(ATTN_TK, tail_w) == 0
    rows = lambda b, p: (b, 0, p)
    cols = lambda b, p: (b, p, 0)
    out = pl.pallas_call(
        functools.partial(_attn_prompt_body, seq=seq),
        grid=(batch, H_B // 2),
        in_specs=[pl.BlockSpec((1, LANES, tpad), cols), pl.BlockSpec((1, tpad, LANES), rows),
                  pl.BlockSpec((1, tpad, LANES), rows), pl.BlockSpec((1, LANES, tpad), cols),
                  pl.BlockSpec((1, seq, LANES), rows)],
        out_specs=pl.BlockSpec((1, seq, LANES), rows),
        out_shape=jax.ShapeDtypeStruct((batch, seq, wide), BF16),
        compiler_params=_cparams("parallel", "parallel"),
        name="attn_prompt",
    )(qt, k0, k1, vt, og)
    return out.reshape(batch * seq, wide)


def _expand_heads(x16, reps):
    return jnp.concatenate([jnp.broadcast_to(x16[h:h + 1, :], (reps, x16.shape[1]))
                            for h in range(x16.shape[0])], axis=0)


def _attn_sample_body(pt_ref, q_ref, kn_ref, vn_ref, lfn_ref, og_ref, kp_ref, vp_ref, lfp_ref,
                      o_ref, qbd, acc, mrun, lrun, carry, stage, *, steps):
    j = pl.program_id(1)
    npages = pl.num_programs(1)
    rows = H_B * steps
    wide = H_B * DH_B
    u = _iota((LANES, LANES), 0)
    s_ = _iota((LANES, LANES), 1)

    def diag():
        return _iota((rows, wide), 0) // steps == _iota((rows, wide), 1) // DH_B

    def fold(k, v, bias, keys_minor):
        s = (_dot(qbd[...], k) if keys_minor else _dot_nt(qbd[...], k)) + bias
        m_prev = mrun[...]
        m_new = jnp.maximum(m_prev, jnp.max(s, axis=1, keepdims=True))
        alpha = jnp.exp(m_prev - m_new)
        p = jnp.exp(s - m_new)
        lrun[...] = alpha * lrun[...] + jnp.sum(p, axis=1, keepdims=True)
        mrun[...] = m_new
        pb = p.astype(BF16)
        acc[...] = alpha * acc[...] + (_dot_nt(pb, v) if keys_minor else _dot(pb, v))

    @pl.when(j == 0)
    def _():
        q = q_ref[0].astype(F32)
        qrep = jnp.concatenate([q] * H_B, axis=0)
        qbd[...] = jnp.where(diag(), qrep, 0.0).astype(BF16)
        acc[...] = jnp.zeros(acc.shape, F32)
        mrun[...] = jnp.full(mrun.shape, NEG_INIT, F32)
        lrun[...] = jnp.zeros(lrun.shape, F32)
        carry[...] = jnp.zeros(carry.shape, F32)
        zpad = jnp.zeros((LANES - steps, wide), F32)
        kt = jnp.concatenate([kn_ref[0].astype(F32), zpad], axis=0).astype(BF16)
        vt = jnp.concatenate([vn_ref[0].astype(F32), zpad], axis=0).astype(BF16)
        lfn = jnp.concatenate([lfn_ref[0], jnp.zeros((LANES - steps, H_B), F32)], axis=0)
        incl = jnp.where(u <= s_, 1.0, 0.0).astype(BF16)
        cnew = _dot_by_01(_rows_to_lanes(lfn, stage), incl)
        key = _iota((rows, LANES), 1)
        qry = _iota((rows, LANES), 0) % steps
        bias = jnp.where(key <= qry, -_expand_heads(cnew, steps), -jnp.inf)
        fold(kt, vt, bias, False)

    lft = lfp_ref[0]
    later = jnp.where(u > s_, 1.0, 0.0).astype(BF16)
    bias16 = _dot_by_01(lft, later) + carry[:, 0:1]
    carry[...] = carry[...] + jnp.sum(lft, axis=1, keepdims=True)
    fold(kp_ref[0].astype(BF16), vp_ref[0].astype(BF16), _expand_heads(bias16, steps), True)

    @pl.when(j == npages - 1)
    def _():
        full = jnp.where(diag(), acc[...] / lrun[...], 0.0)
        out = full[0:steps, :]
        for h in range(1, H_B):
            out = out + full[h * steps:(h + 1) * steps, :]
        o_ref[0] = (out * _sigmoid(og_ref[0].astype(F32))).astype(BF16)


def _attn_sample(q, kn, vn, lfn, og, cache_k, cache_v, cache_logf, page_table, batch, steps):
    wide = H_B * DH_B
    n_phys, page = cache_k.shape[0], cache_k.shape[1]
    npages = page_table.shape[1]
    assert page == LANES
    ck = jnp.transpose(cache_k, (0, 2, 3, 1)).reshape(n_phys, wide, page)
    cv = jnp.transpose(cache_v, (0, 2, 3, 1)).reshape(n_phys, wide, page)
    clf = jnp.transpose(cache_logf, (0, 2, 1))
    rows = H_B * steps
    tok = lambda b, j, pt: (b, 0, 0)
    pg = lambda b, j, pt: (pt[b, npages - 1 - j], 0, 0)
    out = pl.pallas_call(
        functools.partial(_attn_sample_body, steps=steps),
        grid_spec=pltpu.PrefetchScalarGridSpec(
            num_scalar_prefetch=1,
            grid=(batch, npages),
            in_specs=[pl.BlockSpec((1, steps, wide), tok), pl.BlockSpec((1, steps, wide), tok),
                      pl.BlockSpec((1, steps, wide), tok), pl.BlockSpec((1, steps, H_B), tok),
                      pl.BlockSpec((1, steps, wide), tok),
                      pl.BlockSpec((1, wide, page), pg), pl.BlockSpec((1, wide, page), pg),
                      pl.BlockSpec((1, H_B, page), pg)],
            out_specs=pl.BlockSpec((1, steps, wide), tok),
            scratch_shapes=[pltpu.VMEM((rows, wide), BF16), pltpu.VMEM((rows, wide), F32),
                            pltpu.VMEM((rows, 1), F32), pltpu.VMEM((rows, 1), F32),
                            pltpu.VMEM((H_B, LANES), F32), pltpu.VMEM((LANES, LANES), F32)]),
        out_shape=jax.ShapeDtypeStruct((batch, steps, wide), BF16),
        compiler_params=_cparams("parallel", "arbitrary"),
        name="attn_sample",
    )(page_table, q.reshape(batch, steps, wide), kn.reshape(batch, steps, wide),
      vn.reshape(batch, steps, wide), lfn.reshape(batch, steps, H_B), og.reshape(batch, steps, wide),
      ck, cv, clf)
    return out.reshape(batch * steps, wide)


def _prep_params(norm_a, w_in_a, b_ig_a, b_fg_a, mh_norm_a, w_out_a, norm_kv, w_kvf, b_fg_b,
                 k_norm_b, norm_b, w_qo_b, q_norm_b, w_out_b, norm_ffn, w_gate_up, w_down,
                 norm_final):
    d = w_in_a.shape[1]
    hk, hv, hd = H_A * DK_A, H_A * DV_A, H_B * DH_B
    w_in = w_in_a[0]
    wq = w_in[:, :hk].reshape(d, H_A, DK_A)
    wk = w_in[:, hk:2 * hk].reshape(d, H_A, DK_A)
    row = lambda a: a.reshape(1, -1).astype(F32)
    pad_cols = lambda a: jnp.pad(a, ((0, 0), (0, LANES - a.shape[1])))
    lane = jnp.arange(H_A * LANES) % LANES
    return dict(
        norm_a=row(norm_a[0]),
        wqk=jnp.concatenate([wq, wk], axis=2).reshape(d, H_A * LANES).astype(BF16),
        wv=w_in[:, 2 * hk:2 * hk + hv].astype(BF16),
        wog=w_in[:, 2 * hk + hv:2 * hk + 2 * hv].astype(BF16),
        wg=pad_cols(w_in[:, 2 * hk + 2 * hv:]).astype(BF16),
        bg=pad_cols(jnp.concatenate([b_ig_a[0], b_fg_a[0]]).reshape(1, -1).astype(F32)),
        qs=jnp.where(lane < DK_A, DK_A ** -0.5, 1.0).reshape(1, -1).astype(F32),
        mhg=mh_norm_a[0].astype(F32),
        wo_a=w_out_a[0].astype(BF16),
        gkv=row(norm_kv),
        wk=w_kvf[:, :hd].astype(BF16),
        wvs=w_kvf[:, hd:2 * hd].astype(BF16),
        wf=pad_cols(w_kvf[:, 2 * hd:]).astype(BF16),
        bf=pad_cols(b_fg_b.reshape(1, -1).astype(F32)),
        kg=row(jnp.tile(k_norm_b, H_B)),
        gb=row(norm_b[0]),
        wq=w_qo_b[0][:, :hd].astype(BF16),
        wog_b=w_qo_b[0][:, hd:].astype(BF16),
        qg=row(jnp.tile(q_norm_b[0], H_B)) * DH_B ** -0.5,
        wo_b=w_out_b[0].astype(BF16),
        gf=[row(norm_ffn[l]) for l in range(2)],
        wgu=[w_gate_up[l].astype(BF16) for l in range(2)],
        wd=[w_down[l].astype(BF16) for l in range(2)],
        gfin=row(norm_final),
    )


def _layer0(h, p, tm, mlstm):
    qk, v, og, gt = _proj_in(h, p["norm_a"], p["wqk"], p["wv"], p["wog"], p["wg"], p["bg"], p["qs"], tm)
    hg, c, n, m = mlstm(qk, v, og, gt)
    h2 = _mix_ffn(hg, h, p["wo_a"], p["gf"][0], p["wgu"][0], p["wd"][0], p["gfin"], tm, False)
    return h2, c, n, m


def _shared_and_q(h2, p, tm):
    return _kvq_proj(h2, p["gkv"], p["wk"], p["wvs"], p["wf"], p["bf"], p["kg"], p["gb"],
                     p["wq"], p["wog_b"], p["qg"], tm)


def _layer1_tail(o, h2, p, tm):
    return _mix_ffn(o, h2, p["wo_b"], p["gf"][1], p["wgu"][1], p["wd"][1], p["gfin"], tm, True)


def kernel(x_prompt, x_sample, state_C, state_n, state_m, cache_k, cache_v, cache_logf, page_table,
           meta_tokens, norm_a, w_in_a, b_ig_a, b_fg_a, mh_norm_a, w_out_a, norm_kv, w_kvf, b_fg_b,
           k_norm_b, norm_b, w_qo_b, q_norm_b, w_out_b, norm_ffn, w_gate_up, w_down, norm_final):
    assert w_in_a.shape[0] == 1 and w_qo_b.shape[0] == 1 and norm_ffn.shape[0] == 2
    p = _prep_params(norm_a, w_in_a, b_ig_a, b_fg_a, mh_norm_a, w_out_a, norm_kv, w_kvf, b_fg_b,
                     k_norm_b, norm_b, w_qo_b, q_norm_b, w_out_b, norm_ffn, w_gate_up, w_down,
                     norm_final)
    bp, sp, d = x_prompt.shape
    bs, ss, _ = x_sample.shape
    tp = sp + N_META
    hd = H_B * DH_B
    tm = 512

    meta = jnp.broadcast_to(meta_tokens[None].astype(F32), (bp, N_META, d))
    h0 = jnp.concatenate([meta, x_prompt], axis=1).reshape(bp * tp, d)
    h2, p_c, p_n, p_m = _layer0(h0, p, tm, functools.partial(_mlstm_prompt, mhg=p["mhg"], batch=bp, seq=tp))
    kt, vt, lft, lf, kb, vtb, qt, og = _kvq_proj_t(
        h2, p["gkv"], p["wk"], p["wvs"], p["wf"], p["bf"], p["kg"], p["gb"], p["wq"], p["wog_b"],
        p["qg"], bp, tp)
    k0, k1 = _key_aug(lf, kb)
    o = _attn_prompt(qt, k0, k1, vtb, og, tp)
    y = _layer1_tail(o, h2, p, tm)
    y_prompt = y.reshape(bp, tp, d)[:, N_META:]
    p_k = jnp.transpose(kt.reshape(bp, H_B, DH_B, tp), (0, 3, 1, 2))
    p_v = jnp.transpose(vt.reshape(bp, H_B, DH_B, tp), (0, 3, 1, 2))
    p_lf = jnp.transpose(lft, (0, 2, 1))

    hs0 = x_sample.reshape(bs * ss, d)
    hs2, s_c, s_n, s_m = _layer0(
        hs0, p, tm, functools.partial(_mlstm_sample, mhg=p["mhg"], c0=state_C[0], n0=state_n[0],
                                      m0=state_m[0], batch=bs, steps=ss))
    ks, vs, lfs, kbs, vbs, qbs, ogs = _shared_and_q(hs2, p, tm)
    os_ = _attn_sample(qbs, kbs, vbs, lfs, ogs, cache_k, cache_v, cache_logf, page_table, bs, ss)
    y_sample = _layer1_tail(os_, hs2, p, tm).reshape(bs, ss, d)

    return (y_prompt, y_sample, p_c[None], p_n[None], p_m[None], p_k, p_v, p_lf,
            s_c[None], s_n[None], s_m[None], ks.reshape(bs, ss, H_B, DH_B),
            vs.reshape(bs, ss, H_B, DH_B), lfs.reshape(bs, ss, H_B))
```

```python
import functools

import jax
import jax.numpy as jnp
from jax import lax
from jax.experimental import pallas as pl
from jax.experimental.pallas import tpu as pltpu

F32 = jnp.float32
BF16 = jnp.bfloat16

N_META = 16
H_A = 8
DK_A = 64
DV_A = 128
GATE_CAP = 15.0
H_B = 16
DH_B = 64
EPS = 1e-6

LANES = 128
SUBLANES = 8
VMEM_LIMIT_BYTES = 56 * 1024 * 1024

MLSTM_CHUNK = 128
MLSTM_HEADS_PER_STEP = 4
SAMPLE_CHUNK_ALIGN = 8
ATTN_TQ = 512
ATTN_TK = 256
KVQ_T_TILE = 384
N_SPLIT = 3
FF_CHUNK = 256
MASKED_GATE = -1e30
NEG_INIT = -1e30


def _cparams(*sem):
    return pltpu.CompilerParams(dimension_semantics=sem, vmem_limit_bytes=VMEM_LIMIT_BYTES)


def _const_spec(shape):
    nd = len(shape)
    return pl.BlockSpec(shape, lambda *_: (0,) * nd, pipeline_mode=pl.Buffered(1))


def _rms_scale(x):
    return lax.rsqrt(jnp.mean(x * x, axis=-1, keepdims=True) + EPS)


def _log_sigmoid(x):
    return jnp.minimum(x, 0.0) - jnp.log1p(jnp.exp(-jnp.abs(x)))


def _sigmoid(x):
    return 1.0 / (1.0 + jnp.exp(-x))


def _dot(a, b):
    return jnp.dot(a, b, preferred_element_type=F32)


def _dot_nt(a, b):
    return lax.dot_general(a, b, (((1,), (1,)), ((), ())), preferred_element_type=F32)


def _dot_tn(a, b):
    return lax.dot_general(a, b, (((0,), (0,)), ((), ())), preferred_element_type=F32)


def _split3(x):
    hi = x.astype(BF16)
    r1 = x - hi.astype(F32)
    mid = r1.astype(BF16)
    lo = (r1 - mid.astype(F32)).astype(BF16)
    return hi, mid, lo


def _dot_by_01(x, m01):
    hi, mid, lo = _split3(x)
    return _dot(hi, m01) + _dot(mid, m01) + _dot(lo, m01)


def _iota(shape, dim):
    return lax.broadcasted_iota(jnp.int32, shape, dim)


def _proj_in_body(x_ref, g_ref, wqk_ref, wv_ref, wog_ref, wg_ref, bg_ref, qs_ref,
                  qk_ref, v_ref, og_ref, gt_ref):
    x = x_ref[...]
    xn = (x * _rms_scale(x) * g_ref[...]).astype(BF16)
    qk_ref[...] = (_dot(xn, wqk_ref[...]) * qs_ref[...]).astype(BF16)
    v_ref[...] = _dot(xn, wv_ref[...]).astype(BF16)
    og_ref[...] = _dot(xn, wog_ref[...]).astype(BF16)
    z = _dot(xn, wg_ref[...]) + bg_ref[...]
    cap = GATE_CAP * jnp.tanh(z / GATE_CAP)
    lane = _iota(cap.shape, 1)
    gt_ref[...] = jnp.where(lane < H_A, cap, _log_sigmoid(cap))


def _proj_in(x, g, wqk, wv, wog, wg, bg, qs, tm):
    n, d = x.shape
    row = lambda i: (i, 0)
    return pl.pallas_call(
        _proj_in_body,
        grid=(pl.cdiv(n, tm),),
        in_specs=[pl.BlockSpec((tm, d), row), _const_spec(g.shape), _const_spec(wqk.shape),
                  _const_spec(wv.shape), _const_spec(wog.shape), _const_spec(wg.shape),
                  _const_spec(bg.shape), _const_spec(qs.shape)],
        out_specs=[pl.BlockSpec((tm, wqk.shape[1]), row), pl.BlockSpec((tm, wv.shape[1]), row),
                   pl.BlockSpec((tm, wog.shape[1]), row), pl.BlockSpec((tm, LANES), row)],
        out_shape=[jax.ShapeDtypeStruct((n, wqk.shape[1]), BF16),
                   jax.ShapeDtypeStruct((n, wv.shape[1]), BF16),
                   jax.ShapeDtypeStruct((n, wog.shape[1]), BF16),
                   jax.ShapeDtypeStruct((n, LANES), F32)],
        compiler_params=_cparams("parallel"),
        name="proj_in",
    )(x, g, wqk, wv, wog, wg, bg, qs)


def _mix_ffn_body(a_ref, h_ref, wo_ref, gf_ref, wgu_ref, wd_ref, gout_ref, o_ref, *, d_ff, final):
    h1 = h_ref[...] + _dot(a_ref[...], wo_ref[...])
    xn = (h1 * _rms_scale(h1) * gf_ref[...]).astype(BF16)
    acc = h1
    for c in range(d_ff // FF_CHUNK):
        lo = c * FF_CHUNK
        gate = _dot(xn, wgu_ref[:, lo:lo + FF_CHUNK])
        up = _dot(xn, wgu_ref[:, d_ff + lo:d_ff + lo + FF_CHUNK])
        act = (gate * _sigmoid(gate) * up).astype(BF16)
        acc = acc + _dot(act, wd_ref[lo:lo + FF_CHUNK, :])
    if final:
        acc = acc * _rms_scale(acc) * gout_ref[...]
    o_ref[...] = acc


def _mix_ffn(a, h, wo, gf, wgu, wd, gout, tm, final):
    n, d = h.shape
    d_ff = wd.shape[0]
    row = lambda i: (i, 0)
    return pl.pallas_call(
        functools.partial(_mix_ffn_body, d_ff=d_ff, final=final),
        grid=(pl.cdiv(n, tm),),
        in_specs=[pl.BlockSpec((tm, a.shape[1]), row), pl.BlockSpec((tm, d), row),
                  _const_spec(wo.shape), _const_spec(gf.shape), _const_spec(wgu.shape),
                  _const_spec(wd.shape), _const_spec(gout.shape)],
        out_specs=pl.BlockSpec((tm, d), row),
        out_shape=jax.ShapeDtypeStruct((n, d), F32),
        compiler_params=_cparams("parallel"),
        name="mix_ffn_final" if final else "mix_ffn",
    )(a, h, wo, gf, wgu, wd, gout)


def _head_rmsnorm64(x):
    outs = []
    for j in range(x.shape[1] // LANES):
        blk = x[:, j * LANES:(j + 1) * LANES]
        sq = blk * blk
        lane = _iota(blk.shape, 1)
        s_all = jnp.sum(sq, axis=1, keepdims=True)
        s_lo = jnp.sum(jnp.where(lane < DH_B, sq, 0.0), axis=1, keepdims=True)
        ms = jnp.where(lane < DH_B, s_lo, s_all - s_lo) / DH_B
        outs.append(blk * lax.rsqrt(ms + EPS))
    return jnp.concatenate(outs, axis=1)


def _kvq_compute(h, gkv_ref, wk_ref, wv_ref, wf_ref, bf_ref, kg_ref, gb_ref, wq_ref, wog_ref, qg_ref):
    hr = h * _rms_scale(h)
    xs = (hr * gkv_ref[...]).astype(BF16)
    k = _head_rmsnorm64(_dot(xs, wk_ref[...])) * kg_ref[...]
    v = _dot(xs, wv_ref[...])
    lf = _log_sigmoid(_dot(xs, wf_ref[...]) + bf_ref[...])
    xq = (hr * gb_ref[...]).astype(BF16)
    q = _head_rmsnorm64(_dot(xq, wq_ref[...])) * qg_ref[...]
    og = _dot(xq, wog_ref[...])
    return k, v, lf, q, og


def _kvq_body(h_ref, gkv_ref, wk_ref, wv_ref, wf_ref, bf_ref, kg_ref, gb_ref, wq_ref, wog_ref,
              qg_ref, k_ref, v_ref, lf_ref, kb_ref, vb_ref, qb_ref, og_ref):
    k, v, lf, q, og = _kvq_compute(h_ref[...], gkv_ref, wk_ref, wv_ref, wf_ref, bf_ref, kg_ref,
                                   gb_ref, wq_ref, wog_ref, qg_ref)
    k_ref[...] = k
    v_ref[...] = v
    kb_ref[...] = k.astype(BF16)
    vb_ref[...] = v.astype(BF16)
    lf_ref[...] = lf[:, :H_B]
    qb_ref[...] = q.astype(BF16)
    og_ref[...] = og.astype(BF16)


def _kvq_t_body(h_ref, gkv_ref, wk_ref, wv_ref, wf_ref, bf_ref, kg_ref, gb_ref, wq_ref, wog_ref,
                qg_ref, kt_ref, vt_ref, lft_ref, lf_ref, kb_ref, vtb_ref, qt_ref, og_ref, *, seq):
    tm = h_ref.shape[1]
    k, v, lf, q, og = _kvq_compute(h_ref[0], gkv_ref, wk_ref, wv_ref, wf_ref, bf_ref, kg_ref,
                                   gb_ref, wq_ref, wog_ref, qg_ref)
    valid = pl.program_id(1) * tm + _iota((tm, 1), 0) < seq
    k = jnp.where(valid, k, 0.0)
    v = jnp.where(valid, v, 0.0)
    lf = jnp.where(valid, lf, 0.0)
    vt = v.T
    kt_ref[0] = k.T
    vt_ref[0] = vt
    vtb_ref[0] = vt.astype(BF16)
    qt_ref[0] = jnp.where(valid, q, 0.0).T.astype(BF16)
    kb_ref[0] = k.astype(BF16)
    lf_ref[0] = lf[:, :H_B]
    lft_ref[0] = lf.T[:H_B, :]
    og_ref[0] = og.astype(BF16)


def _kvq_proj_t(h, gkv, wk, wv, wf, bf, kg, gb, wq, wog, qg, batch, seq):
    d = h.shape[1]
    hd = wk.shape[1]
    tm = KVQ_T_TILE
    nt = pl.cdiv(seq, tm)
    tpad = nt * tm
    rows = lambda b, i: (b, i, 0)
    cols = lambda b, i: (b, 0, i)
    return pl.pallas_call(
        functools.partial(_kvq_t_body, seq=seq),
        grid=(batch, nt),
        in_specs=[pl.BlockSpec((1, tm, d), rows)] + [_const_spec(w.shape) for w in
                                                      (gkv, wk, wv, wf, bf, kg, gb, wq, wog, qg)],
        out_specs=[pl.BlockSpec((1, hd, tm), cols), pl.BlockSpec((1, hd, tm), cols),
                   pl.BlockSpec((1, H_B, tm), cols), pl.BlockSpec((1, tm, H_B), rows),
                   pl.BlockSpec((1, tm, hd), rows), pl.BlockSpec((1, hd, tm), cols),
                   pl.BlockSpec((1, hd, tm), cols), pl.BlockSpec((1, tm, wog.shape[1]), rows)],
        out_shape=[jax.ShapeDtypeStruct((batch, hd, seq), F32), jax.ShapeDtypeStruct((batch, hd, seq), F32),
                   jax.ShapeDtypeStruct((batch, H_B, seq), F32), jax.ShapeDtypeStruct((batch, tpad, H_B), F32),
                   jax.ShapeDtypeStruct((batch, tpad, hd), BF16), jax.ShapeDtypeStruct((batch, hd, tpad), BF16),
                   jax.ShapeDtypeStruct((batch, hd, tpad), BF16),
                   jax.ShapeDtypeStruct((batch, seq, wog.shape[1]), BF16)],
        compiler_params=_cparams("parallel", "parallel"),
        name="kvq_proj_t",
    )(h.reshape(batch, seq, d), gkv, wk, wv, wf, bf, kg, gb, wq, wog, qg)


def _kvq_proj(h, gkv, wk, wv, wf, bf, kg, gb, wq, wog, qg, tm):
    n, d = h.shape
    hd = wk.shape[1]
    row = lambda i: (i, 0)
    wide = pl.BlockSpec((tm, hd), row)
    return pl.pallas_call(
        _kvq_body,
        grid=(pl.cdiv(n, tm),),
        in_specs=[pl.BlockSpec((tm, d), row)] + [_const_spec(w.shape) for w in
                                                 (gkv, wk, wv, wf, bf, kg, gb, wq, wog, qg)],
        out_specs=[wide, wide, pl.BlockSpec((tm, H_B), row), wide, wide, wide,
                   pl.BlockSpec((tm, wog.shape[1]), row)],
        out_shape=[jax.ShapeDtypeStruct((n, hd), F32), jax.ShapeDtypeStruct((n, hd), F32),
                   jax.ShapeDtypeStruct((n, H_B), F32), jax.ShapeDtypeStruct((n, hd), BF16),
                   jax.ShapeDtypeStruct((n, hd), BF16), jax.ShapeDtypeStruct((n, hd), BF16),
                   jax.ShapeDtypeStruct((n, wog.shape[1]), BF16)],
        compiler_params=_cparams("parallel"),
        name="kvq_proj",
    )(h, gkv, wk, wv, wf, bf, kg, gb, wq, wog, qg)


def _col_to_row(col):
    n = col.shape[0]
    eye = _iota((n, n), 0) == _iota((n, n), 1)
    return jnp.sum(jnp.where(eye, col, 0.0), axis=0, keepdims=True)


def _row_to_col(row):
    n = row.shape[1]
    eye = _iota((n, n), 0) == _iota((n, n), 1)
    return jnp.sum(jnp.where(eye, row, 0.0), axis=1, keepdims=True)


def _mlstm_chunk(qk, v, li, lf, cfull, m):
    L = qk.shape[0]
    r = _iota((L, L), 0)
    c = _iota((L, L), 1)
    tril = r >= c
    lf_row = _col_to_row(lf)
    li_row = _col_to_row(li)
    b = jnp.sum(jnp.where(tril, lf_row, 0.0), axis=1, keepdims=True)
    b_row = _col_to_row(b)
    dmat = jnp.where(tril, b - b_row + li_row, -jnp.inf)
    inter = b + m
    m_t = jnp.maximum(inter, jnp.max(dmat, axis=1, keepdims=True))
    w_inter = jnp.exp(inter - m_t)

    lane = _iota((L, LANES), 1)
    q_lo = jnp.where(lane < DK_A, qk, 0.0).astype(BF16)
    kq = pltpu.roll(qk, DK_A, 1)
    s = _dot_nt(q_lo, kq.astype(BF16)) * jnp.exp(dmat - m_t)
    ones_col = jnp.where(lane == 0, 1.0, 0.0)
    vaug = jnp.concatenate([v, ones_col], axis=1).astype(BF16)
    tot = w_inter * _dot(q_lo, cfull.astype(BF16)) + _dot(s.astype(BF16), vaug)
    num = tot[:, :DV_A]
    den = tot[:, DV_A:DV_A + 1]
    h = num / jnp.maximum(jnp.abs(den), jnp.exp(-m_t))

    b_end = b[L - 1:L, :]
    g = b_end - b + li
    m_new = jnp.maximum(b_end + m, jnp.max(g, axis=0, keepdims=True))
    w_c = jnp.exp(b_end + m - m_new)
    w_g = jnp.exp(g - m_new)
    upd = _dot_tn((w_g * kq).astype(BF16), vaug)
    keep = _iota(cfull.shape, 0) < DK_A
    cfull_new = jnp.where(keep, w_c * cfull + upd, 0.0)
    return h, cfull_new, m_new


def _mlstm_head_out(h, og, gain):
    return h * _rms_scale(h) * gain * _sigmoid(og)


def _mlstm_prompt_body(qk_ref, v_ref, og_ref, gt_ref, mhg_ref, hg_ref, c_ref, n_ref, m_ref,
                       cst, mst, *, seq, hb):
    grp = pl.program_id(1)
    ch = MLSTM_CHUNK
    nfull, tail = seq // ch, seq % ch
    cst[...] = jnp.zeros(cst.shape, F32)
    mst[...] = jnp.zeros(mst.shape, F32)

    def chunk(r0, first_valid):
        gt = gt_ref[0, pl.ds(r0, ch), :]
        lane = _iota(gt.shape, 1)
        rowi = _iota((ch, 1), 0)
        outs = []
        for hh in range(hb):
            head = grp * hb + hh
            li = jnp.sum(jnp.where(lane == head, gt, 0.0), axis=1, keepdims=True)
            lf = jnp.sum(jnp.where(lane == head + H_A, gt, 0.0), axis=1, keepdims=True)
            if first_valid:
                li = jnp.where(rowi >= first_valid, li, MASKED_GATE)
                lf = jnp.where(rowi >= first_valid, lf, 0.0)
            sl = slice(hh * LANES, (hh + 1) * LANES)
            qk = qk_ref[0, pl.ds(r0, ch), sl].astype(F32)
            v = v_ref[0, pl.ds(r0, ch), sl].astype(F32)
            og = og_ref[0, pl.ds(r0, ch), sl].astype(F32)
            h, cnew, mnew = _mlstm_chunk(qk, v, li, lf, cst[hh], mst[hh, 0:1, 0:1])
            cst[hh] = cnew
            mst[hh] = jnp.broadcast_to(mnew, mst.shape[1:])
            outs.append(_mlstm_head_out(h, og, mhg_ref[hh]))
        return jnp.concatenate(outs, axis=1).astype(BF16)

    def loop_body(j, carry):
        r0 = pl.multiple_of(j * ch, ch)
        hg_ref[0, pl.ds(r0, ch), :] = chunk(r0, 0)
        return carry

    lax.fori_loop(0, nfull, loop_body, 0)
    if tail:
        out = chunk(seq - ch, ch - tail)
        hg_ref[0, seq - tail:seq, :] = out[ch - tail:, :]

    for hh in range(hb):
        cfull = cst[hh]
        c_ref[0, hh] = cfull[:DK_A, :DV_A]
        n_ref[0, hh] = _col_to_row(cfull[:DK_A, DV_A:DV_A + 1])
        m_ref[0, hh] = mst[hh, 0:1, 0:1]


def _mlstm_prompt(qk, v, og, gt, mhg, batch, seq):
    hb = MLSTM_HEADS_PER_STEP
    w = hb * LANES
    qk3, v3, og3 = (a.reshape(batch, seq, a.shape[-1]) for a in (qk, v, og))
    gt3 = gt.reshape(batch, seq, LANES)
    seq_blk = lambda b, g: (b, 0, g)
    hg, c, n, m = pl.pallas_call(
        functools.partial(_mlstm_prompt_body, seq=seq, hb=hb),
        grid=(batch, H_A // hb),
        in_specs=[pl.BlockSpec((1, seq, w), seq_blk), pl.BlockSpec((1, seq, w), seq_blk),
                  pl.BlockSpec((1, seq, w), seq_blk),
                  pl.BlockSpec((1, seq, LANES), lambda b, g: (b, 0, 0)),
                  pl.BlockSpec((hb, 1, DV_A), lambda b, g: (g, 0, 0))],
        out_specs=[pl.BlockSpec((1, seq, w), seq_blk),
                   pl.BlockSpec((1, hb, DK_A, DV_A), lambda b, g: (b, g, 0, 0)),
                   pl.BlockSpec((1, hb, 1, DK_A), lambda b, g: (b, g, 0, 0)),
                   pl.BlockSpec((1, hb, 1, 1), lambda b, g: (b, g, 0, 0))],
        out_shape=[jax.ShapeDtypeStruct((batch, seq, H_A * DV_A), BF16),
                   jax.ShapeDtypeStruct((batch, H_A, DK_A, DV_A), F32),
                   jax.ShapeDtypeStruct((batch, H_A, 1, DK_A), F32),
                   jax.ShapeDtypeStruct((batch, H_A, 1, 1), F32)],
        scratch_shapes=[pltpu.VMEM((hb, LANES, 2 * LANES), F32), pltpu.VMEM((hb, SUBLANES, LANES), F32)],
        compiler_params=_cparams("parallel", "arbitrary"),
        name="mlstm_prompt",
    )(qk3, v3, og3, gt3, mhg.reshape(H_A, 1, DV_A))
    return (hg.reshape(batch * seq, H_A * DV_A), c, n.reshape(batch, H_A, DK_A),
            m.reshape(batch, H_A))


def _mlstm_sample_body(qk_ref, v_ref, og_ref, gt_ref, mhg_ref, c0_ref, n0_ref, m0_ref,
                       hg_ref, c_ref, n_ref, m_ref, *, steps):
    ch = pl.cdiv(steps, SAMPLE_CHUNK_ALIGN) * SAMPLE_CHUNK_ALIGN
    pad = jnp.zeros((ch - steps, LANES), F32)
    padded = (lambda a: jnp.concatenate([a, pad], axis=0)) if ch > steps else (lambda a: a)
    rowi = _iota((ch, 1), 0)
    gt = padded(gt_ref[0])
    lane1 = _iota((1, LANES), 1)
    lane_c = _iota((DK_A, LANES), 1)
    m_out = jnp.zeros((1, LANES), F32)
    outs = []
    for hh in range(H_A):
        sl = slice(hh * LANES, (hh + 1) * LANES)
        li = jnp.where(rowi < steps, gt[:, hh:hh + 1], MASKED_GATE)
        lf = jnp.where(rowi < steps, gt[:, H_A + hh:H_A + hh + 1], 0.0)
        qk = padded(qk_ref[0, :, sl].astype(F32))
        v = padded(v_ref[0, :, sl].astype(F32))
        ncol = _row_to_col(n0_ref[0, hh:hh + 1, :])
        top = jnp.concatenate([c0_ref[0, hh], jnp.where(lane_c == 0, ncol, 0.0)], axis=1)
        cfull = jnp.concatenate([top, jnp.zeros((LANES - DK_A, 2 * LANES), F32)], axis=0)
        m0 = m0_ref[0, :, hh:hh + 1]
        h, cnew, mnew = _mlstm_chunk(qk, v, li, lf, cfull, m0)
        og = og_ref[0, :, sl].astype(F32)
        outs.append(_mlstm_head_out(h[:steps, :], og, mhg_ref[hh]))
        c_ref[0, hh] = cnew[:DK_A, :DV_A]
        n_ref[0, hh:hh + 1, :] = _col_to_row(cnew[:DK_A, DV_A:DV_A + 1])
        m_out = jnp.where(lane1 == hh, mnew, m_out)
    hg_ref[0] = jnp.concatenate(outs, axis=1).astype(BF16)
    m_ref[0] = m_out[:, :H_A]


def _mlstm_sample(qk, v, og, gt, mhg, c0, n0, m0, batch, steps):
    wide = H_A * LANES
    blk3 = lambda b: (b, 0, 0)
    hg, c, n, m = pl.pallas_call(
        functools.partial(_mlstm_sample_body, steps=steps),
        grid=(batch,),
        in_specs=[pl.BlockSpec((1, steps, wide), blk3), pl.BlockSpec((1, steps, wide), blk3),
                  pl.BlockSpec((1, steps, wide), blk3), pl.BlockSpec((1, steps, LANES), blk3),
                  _const_spec((H_A, 1, DV_A)),
                  pl.BlockSpec((1, H_A, DK_A, DV_A), lambda b: (b, 0, 0, 0)),
                  pl.BlockSpec((1, H_A, DK_A), blk3), pl.BlockSpec((1, 1, H_A), blk3)],
        out_specs=[pl.BlockSpec((1, steps, wide), blk3),
                   pl.BlockSpec((1, H_A, DK_A, DV_A), lambda b: (b, 0, 0, 0)),
                   pl.BlockSpec((1, H_A, DK_A), blk3), pl.BlockSpec((1, 1, H_A), blk3)],
        out_shape=[jax.ShapeDtypeStruct((batch, steps, wide), BF16),
                   jax.ShapeDtypeStruct((batch, H_A, DK_A, DV_A), F32),
                   jax.ShapeDtypeStruct((batch, H_A, DK_A), F32),
                   jax.ShapeDtypeStruct((batch, 1, H_A), F32)],
        compiler_params=_cparams("parallel"),
        name="mlstm_sample",
    )(qk.reshape(batch, steps, wide), v.reshape(batch, steps, wide), og.reshape(batch, steps, wide),
      gt.reshape(batch, steps, LANES), mhg.reshape(H_A, 1, DV_A), c0, n0, m0.reshape(batch, 1, H_A))
    return hg.reshape(batch * steps, wide), c, n, m.reshape(batch, H_A)


def _rows_to_lanes(x16, staging_ref):
    staging_ref[...] = jnp.zeros(staging_ref.shape, F32)
    staging_ref[:, 0:x16.shape[1]] = x16
    return staging_ref[...].T[0:x16.shape[1], :]


def _placement():
    h = jnp.arange(H_B)[None, :, None]
    part = jnp.arange(N_SPLIT)[:, None, None]
    col = jnp.arange(H_B * DH_B)[None, None, :]
    lane = jnp.where(h % 2 == 0, DH_B, 0) + part
    return (col == (h // 2) * LANES + lane).astype(BF16)


def _key_aug_body(lf_ref, kb_ref, place_ref, k0_ref, k1_ref, carry):
    @pl.when(pl.program_id(1) == 0)
    def _():
        carry[...] = jnp.zeros(carry.shape, F32)

    tc, wide = kb_ref.shape[1], kb_ref.shape[2]
    tril = jnp.where(_iota((LANES, LANES), 0) >= _iota((LANES, LANES), 1), 1.0, 0.0).astype(BF16)
    lo_half = (_iota((LANES, wide), 1) & (LANES - 1)) < DH_B
    for sub in range(tc // LANES):
        rs = slice(sub * LANES, (sub + 1) * LANES)
        hi, mid, lo = _split3(lf_ref[0, rs, :])
        c = _dot(tril, hi) + _dot(tril, mid) + _dot(tril, lo) + carry[0:1, 0:H_B]
        carry[0:1, 0:H_B] = c[LANES - 1:LANES, :]
        parts = _split3(c)
        bias = sum(_dot(parts[j], place_ref[j]) for j in range(N_SPLIT))
        k = kb_ref[0, rs, :].astype(F32)
        k0_ref[0, rs, :] = jnp.where(lo_half, k, bias).astype(BF16)
        k1_ref[0, rs, :] = jnp.where(lo_half, bias, k).astype(BF16)


def _key_aug(lf, kb):
    batch, tpad, wide = kb.shape
    tc = KVQ_T_TILE
    blk = lambda b, i: (b, i, 0)
    place = _placement()
    return pl.pallas_call(
        _key_aug_body,
        grid=(batch, tpad // tc),
        in_specs=[pl.BlockSpec((1, tc, H_B), blk), pl.BlockSpec((1, tc, wide), blk),
                  _const_spec(place.shape)],
        out_specs=[pl.BlockSpec((1, tc, wide), blk), pl.BlockSpec((1, tc, wide), blk)],
        out_shape=[jax.ShapeDtypeStruct((batch, tpad, wide), BF16)] * 2,
        scratch_shapes=[pltpu.VMEM((SUBLANES, LANES), F32)],
        compiler_params=_cparams("parallel", "arbitrary"),
        name="key_aug",
    )(lf, kb, place)


def _attn_prompt_body(qt_ref, k0_ref, k1_ref, vt_ref, og_ref, o_ref, s_scr, p_scr, acc_scr, qa_scr,
                      *, seq):
    tq, tk = ATTN_TQ, ATTN_TK
    tpad = qt_ref.shape[2]
    nfull = seq // tq
    kaug = (k0_ref, k1_ref)

    def fold(state, qa, q0, k0, nk, width, masked):
        new = []
        scores = [_dot(kaug[hh][0, pl.ds(k0, nk), :], qa[hh]) for hh in range(2)]
        for hh in range(2):
            m, l, acc = state[hh]
            s = scores[hh]
            if masked:
                kpos = k0 + _iota((nk, width), 0)
                qpos = q0 + _iota((nk, width), 1)
                s = jnp.where(kpos <= qpos, s, -jnp.inf)
            m_new = jnp.maximum(m, jnp.max(s, axis=0, keepdims=True))
            alpha = jnp.exp(m - m_new)
            p = jnp.exp(s - m_new)
            l = alpha * l + jnp.sum(p, axis=0, keepdims=True)
            vt = vt_ref[0, hh * DH_B:(hh + 1) * DH_B, pl.ds(k0, nk)]
            acc = alpha * acc + _dot(vt, p.astype(BF16))
            new.append((m_new, l, acc))
        return tuple(new)

    def q_tile(q0, width, n_unmasked, rows_out):
        qt = qt_ref[0, :, pl.ds(q0, width)].astype(F32)
        row = _iota(qt.shape, 0)
        qa = [jnp.where(row < DH_B, qt, jnp.where(row < DH_B + N_SPLIT, -1.0, 0.0)).astype(BF16),
              jnp.where(row >= DH_B, qt, jnp.where(row < N_SPLIT, -1.0, 0.0)).astype(BF16)]
        init = tuple((jnp.full((1, width), NEG_INIT, F32), jnp.zeros((1, width), F32),
                      jnp.zeros((DH_B, width), F32)) for _ in range(2))
        state = lax.fori_loop(
            0, n_unmasked,
            lambda j, st: fold(st, qa, q0, pl.multiple_of(j * tk, tk), tk, width, False), init)
        nk = min(tk, width)
        for d in range(width // nk):
            state = fold(state, qa, q0, q0 + d * nk, nk, width, True)
        out_t = jnp.concatenate([state[0][2] / state[0][1], state[1][2] / state[1][1]], axis=0)
        out = out_t.T[:rows_out, :]
        gate = _sigmoid(og_ref[0, pl.ds(q0, rows_out), :].astype(F32))
        o_ref[0, pl.ds(q0, rows_out), :] = (out * gate).astype(BF16)

    def softmax_step(stats, q0, k0, slot, masked):
        new, alphas = [], []
        for hh in range(2):
            m, l = stats[hh]
            s = s_scr[slot, hh]
            if masked:
                kpos = k0 + _iota((tk, tq), 0)
                qpos = q0 + _iota((tk, tq), 1)
                s = jnp.where(kpos <= qpos, s, -jnp.inf)
            m_new = jnp.maximum(m, jnp.max(s, axis=0, keepdims=True))
            alpha = jnp.exp(m - m_new)
            p = jnp.exp(s - m_new)
            p_scr[slot, hh] = p.astype(BF16)
            new.append((m_new, alpha * l + jnp.sum(p, axis=0, keepdims=True)))
            alphas.append(alpha)
        return tuple(new), alphas

    def pipe_step(stats, q0, kidx, slot, masked, issue_next):
        k0 = pl.multiple_of(kidx * tk, tk)
        if issue_next:
            for hh in range(2):
                s_scr[1 - slot, hh] = _dot(kaug[hh][0, pl.ds(k0 + tk, tk), :], qa_scr[hh])
        kprev = pl.multiple_of(jnp.maximum(kidx - 1, 0) * tk, tk)
        pvs = [_dot(vt_ref[0, hh * DH_B:(hh + 1) * DH_B, pl.ds(kprev, tk)], p_scr[1 - slot, hh])
               for hh in range(2)]
        stats, alphas = softmax_step(stats, q0, k0, slot, masked)
        for hh in range(2):
            acc_scr[hh] = (acc_scr[hh] + pvs[hh]) * alphas[hh]
        return stats

    def q_tile_pipelined(i):
        per = tq // tk
        q0 = pl.multiple_of(i * tq, tq)
        qt = qt_ref[0, :, pl.ds(q0, tq)].astype(F32)
        row = _iota(qt.shape, 0)
        qa_scr[0] = jnp.where(row < DH_B, qt, jnp.where(row < DH_B + N_SPLIT, -1.0, 0.0)).astype(BF16)
        qa_scr[1] = jnp.where(row >= DH_B, qt, jnp.where(row < N_SPLIT, -1.0, 0.0)).astype(BF16)
        p_scr[1] = jnp.zeros(p_scr.shape[1:], BF16)
        acc_scr[...] = jnp.zeros(acc_scr.shape, F32)
        for hh in range(2):
            s_scr[0, hh] = _dot(kaug[hh][0, pl.ds(0, tk), :], qa_scr[hh])
        stats = tuple((jnp.full((1, tq), NEG_INIT, F32), jnp.zeros((1, tq), F32)) for _ in range(2))

        def pair(jj, st):
            for d in range(per):
                st = pipe_step(st, q0, jj * per + d, d % 2, False, True)
            return st

        stats = lax.fori_loop(0, i, pair, stats)
        for d in range(per):
            stats = pipe_step(stats, q0, i * per + d, d % 2, True, d < per - 1)
        last = (per - 1) % 2
        klast = pl.multiple_of((i * per + per - 1) * tk, tk)
        outs = []
        for hh in range(2):
            pv = _dot(vt_ref[0, hh * DH_B:(hh + 1) * DH_B, pl.ds(klast, tk)], p_scr[last, hh])
            outs.append((acc_scr[hh] + pv) / stats[hh][1])
        out = jnp.concatenate(outs, axis=0).T
        gate = _sigmoid(og_ref[0, pl.ds(q0, tq), :].astype(F32))
        o_ref[0, pl.ds(q0, tq), :] = (out * gate).astype(BF16)

    def qbody(i, _):
        q_tile_pipelined(i)
        return 0

    lax.fori_loop(0, nfull, qbody, 0)
    if seq > nfull * tq:
        q0 = nfull * tq
        q_tile(q0, tpad - q0, q0 // tk, seq - q0)


def _attn_prompt(qt, k0, k1, vt, og, seq):
    batch, wide, tpad = qt.shape
    tail_w = tpad - (seq // ATTN_TQ) * ATTN_TQ
    assert ATTN_TQ % (2 * ATTN_TK) == 0 and tail_w <= ATTN_TQ and tail_w % min(ATTN_TK, tail_w) == 0
    rows = lambda b, p: (b, 0, p)
    cols = lambda b, p: (b, p, 0)
    out = pl.pallas_call(
        functools.partial(_attn_prompt_body, seq=seq),
        grid=(batch, H_B // 2),
        in_specs=[pl.BlockSpec((1, LANES, tpad), cols), pl.BlockSpec((1, tpad, LANES), rows),
                  pl.BlockSpec((1, tpad, LANES), rows), pl.BlockSpec((1, LANES, tpad), cols),
                  pl.BlockSpec((1, seq, LANES), rows)],
        out_specs=pl.BlockSpec((1, seq, LANES), rows),
        out_shape=jax.ShapeDtypeStruct((batch, seq, wide), BF16),
        scratch_shapes=[pltpu.VMEM((2, 2, ATTN_TK, ATTN_TQ), F32),
                        pltpu.VMEM((2, 2, ATTN_TK, ATTN_TQ), BF16),
                        pltpu.VMEM((2, DH_B, ATTN_TQ), F32),
                        pltpu.VMEM((2, LANES, ATTN_TQ), BF16)],
        compiler_params=_cparams("parallel", "parallel"),
        name="attn_prompt",
    )(qt, k0, k1, vt, og)
    return out.reshape(batch * seq, wide)


def _attn_sample_body(pt_ref, q_ref, kn_ref, vn_ref, lfn_ref, og_ref, *rest, steps, npages):
    k_refs, v_refs, lf_refs = rest[:npages], rest[npages:2 * npages], rest[2 * npages:3 * npages]
    o_ref, stage = rest[3 * npages], rest[3 * npages + 1]
    rows = H_B * steps
    wide = H_B * DH_B
    u = _iota((LANES, LANES), 0)
    s_ = _iota((LANES, LANES), 1)

    q = q_ref[0].astype(F32)
    qrep = jnp.concatenate([jnp.broadcast_to(q[t:t + 1, :], (H_B, wide)) for t in range(steps)], axis=0)
    diag = _iota((rows, wide), 0) % H_B == _iota((rows, wide), 1) // DH_B
    qbd = jnp.where(diag, qrep, 0.0).astype(BF16)

    lf_all = jnp.concatenate([r[0] for r in lf_refs], axis=0)
    later_and_ones = jnp.concatenate([jnp.where(u > s_, 1.0, 0.0), jnp.ones((LANES, LANES), F32)],
                                     axis=1).astype(BF16)
    wt = _dot_by_01(lf_all, later_and_ones)
    pr = _iota((npages * H_B, npages * H_B), 0)
    pc = _iota((npages * H_B, npages * H_B), 1)
    later_pages = jnp.where((pc % H_B == pr % H_B) & (pc // H_B > pr // H_B), 1.0, 0.0).astype(BF16)
    hi, mid, lo = _split3(wt[:, LANES:])
    rsum = wt[:, :LANES] + _dot(later_pages, hi) + _dot(later_pages, mid) + _dot(later_pages, lo)
    bias_past = jnp.concatenate(
        [jnp.concatenate([rsum[r * H_B:(r + 1) * H_B, :]] * steps, axis=0) for r in range(npages)], axis=1)

    kcat = jnp.concatenate([r[0].astype(BF16) for r in k_refs], axis=1)
    s_past = _dot(qbd, kcat) + bias_past

    zpad = jnp.zeros((LANES - steps, wide), F32)
    kn = jnp.concatenate([kn_ref[0].astype(F32), zpad], axis=0).astype(BF16)
    vn = jnp.concatenate([vn_ref[0].astype(F32), zpad], axis=0).astype(BF16)
    lfn = jnp.concatenate([lfn_ref[0], jnp.zeros((LANES - steps, H_B), F32)], axis=0)
    incl = jnp.where(u <= s_, 1.0, 0.0).astype(BF16)
    cnew = _dot_by_01(_rows_to_lanes(lfn, stage), incl)
    key = _iota((rows, LANES), 1)
    qry = _iota((rows, LANES), 0) // H_B
    bias_new = jnp.where(key <= qry, -jnp.concatenate([cnew] * steps, axis=0), -jnp.inf)
    s_new = _dot_nt(qbd, kn) + bias_new

    m = jnp.maximum(jnp.max(s_past, axis=1, keepdims=True), jnp.max(s_new, axis=1, keepdims=True))
    p_past = jnp.exp(s_past - m)
    p_new = jnp.exp(s_new - m)
    l = jnp.sum(p_past, axis=1, keepdims=True) + jnp.sum(p_new, axis=1, keepdims=True)
    vcat = jnp.concatenate([r[0].astype(BF16) for r in v_refs], axis=1)
    acc = _dot_nt(p_past.astype(BF16), vcat) + _dot(p_new.astype(BF16), vn)
    full = jnp.where(diag, acc / l, 0.0)
    out = jnp.concatenate([jnp.sum(full[t * H_B:(t + 1) * H_B, :], axis=0, keepdims=True)
                           for t in range(steps)], axis=0)
    o_ref[0] = (out * _sigmoid(og_ref[0].astype(F32))).astype(BF16)


def _attn_sample(q, kn, vn, lfn, og, cache_k, cache_v, cache_logf, page_table, batch, steps):
    wide = H_B * DH_B
    n_phys, page = cache_k.shape[0], cache_k.shape[1]
    npages = page_table.shape[1]
    assert page == LANES
    ck = jnp.transpose(cache_k, (0, 2, 3, 1)).reshape(n_phys, wide, page)
    cv = jnp.transpose(cache_v, (0, 2, 3, 1)).reshape(n_phys, wide, page)
    clf = jnp.transpose(cache_logf, (0, 2, 1))
    tok = lambda b, pt: (b, 0, 0)
    page_of = lambda r: (lambda b, pt: (pt[b, r], 0, 0))
    out = pl.pallas_call(
        functools.partial(_attn_sample_body, steps=steps, npages=npages),
        grid_spec=pltpu.PrefetchScalarGridSpec(
            num_scalar_prefetch=1,
            grid=(batch,),
            in_specs=[pl.BlockSpec((1, steps, wide), tok), pl.BlockSpec((1, steps, wide), tok),
                      pl.BlockSpec((1, steps, wide), tok), pl.BlockSpec((1, steps, H_B), tok),
                      pl.BlockSpec((1, steps, wide), tok)]
            + [pl.BlockSpec((1, wide, page), page_of(r)) for r in range(npages)]
            + [pl.BlockSpec((1, wide, page), page_of(r)) for r in range(npages)]
            + [pl.BlockSpec((1, H_B, page), page_of(r)) for r in range(npages)],
            out_specs=pl.BlockSpec((1, steps, wide), tok),
            scratch_shapes=[pltpu.VMEM((LANES, LANES), F32)]),
        out_shape=jax.ShapeDtypeStruct((batch, steps, wide), BF16),
        compiler_params=_cparams("parallel"),
        name="attn_sample",
    )(page_table, q.reshape(batch, steps, wide), kn.reshape(batch, steps, wide),
      vn.reshape(batch, steps, wide), lfn.reshape(batch, steps, H_B), og.reshape(batch, steps, wide),
      *([ck] * npages), *([cv] * npages), *([clf] * npages))
    return out.reshape(batch * steps, wide)


def _prep_params(norm_a, w_in_a, b_ig_a, b_fg_a, mh_norm_a, w_out_a, norm_kv, w_kvf, b_fg_b,
                 k_norm_b, norm_b, w_qo_b, q_norm_b, w_out_b, norm_ffn, w_gate_up, w_down,
                 norm_final):
    d = w_in_a.shape[1]
    hk, hv, hd = H_A * DK_A, H_A * DV_A, H_B * DH_B
    w_in = w_in_a[0]
    wq = w_in[:, :hk].reshape(d, H_A, DK_A)
    wk = w_in[:, hk:2 * hk].reshape(d, H_A, DK_A)
    row = lambda a: a.reshape(1, -1).astype(F32)
    pad_cols = lambda a: jnp.pad(a, ((0, 0), (0, LANES - a.shape[1])))
    lane = jnp.arange(H_A * LANES) % LANES
    return dict(
        norm_a=row(norm_a[0]),
        wqk=jnp.concatenate([wq, wk], axis=2).reshape(d, H_A * LANES).astype(BF16),
        wv=w_in[:, 2 * hk:2 * hk + hv].astype(BF16),
        wog=w_in[:, 2 * hk + hv:2 * hk + 2 * hv].astype(BF16),
        wg=pad_cols(w_in[:, 2 * hk + 2 * hv:]).astype(BF16),
        bg=pad_cols(jnp.concatenate([b_ig_a[0], b_fg_a[0]]).reshape(1, -1).astype(F32)),
        qs=jnp.where(lane < DK_A, DK_A ** -0.5, 1.0).reshape(1, -1).astype(F32),
        mhg=mh_norm_a[0].astype(F32),
        wo_a=w_out_a[0].astype(BF16),
        gkv=row(norm_kv),
        wk=w_kvf[:, :hd].astype(BF16),
        wvs=w_kvf[:, hd:2 * hd].astype(BF16),
        wf=pad_cols(w_kvf[:, 2 * hd:]).astype(BF16),
        bf=pad_cols(b_fg_b.reshape(1, -1).astype(F32)),
        kg=row(jnp.tile(k_norm_b, H_B)),
        gb=row(norm_b[0]),
        wq=w_qo_b[0][:, :hd].astype(BF16),
        wog_b=w_qo_b[0][:, hd:].astype(BF16),
        qg=row(jnp.tile(q_norm_b[0], H_B)) * DH_B ** -0.5,
        wo_b=w_out_b[0].astype(BF16),
        gf=[row(norm_ffn[l]) for l in range(2)],
        wgu=[w_gate_up[l].astype(BF16) for l in range(2)],
        wd=[w_down[l].astype(BF16) for l in range(2)],
        gfin=row(norm_final),
    )


def _layer0(h, p, tm, mlstm):
    qk, v, og, gt = _proj_in(h, p["norm_a"], p["wqk"], p["wv"], p["wog"], p["wg"], p["bg"], p["qs"], tm)
    hg, c, n, m = mlstm(qk, v, og, gt)
    h2 = _mix_ffn(hg, h, p["wo_a"], p["gf"][0], p["wgu"][0], p["wd"][0], p["gfin"], tm, False)
    return h2, c, n, m


def _shared_and_q(h2, p, tm):
    return _kvq_proj(h2, p["gkv"], p["wk"], p["wvs"], p["wf"], p["bf"], p["kg"], p["gb"],
                     p["wq"], p["wog_b"], p["qg"], tm)


def _layer1_tail(o, h2, p, tm):
    return _mix_ffn(o, h2, p["wo_b"], p["gf"][1], p["wgu"][1], p["wd"][1], p["gfin"], tm, True)


def kernel(x_prompt, x_sample, state_C, state_n, state_m, cache_k, cache_v, cache_logf, page_table,
           meta_tokens, norm_a, w_in_a, b_ig_a, b_fg_a, mh_norm_a, w_out_a, norm_kv, w_kvf, b_fg_b,
           k_norm_b, norm_b, w_qo_b, q_norm_b, w_out_b, norm_ffn, w_gate_up, w_down, norm_final):
    assert w_in_a.shape[0] == 1 and w_qo_b.shape[0] == 1 and norm_ffn.shape[0] == 2
    p = _prep_params(norm_a, w_in_a, b_ig_a, b_fg_a, mh_norm_a, w_out_a, norm_kv, w_kvf, b_fg_b,
                     k_norm_b, norm_b, w_qo_b, q_norm_b, w_out_b, norm_ffn, w_gate_up, w_down,
                     norm_final)
    bp, sp, d = x_prompt.shape
    bs, ss, _ = x_sample.shape
    tp = sp + N_META
    hd = H_B * DH_B
    tm = 512

    meta = jnp.broadcast_to(meta_tokens[None].astype(F32), (bp, N_META, d))
    h0 = jnp.concatenate([meta, x_prompt], axis=1).reshape(bp * tp, d)
    h2, p_c, p_n, p_m = _layer0(h0, p, tm, functools.partial(_mlstm_prompt, mhg=p["mhg"], batch=bp, seq=tp))
    kt, vt, lft, lf, kb, vtb, qt, og = _kvq_proj_t(
        h2, p["gkv"], p["wk"], p["wvs"], p["wf"], p["bf"], p["kg"], p["gb"], p["wq"], p["wog_b"],
        p["qg"], bp, tp)
    k0, k1 = _key_aug(lf, kb)
    o = _attn_prompt(qt, k0, k1, vtb, og, tp)
    y = _layer1_tail(o, h2, p, tm)
    y_prompt = y.reshape(bp, tp, d)[:, N_META:]
    p_k = jnp.transpose(kt.reshape(bp, H_B, DH_B, tp), (0, 3, 1, 2))
    p_v = jnp.transpose(vt.reshape(bp, H_B, DH_B, tp), (0, 3, 1, 2))
    p_lf = jnp.transpose(lft, (0, 2, 1))

    hs0 = x_sample.reshape(bs * ss, d)
    hs2, s_c, s_n, s_m = _layer0(
        hs0, p, tm, functools.partial(_mlstm_sample, mhg=p["mhg"], c0=state_C[0], n0=state_n[0],
                                      m0=state_m[0], batch=bs, steps=ss))
    ks, vs, lfs, kbs, vbs, qbs, ogs = _shared_and_q(hs2, p, tm)
    os_ = _attn_sample(qbs, kbs, vbs, lfs, ogs, cache_k, cache_v, cache_logf, page_table, bs, ss)
    y_sample = _layer1_tail(os_, hs2, p, tm).reshape(bs, ss, d)

    return (y_prompt, y_sample, p_c[None], p_n[None], p_m[None], p_k, p_v, p_lf,
            s_c[None], s_n[None], s_m[None], ks.reshape(bs, ss, H_B, DH_B),
            vs.reshape(bs, ss, H_B, DH_B), lfs.reshape(bs, ss, H_B))
```

```python
import functools

import jax
import jax.numpy as jnp
from jax import lax
from jax.experimental import pallas as pl
from jax.experimental.pallas import tpu as pltpu

F32 = jnp.float32
BF16 = jnp.bfloat16

N_META = 16
H_A = 8
DK_A = 64
DV_A = 128
GATE_CAP = 15.0
H_B = 16
DH_B = 64
EPS = 1e-6

LANES = 128
SUBLANES = 8
VMEM_LIMIT_BYTES = 56 * 1024 * 1024

MLSTM_CHUNK = 128
MLSTM_HEADS_PER_STEP = 4
SAMPLE_CHUNK_ALIGN = 8
ATTN_TQ = 512
ATTN_TK = 256
KVQ_T_TILE = 384
N_SPLIT = 3
ROWSUM_ROWS = 16
LOG2E = 1.4426950408889634
FF_CHUNK = 256
MASKED_GATE = -1e30
NEG_INIT = -1e30


def _cparams(*sem):
    return pltpu.CompilerParams(dimension_semantics=sem, vmem_limit_bytes=VMEM_LIMIT_BYTES)


def _const_spec(shape):
    nd = len(shape)
    return pl.BlockSpec(shape, lambda *_: (0,) * nd, pipeline_mode=pl.Buffered(1))


def _rms_scale(x):
    return lax.rsqrt(jnp.mean(x * x, axis=-1, keepdims=True) + EPS)


def _log_sigmoid(x):
    return jnp.minimum(x, 0.0) - jnp.log1p(jnp.exp(-jnp.abs(x)))


def _sigmoid(x):
    return 1.0 / (1.0 + jnp.exp(-x))


def _dot(a, b):
    return jnp.dot(a, b, preferred_element_type=F32)


def _dot_nt(a, b):
    return lax.dot_general(a, b, (((1,), (1,)), ((), ())), preferred_element_type=F32)


def _dot_tn(a, b):
    return lax.dot_general(a, b, (((0,), (0,)), ((), ())), preferred_element_type=F32)


def _split3(x):
    hi = x.astype(BF16)
    r1 = x - hi.astype(F32)
    mid = r1.astype(BF16)
    lo = (r1 - mid.astype(F32)).astype(BF16)
    return hi, mid, lo


def _dot_by_01(x, m01):
    hi, mid, lo = _split3(x)
    return _dot(hi, m01) + _dot(mid, m01) + _dot(lo, m01)


def _iota(shape, dim):
    return lax.broadcasted_iota(jnp.int32, shape, dim)


def _proj_in_body(x_ref, g_ref, wqk_ref, wv_ref, wog_ref, wg_ref, bg_ref, qs_ref,
                  qk_ref, v_ref, og_ref, gt_ref):
    x = x_ref[...]
    xn = (x * _rms_scale(x) * g_ref[...]).astype(BF16)
    qk_ref[...] = (_dot(xn, wqk_ref[...]) * qs_ref[...]).astype(BF16)
    v_ref[...] = _dot(xn, wv_ref[...]).astype(BF16)
    og_ref[...] = _dot(xn, wog_ref[...]).astype(BF16)
    z = _dot(xn, wg_ref[...]) + bg_ref[...]
    cap = GATE_CAP * jnp.tanh(z / GATE_CAP)
    lane = _iota(cap.shape, 1)
    gt_ref[...] = jnp.where(lane < H_A, cap, _log_sigmoid(cap))


def _proj_in(x, g, wqk, wv, wog, wg, bg, qs, tm):
    n, d = x.shape
    row = lambda i: (i, 0)
    return pl.pallas_call(
        _proj_in_body,
        grid=(pl.cdiv(n, tm),),
        in_specs=[pl.BlockSpec((tm, d), row), _const_spec(g.shape), _const_spec(wqk.shape),
                  _const_spec(wv.shape), _const_spec(wog.shape), _const_spec(wg.shape),
                  _const_spec(bg.shape), _const_spec(qs.shape)],
        out_specs=[pl.BlockSpec((tm, wqk.shape[1]), row), pl.BlockSpec((tm, wv.shape[1]), row),
                   pl.BlockSpec((tm, wog.shape[1]), row), pl.BlockSpec((tm, LANES), row)],
        out_shape=[jax.ShapeDtypeStruct((n, wqk.shape[1]), BF16),
                   jax.ShapeDtypeStruct((n, wv.shape[1]), BF16),
                   jax.ShapeDtypeStruct((n, wog.shape[1]), BF16),
                   jax.ShapeDtypeStruct((n, LANES), F32)],
        compiler_params=_cparams("parallel"),
        name="proj_in",
    )(x, g, wqk, wv, wog, wg, bg, qs)


def _mix_ffn_body(a_ref, h_ref, wo_ref, gf_ref, wgu_ref, wd_ref, gout_ref, o_ref, *, d_ff, final):
    h1 = h_ref[...] + _dot(a_ref[...], wo_ref[...])
    xn = (h1 * _rms_scale(h1) * gf_ref[...]).astype(BF16)
    acc = h1
    for c in range(d_ff // FF_CHUNK):
        lo = c * FF_CHUNK
        gate = _dot(xn, wgu_ref[:, lo:lo + FF_CHUNK])
        up = _dot(xn, wgu_ref[:, d_ff + lo:d_ff + lo + FF_CHUNK])
        act = (gate * _sigmoid(gate) * up).astype(BF16)
        acc = acc + _dot(act, wd_ref[lo:lo + FF_CHUNK, :])
    if final:
        acc = acc * _rms_scale(acc) * gout_ref[...]
    o_ref[...] = acc


def _mix_ffn(a, h, wo, gf, wgu, wd, gout, tm, final):
    n, d = h.shape
    d_ff = wd.shape[0]
    row = lambda i: (i, 0)
    return pl.pallas_call(
        functools.partial(_mix_ffn_body, d_ff=d_ff, final=final),
        grid=(pl.cdiv(n, tm),),
        in_specs=[pl.BlockSpec((tm, a.shape[1]), row), pl.BlockSpec((tm, d), row),
                  _const_spec(wo.shape), _const_spec(gf.shape), _const_spec(wgu.shape),
                  _const_spec(wd.shape), _const_spec(gout.shape)],
        out_specs=pl.BlockSpec((tm, d), row),
        out_shape=jax.ShapeDtypeStruct((n, d), F32),
        compiler_params=_cparams("parallel"),
        name="mix_ffn_final" if final else "mix_ffn",
    )(a, h, wo, gf, wgu, wd, gout)


def _head_rmsnorm64(x):
    outs = []
    for j in range(x.shape[1] // LANES):
        blk = x[:, j * LANES:(j + 1) * LANES]
        sq = blk * blk
        lane = _iota(blk.shape, 1)
        s_all = jnp.sum(sq, axis=1, keepdims=True)
        s_lo = jnp.sum(jnp.where(lane < DH_B, sq, 0.0), axis=1, keepdims=True)
        ms = jnp.where(lane < DH_B, s_lo, s_all - s_lo) / DH_B
        outs.append(blk * lax.rsqrt(ms + EPS))
    return jnp.concatenate(outs, axis=1)


def _kvq_compute(h, gkv_ref, wk_ref, wv_ref, wf_ref, bf_ref, kg_ref, gb_ref, wq_ref, wog_ref, qg_ref):
    hr = h * _rms_scale(h)
    xs = (hr * gkv_ref[...]).astype(BF16)
    k = _head_rmsnorm64(_dot(xs, wk_ref[...])) * kg_ref[...]
    v = _dot(xs, wv_ref[...])
    lf = _log_sigmoid(_dot(xs, wf_ref[...]) + bf_ref[...])
    xq = (hr * gb_ref[...]).astype(BF16)
    q = _head_rmsnorm64(_dot(xq, wq_ref[...])) * qg_ref[...]
    og = _dot(xq, wog_ref[...])
    return k, v, lf, q, og


def _kvq_body(h_ref, gkv_ref, wk_ref, wv_ref, wf_ref, bf_ref, kg_ref, gb_ref, wq_ref, wog_ref,
              qg_ref, k_ref, v_ref, lf_ref, kb_ref, vb_ref, qb_ref, og_ref):
    k, v, lf, q, og = _kvq_compute(h_ref[...], gkv_ref, wk_ref, wv_ref, wf_ref, bf_ref, kg_ref,
                                   gb_ref, wq_ref, wog_ref, qg_ref)
    k_ref[...] = k
    v_ref[...] = v
    kb_ref[...] = k.astype(BF16)
    vb_ref[...] = v.astype(BF16)
    lf_ref[...] = lf[:, :H_B]
    qb_ref[...] = q.astype(BF16)
    og_ref[...] = og.astype(BF16)


def _kvq_t_body(h_ref, gkv_ref, wk_ref, wv_ref, wf_ref, bf_ref, kg_ref, gb_ref, wq_ref, wog_ref,
                qg_ref, kt_ref, vt_ref, lft_ref, lf_ref, kb_ref, vtb_ref, qt_ref, og_ref, *, seq):
    tm = h_ref.shape[1]
    k, v, lf, q, og = _kvq_compute(h_ref[0], gkv_ref, wk_ref, wv_ref, wf_ref, bf_ref, kg_ref,
                                   gb_ref, wq_ref, wog_ref, qg_ref)
    valid = pl.program_id(1) * tm + _iota((tm, 1), 0) < seq
    k = jnp.where(valid, k, 0.0)
    v = jnp.where(valid, v, 0.0)
    lf = jnp.where(valid, lf, 0.0)
    vt = v.T
    kt_ref[0] = k.T
    vt_ref[0] = vt
    vtb_ref[0] = vt.astype(BF16)
    qt_ref[0] = jnp.where(valid, q, 0.0).T.astype(BF16)
    kb_ref[0] = k.astype(BF16)
    lf_ref[0] = lf[:, :H_B]
    lft_ref[0] = lf.T[:H_B, :]
    og_ref[0] = og.astype(BF16)


def _kvq_proj_t(h, gkv, wk, wv, wf, bf, kg, gb, wq, wog, qg, batch, seq):
    d = h.shape[1]
    hd = wk.shape[1]
    tm = KVQ_T_TILE
    nt = pl.cdiv(seq, tm)
    tpad = nt * tm
    rows = lambda b, i: (b, i, 0)
    cols = lambda b, i: (b, 0, i)
    return pl.pallas_call(
        functools.partial(_kvq_t_body, seq=seq),
        grid=(batch, nt),
        in_specs=[pl.BlockSpec((1, tm, d), rows)] + [_const_spec(w.shape) for w in
                                                      (gkv, wk, wv, wf, bf, kg, gb, wq, wog, qg)],
        out_specs=[pl.BlockSpec((1, hd, tm), cols), pl.BlockSpec((1, hd, tm), cols),
                   pl.BlockSpec((1, H_B, tm), cols), pl.BlockSpec((1, tm, H_B), rows),
                   pl.BlockSpec((1, tm, hd), rows), pl.BlockSpec((1, hd, tm), cols),
                   pl.BlockSpec((1, hd, tm), cols), pl.BlockSpec((1, tm, wog.shape[1]), rows)],
        out_shape=[jax.ShapeDtypeStruct((batch, hd, seq), F32), jax.ShapeDtypeStruct((batch, hd, seq), F32),
                   jax.ShapeDtypeStruct((batch, H_B, seq), F32), jax.ShapeDtypeStruct((batch, tpad, H_B), F32),
                   jax.ShapeDtypeStruct((batch, tpad, hd), BF16), jax.ShapeDtypeStruct((batch, hd, tpad), BF16),
                   jax.ShapeDtypeStruct((batch, hd, tpad), BF16),
                   jax.ShapeDtypeStruct((batch, seq, wog.shape[1]), BF16)],
        compiler_params=_cparams("parallel", "parallel"),
        name="kvq_proj_t",
    )(h.reshape(batch, seq, d), gkv, wk, wv, wf, bf, kg, gb, wq, wog, qg)


def _kvq_proj(h, gkv, wk, wv, wf, bf, kg, gb, wq, wog, qg, tm):
    n, d = h.shape
    hd = wk.shape[1]
    row = lambda i: (i, 0)
    wide = pl.BlockSpec((tm, hd), row)
    return pl.pallas_call(
        _kvq_body,
        grid=(pl.cdiv(n, tm),),
        in_specs=[pl.BlockSpec((tm, d), row)] + [_const_spec(w.shape) for w in
                                                 (gkv, wk, wv, wf, bf, kg, gb, wq, wog, qg)],
        out_specs=[wide, wide, pl.BlockSpec((tm, H_B), row), wide, wide, wide,
                   pl.BlockSpec((tm, wog.shape[1]), row)],
        out_shape=[jax.ShapeDtypeStruct((n, hd), F32), jax.ShapeDtypeStruct((n, hd), F32),
                   jax.ShapeDtypeStruct((n, H_B), F32), jax.ShapeDtypeStruct((n, hd), BF16),
                   jax.ShapeDtypeStruct((n, hd), BF16), jax.ShapeDtypeStruct((n, hd), BF16),
                   jax.ShapeDtypeStruct((n, wog.shape[1]), BF16)],
        compiler_params=_cparams("parallel"),
        name="kvq_proj",
    )(h, gkv, wk, wv, wf, bf, kg, gb, wq, wog, qg)


def _col_to_row(col):
    n = col.shape[0]
    eye = _iota((n, n), 0) == _iota((n, n), 1)
    return jnp.sum(jnp.where(eye, col, 0.0), axis=0, keepdims=True)


def _row_to_col(row):
    n = row.shape[1]
    eye = _iota((n, n), 0) == _iota((n, n), 1)
    return jnp.sum(jnp.where(eye, row, 0.0), axis=1, keepdims=True)


def _mlstm_chunk(qk, v, li, lf, cfull, m):
    L = qk.shape[0]
    r = _iota((L, L), 0)
    c = _iota((L, L), 1)
    tril = r >= c
    lf_row = _col_to_row(lf)
    li_row = _col_to_row(li)
    b = jnp.sum(jnp.where(tril, lf_row, 0.0), axis=1, keepdims=True)
    b_row = _col_to_row(b)
    dmat = jnp.where(tril, b - b_row + li_row, -jnp.inf)
    inter = b + m
    m_t = jnp.maximum(inter, jnp.max(dmat, axis=1, keepdims=True))
    w_inter = jnp.exp(inter - m_t)

    lane = _iota((L, LANES), 1)
    q_lo = jnp.where(lane < DK_A, qk, 0.0).astype(BF16)
    kq = pltpu.roll(qk, DK_A, 1)
    s = _dot_nt(q_lo, kq.astype(BF16)) * jnp.exp(dmat - m_t)
    ones_col = jnp.where(lane == 0, 1.0, 0.0)
    vaug = jnp.concatenate([v, ones_col], axis=1).astype(BF16)
    tot = w_inter * _dot(q_lo, cfull.astype(BF16)) + _dot(s.astype(BF16), vaug)
    num = tot[:, :DV_A]
    den = tot[:, DV_A:DV_A + 1]
    h = num / jnp.maximum(jnp.abs(den), jnp.exp(-m_t))

    b_end = b[L - 1:L, :]
    g = b_end - b + li
    m_new = jnp.maximum(b_end + m, jnp.max(g, axis=0, keepdims=True))
    w_c = jnp.exp(b_end + m - m_new)
    w_g = jnp.exp(g - m_new)
    upd = _dot_tn((w_g * kq).astype(BF16), vaug)
    keep = _iota(cfull.shape, 0) < DK_A
    cfull_new = jnp.where(keep, w_c * cfull + upd, 0.0)
    return h, cfull_new, m_new


def _mlstm_head_out(h, og, gain):
    return h * _rms_scale(h) * gain * _sigmoid(og)


def _mlstm_prompt_body(qk_ref, v_ref, og_ref, gt_ref, mhg_ref, hg_ref, c_ref, n_ref, m_ref,
                       cst, mst, *, seq, hb):
    grp = pl.program_id(1)
    ch = MLSTM_CHUNK
    nfull, tail = seq // ch, seq % ch
    cst[...] = jnp.zeros(cst.shape, F32)
    mst[...] = jnp.zeros(mst.shape, F32)

    def chunk(r0, first_valid):
        gt = gt_ref[0, pl.ds(r0, ch), :]
        lane = _iota(gt.shape, 1)
        rowi = _iota((ch, 1), 0)
        outs = []
        for hh in range(hb):
            head = grp * hb + hh
            li = jnp.sum(jnp.where(lane == head, gt, 0.0), axis=1, keepdims=True)
            lf = jnp.sum(jnp.where(lane == head + H_A, gt, 0.0), axis=1, keepdims=True)
            if first_valid:
                li = jnp.where(rowi >= first_valid, li, MASKED_GATE)
                lf = jnp.where(rowi >= first_valid, lf, 0.0)
            sl = slice(hh * LANES, (hh + 1) * LANES)
            qk = qk_ref[0, pl.ds(r0, ch), sl].astype(F32)
            v = v_ref[0, pl.ds(r0, ch), sl].astype(F32)
            og = og_ref[0, pl.ds(r0, ch), sl].astype(F32)
            h, cnew, mnew = _mlstm_chunk(qk, v, li, lf, cst[hh], mst[hh, 0:1, 0:1])
            cst[hh] = cnew
            mst[hh] = jnp.broadcast_to(mnew, mst.shape[1:])
            outs.append(_mlstm_head_out(h, og, mhg_ref[hh]))
        return jnp.concatenate(outs, axis=1).astype(BF16)

    def loop_body(j, carry):
        r0 = pl.multiple_of(j * ch, ch)
        hg_ref[0, pl.ds(r0, ch), :] = chunk(r0, 0)
        return carry

    lax.fori_loop(0, nfull, loop_body, 0)
    if tail:
        out = chunk(seq - ch, ch - tail)
        hg_ref[0, seq - tail:seq, :] = out[ch - tail:, :]

    for hh in range(hb):
        cfull = cst[hh]
        c_ref[0, hh] = cfull[:DK_A, :DV_A]
        n_ref[0, hh] = _col_to_row(cfull[:DK_A, DV_A:DV_A + 1])
        m_ref[0, hh] = mst[hh, 0:1, 0:1]


def _mlstm_prompt(qk, v, og, gt, mhg, batch, seq):
    hb = MLSTM_HEADS_PER_STEP
    w = hb * LANES
    qk3, v3, og3 = (a.reshape(batch, seq, a.shape[-1]) for a in (qk, v, og))
    gt3 = gt.reshape(batch, seq, LANES)
    seq_blk = lambda b, g: (b, 0, g)
    hg, c, n, m = pl.pallas_call(
        functools.partial(_mlstm_prompt_body, seq=seq, hb=hb),
        grid=(batch, H_A // hb),
        in_specs=[pl.BlockSpec((1, seq, w), seq_blk), pl.BlockSpec((1, seq, w), seq_blk),
                  pl.BlockSpec((1, seq, w), seq_blk),
                  pl.BlockSpec((1, seq, LANES), lambda b, g: (b, 0, 0)),
                  pl.BlockSpec((hb, 1, DV_A), lambda b, g: (g, 0, 0))],
        out_specs=[pl.BlockSpec((1, seq, w), seq_blk),
                   pl.BlockSpec((1, hb, DK_A, DV_A), lambda b, g: (b, g, 0, 0)),
                   pl.BlockSpec((1, hb, 1, DK_A), lambda b, g: (b, g, 0, 0)),
                   pl.BlockSpec((1, hb, 1, 1), lambda b, g: (b, g, 0, 0))],
        out_shape=[jax.ShapeDtypeStruct((batch, seq, H_A * DV_A), BF16),
                   jax.ShapeDtypeStruct((batch, H_A, DK_A, DV_A), F32),
                   jax.ShapeDtypeStruct((batch, H_A, 1, DK_A), F32),
                   jax.ShapeDtypeStruct((batch, H_A, 1, 1), F32)],
        scratch_shapes=[pltpu.VMEM((hb, LANES, 2 * LANES), F32), pltpu.VMEM((hb, SUBLANES, LANES), F32)],
        compiler_params=_cparams("parallel", "arbitrary"),
        name="mlstm_prompt",
    )(qk3, v3, og3, gt3, mhg.reshape(H_A, 1, DV_A))
    return (hg.reshape(batch * seq, H_A * DV_A), c, n.reshape(batch, H_A, DK_A),
            m.reshape(batch, H_A))


def _mlstm_sample_body(qk_ref, v_ref, og_ref, gt_ref, mhg_ref, c0_ref, n0_ref, m0_ref,
                       hg_ref, c_ref, n_ref, m_ref, *, steps):
    ch = pl.cdiv(steps, SAMPLE_CHUNK_ALIGN) * SAMPLE_CHUNK_ALIGN
    pad = jnp.zeros((ch - steps, LANES), F32)
    padded = (lambda a: jnp.concatenate([a, pad], axis=0)) if ch > steps else (lambda a: a)
    rowi = _iota((ch, 1), 0)
    gt = padded(gt_ref[0])
    lane1 = _iota((1, LANES), 1)
    lane_c = _iota((DK_A, LANES), 1)
    m_out = jnp.zeros((1, LANES), F32)
    outs = []
    for hh in range(H_A):
        sl = slice(hh * LANES, (hh + 1) * LANES)
        li = jnp.where(rowi < steps, gt[:, hh:hh + 1], MASKED_GATE)
        lf = jnp.where(rowi < steps, gt[:, H_A + hh:H_A + hh + 1], 0.0)
        qk = padded(qk_ref[0, :, sl].astype(F32))
        v = padded(v_ref[0, :, sl].astype(F32))
        ncol = _row_to_col(n0_ref[0, hh:hh + 1, :])
        top = jnp.concatenate([c0_ref[0, hh], jnp.where(lane_c == 0, ncol, 0.0)], axis=1)
        cfull = jnp.concatenate([top, jnp.zeros((LANES - DK_A, 2 * LANES), F32)], axis=0)
        m0 = m0_ref[0, :, hh:hh + 1]
        h, cnew, mnew = _mlstm_chunk(qk, v, li, lf, cfull, m0)
        og = og_ref[0, :, sl].astype(F32)
        outs.append(_mlstm_head_out(h[:steps, :], og, mhg_ref[hh]))
        c_ref[0, hh] = cnew[:DK_A, :DV_A]
        n_ref[0, hh:hh + 1, :] = _col_to_row(cnew[:DK_A, DV_A:DV_A + 1])
        m_out = jnp.where(lane1 == hh, mnew, m_out)
    hg_ref[0] = jnp.concatenate(outs, axis=1).astype(BF16)
    m_ref[0] = m_out[:, :H_A]


def _mlstm_sample(qk, v, og, gt, mhg, c0, n0, m0, batch, steps):
    wide = H_A * LANES
    blk3 = lambda b: (b, 0, 0)
    hg, c, n, m = pl.pallas_call(
        functools.partial(_mlstm_sample_body, steps=steps),
        grid=(batch,),
        in_specs=[pl.BlockSpec((1, steps, wide), blk3), pl.BlockSpec((1, steps, wide), blk3),
                  pl.BlockSpec((1, steps, wide), blk3), pl.BlockSpec((1, steps, LANES), blk3),
                  _const_spec((H_A, 1, DV_A)),
                  pl.BlockSpec((1, H_A, DK_A, DV_A), lambda b: (b, 0, 0, 0)),
                  pl.BlockSpec((1, H_A, DK_A), blk3), pl.BlockSpec((1, 1, H_A), blk3)],
        out_specs=[pl.BlockSpec((1, steps, wide), blk3),
                   pl.BlockSpec((1, H_A, DK_A, DV_A), lambda b: (b, 0, 0, 0)),
                   pl.BlockSpec((1, H_A, DK_A), blk3), pl.BlockSpec((1, 1, H_A), blk3)],
        out_shape=[jax.ShapeDtypeStruct((batch, steps, wide), BF16),
                   jax.ShapeDtypeStruct((batch, H_A, DK_A, DV_A), F32),
                   jax.ShapeDtypeStruct((batch, H_A, DK_A), F32),
                   jax.ShapeDtypeStruct((batch, 1, H_A), F32)],
        compiler_params=_cparams("parallel"),
        name="mlstm_sample",
    )(qk.reshape(batch, steps, wide), v.reshape(batch, steps, wide), og.reshape(batch, steps, wide),
      gt.reshape(batch, steps, LANES), mhg.reshape(H_A, 1, DV_A), c0, n0, m0.reshape(batch, 1, H_A))
    return hg.reshape(batch * steps, wide), c, n, m.reshape(batch, H_A)


def _rows_to_lanes(x16, staging_ref):
    staging_ref[...] = jnp.zeros(staging_ref.shape, F32)
    staging_ref[:, 0:x16.shape[1]] = x16
    return staging_ref[...].T[0:x16.shape[1], :]


def _placement():
    h = jnp.arange(H_B)[None, :, None]
    part = jnp.arange(N_SPLIT)[:, None, None]
    col = jnp.arange(H_B * DH_B)[None, None, :]
    lane = jnp.where(h % 2 == 0, DH_B, 0) + part
    return (col == (h // 2) * LANES + lane).astype(BF16)


def _key_aug_body(lf_ref, kb_ref, place_ref, k0_ref, k1_ref, carry):
    @pl.when(pl.program_id(1) == 0)
    def _():
        carry[...] = jnp.zeros(carry.shape, F32)

    tc, wide = kb_ref.shape[1], kb_ref.shape[2]
    tril = jnp.where(_iota((LANES, LANES), 0) >= _iota((LANES, LANES), 1), 1.0, 0.0).astype(BF16)
    lo_half = (_iota((LANES, wide), 1) & (LANES - 1)) < DH_B
    for sub in range(tc // LANES):
        rs = slice(sub * LANES, (sub + 1) * LANES)
        hi, mid, lo = _split3(lf_ref[0, rs, :])
        c = _dot(tril, hi) + _dot(tril, mid) + _dot(tril, lo) + carry[0:1, 0:H_B]
        carry[0:1, 0:H_B] = c[LANES - 1:LANES, :]
        parts = _split3(c * LOG2E)
        bias = sum(_dot(parts[j], place_ref[j]) for j in range(N_SPLIT))
        k = kb_ref[0, rs, :].astype(F32)
        k0_ref[0, rs, :] = jnp.where(lo_half, k, bias).astype(BF16)
        k1_ref[0, rs, :] = jnp.where(lo_half, bias, k).astype(BF16)


def _key_aug(lf, kb):
    batch, tpad, wide = kb.shape
    tc = KVQ_T_TILE
    blk = lambda b, i: (b, i, 0)
    place = _placement()
    return pl.pallas_call(
        _key_aug_body,
        grid=(batch, tpad // tc),
        in_specs=[pl.BlockSpec((1, tc, H_B), blk), pl.BlockSpec((1, tc, wide), blk),
                  _const_spec(place.shape)],
        out_specs=[pl.BlockSpec((1, tc, wide), blk), pl.BlockSpec((1, tc, wide), blk)],
        out_shape=[jax.ShapeDtypeStruct((batch, tpad, wide), BF16)] * 2,
        scratch_shapes=[pltpu.VMEM((SUBLANES, LANES), F32)],
        compiler_params=_cparams("parallel", "arbitrary"),
        name="key_aug",
    )(lf, kb, place)


def _attn_prompt_body(qt_ref, k0_ref, k1_ref, vt_ref, og_ref, o_ref, s_scr, p_scr, acc_scr, qa_scr,
                      mask_scr, *, seq):
    tq, tk = ATTN_TQ, ATTN_TK
    tpad = qt_ref.shape[2]
    nfull = seq // tq
    kaug = (k0_ref, k1_ref)

    def augmented_queries(qt):
        row = _iota(qt.shape, 0)
        return [jnp.where(row < DH_B, qt, jnp.where(row < DH_B + N_SPLIT, -1.0, 0.0)).astype(BF16),
                jnp.where(row >= DH_B, qt, jnp.where(row < N_SPLIT, -1.0, 0.0)).astype(BF16)]

    def with_ones(vt):
        return jnp.concatenate([vt, jnp.ones((ROWSUM_ROWS, vt.shape[1]), BF16)], axis=0)

    def tail_tile(q0, width, rows_out):
        qa = augmented_queries(qt_ref[0, :, pl.ds(q0, width)].astype(F32))
        causal = _iota((tpad, width), 0) <= q0 + _iota((tpad, width), 1)
        outs = []
        for hh in range(2):
            s = jnp.where(causal, _dot(kaug[hh][0], qa[hh]), -jnp.inf)
            p = jnp.exp2(s - jnp.max(s, axis=0, keepdims=True)).astype(BF16)
            full = _dot(with_ones(vt_ref[0, hh * DH_B:(hh + 1) * DH_B, :]), p)
            outs.append(full[:DH_B, :] / full[DH_B:DH_B + 1, :])
        out = jnp.concatenate(outs, axis=0).T[:rows_out, :]
        gate = _sigmoid(og_ref[0, pl.ds(q0, rows_out), :].astype(F32))
        o_ref[0, pl.ds(q0, rows_out), :] = (out * gate).astype(BF16)

    per = tq // tk
    for d in range(per):
        key = d * tk + _iota((tk, tq), 0)
        mask_scr[d] = jnp.where(key <= _iota((tk, tq), 1), 0.0, -jnp.inf)

    def value_product(hh, k0, slot):
        return _dot(with_ones(vt_ref[0, hh * DH_B:(hh + 1) * DH_B, pl.ds(k0, tk)]), p_scr[slot, hh])

    def pipe_step(ms, kidx, slot, diag_idx, issue_next):
        k0 = pl.multiple_of(kidx * tk, tk)
        if issue_next:
            for hh in range(2):
                s_scr[1 - slot, hh] = _dot(kaug[hh][0, pl.ds(k0 + tk, tk), :], qa_scr[hh])
        kprev = pl.multiple_of(jnp.maximum(kidx - 1, 0) * tk, tk)
        pvs = [value_product(hh, kprev, 1 - slot) for hh in range(2)]
        new = []
        for hh in range(2):
            s = s_scr[slot, hh]
            if diag_idx is not None:
                s = s + mask_scr[diag_idx]
            m_new = jnp.maximum(ms[hh], jnp.max(s, axis=0, keepdims=True))
            p_scr[slot, hh] = jnp.exp2(s - m_new).astype(BF16)
            acc_scr[hh] = (acc_scr[hh] + pvs[hh]) * jnp.exp2(ms[hh] - m_new)
            new.append(m_new)
        return tuple(new)

    def q_tile_pipelined(i):
        q0 = pl.multiple_of(i * tq, tq)
        qa = augmented_queries(qt_ref[0, :, pl.ds(q0, tq)].astype(F32))
        qa_scr[0] = qa[0]
        qa_scr[1] = qa[1]
        p_scr[1] = jnp.zeros(p_scr.shape[1:], BF16)
        acc_scr[...] = jnp.zeros(acc_scr.shape, F32)
        for hh in range(2):
            s_scr[0, hh] = _dot(kaug[hh][0, pl.ds(0, tk), :], qa_scr[hh])
        ms = tuple(jnp.full((1, tq), NEG_INIT, F32) for _ in range(2))

        def group(jj, st):
            for d in range(per):
                st = pipe_step(st, jj * per + d, d % 2, None, True)
            return st

        ms = lax.fori_loop(0, i, group, ms)
        for d in range(per):
            ms = pipe_step(ms, i * per + d, d % 2, d, d < per - 1)
        last = (per - 1) % 2
        klast = pl.multiple_of((i * per + per - 1) * tk, tk)
        outs = []
        for hh in range(2):
            full = acc_scr[hh] + value_product(hh, klast, last)
            outs.append(full[:DH_B, :] / full[DH_B:DH_B + 1, :])
        out = jnp.concatenate(outs, axis=0).T
        gate = _sigmoid(og_ref[0, pl.ds(q0, tq), :].astype(F32))
        o_ref[0, pl.ds(q0, tq), :] = (out * gate).astype(BF16)

    def qbody(i, _):
        q_tile_pipelined(i)
        return 0

    lax.fori_loop(0, nfull, qbody, 0)
    if seq > nfull * tq:
        q0 = nfull * tq
        tail_tile(q0, tpad - q0, seq - q0)


def _attn_prompt(qt, k0, k1, vt, og, seq):
    batch, wide, tpad = qt.shape
    tail_w = tpad - (seq // ATTN_TQ) * ATTN_TQ
    assert ATTN_TQ % (2 * ATTN_TK) == 0 and 0 <= tail_w and tail_w % LANES == 0
    rows = lambda b, p: (b, 0, p)
    cols = lambda b, p: (b, p, 0)
    out = pl.pallas_call(
        functools.partial(_attn_prompt_body, seq=seq),
        grid=(batch, H_B // 2),
        in_specs=[pl.BlockSpec((1, LANES, tpad), cols), pl.BlockSpec((1, tpad, LANES), rows),
                  pl.BlockSpec((1, tpad, LANES), rows), pl.BlockSpec((1, LANES, tpad), cols),
                  pl.BlockSpec((1, seq, LANES), rows)],
        out_specs=pl.BlockSpec((1, seq, LANES), rows),
        out_shape=jax.ShapeDtypeStruct((batch, seq, wide), BF16),
        scratch_shapes=[pltpu.VMEM((2, 2, ATTN_TK, ATTN_TQ), F32),
                        pltpu.VMEM((2, 2, ATTN_TK, ATTN_TQ), BF16),
                        pltpu.VMEM((2, DH_B + ROWSUM_ROWS, ATTN_TQ), F32),
                        pltpu.VMEM((2, LANES, ATTN_TQ), BF16),
                        pltpu.VMEM((ATTN_TQ // ATTN_TK, ATTN_TK, ATTN_TQ), F32)],
        compiler_params=_cparams("parallel", "parallel"),
        name="attn_prompt",
    )(qt, k0, k1, vt, og)
    return out.reshape(batch * seq, wide)


def _attn_sample_body(pt_ref, q_ref, kn_ref, vn_ref, lfn_ref, og_ref, *rest, steps, npages):
    k_refs, v_refs, lf_refs = rest[:npages], rest[npages:2 * npages], rest[2 * npages:3 * npages]
    o_ref, stage = rest[3 * npages], rest[3 * npages + 1]
    rows = H_B * steps
    wide = H_B * DH_B
    u = _iota((LANES, LANES), 0)
    s_ = _iota((LANES, LANES), 1)

    q = q_ref[0].astype(F32)
    qrep = jnp.concatenate([jnp.broadcast_to(q[t:t + 1, :], (H_B, wide)) for t in range(steps)], axis=0)
    diag = _iota((rows, wide), 0) % H_B == _iota((rows, wide), 1) // DH_B
    qbd = jnp.where(diag, qrep, 0.0).astype(BF16)

    lf_all = jnp.concatenate([r[0] for r in lf_refs], axis=0)
    later_and_ones = jnp.concatenate([jnp.where(u > s_, 1.0, 0.0), jnp.ones((LANES, LANES), F32)],
                                     axis=1).astype(BF16)
    wt = _dot_by_01(lf_all, later_and_ones)
    pr = _iota((npages * H_B, npages * H_B), 0)
    pc = _iota((npages * H_B, npages * H_B), 1)
    later_pages = jnp.where((pc % H_B == pr % H_B) & (pc // H_B > pr // H_B), 1.0, 0.0).astype(BF16)
    hi, mid, lo = _split3(wt[:, LANES:])
    rsum = wt[:, :LANES] + _dot(later_pages, hi) + _dot(later_pages, mid) + _dot(later_pages, lo)
    bias_past = jnp.concatenate(
        [jnp.concatenate([rsum[r * H_B:(r + 1) * H_B, :]] * steps, axis=0) for r in range(npages)], axis=1)

    kcat = jnp.concatenate([r[0].astype(BF16) for r in k_refs], axis=1)
    s_past = _dot(qbd, kcat) + bias_past

    zpad = jnp.zeros((LANES - steps, wide), F32)
    kn = jnp.concatenate([kn_ref[0].astype(F32), zpad], axis=0).astype(BF16)
    vn = jnp.concatenate([vn_ref[0].astype(F32), zpad], axis=0).astype(BF16)
    lfn = jnp.concatenate([lfn_ref[0], jnp.zeros((LANES - steps, H_B), F32)], axis=0)
    incl = jnp.where(u <= s_, 1.0, 0.0).astype(BF16)
    cnew = _dot_by_01(_rows_to_lanes(lfn, stage), incl)
    key = _iota((rows, LANES), 1)
    qry = _iota((rows, LANES), 0) // H_B
    bias_new = jnp.where(key <= qry, -jnp.concatenate([cnew] * steps, axis=0), -jnp.inf)
    s_new = _dot_nt(qbd, kn) + bias_new

    m = jnp.maximum(jnp.max(s_past, axis=1, keepdims=True), jnp.max(s_new, axis=1, keepdims=True))
    p_past = jnp.exp(s_past - m)
    p_new = jnp.exp(s_new - m)
    l = jnp.sum(p_past, axis=1, keepdims=True) + jnp.sum(p_new, axis=1, keepdims=True)
    vcat = jnp.concatenate([r[0].astype(BF16) for r in v_refs], axis=1)
    acc = _dot_nt(p_past.astype(BF16), vcat) + _dot(p_new.astype(BF16), vn)
    full = jnp.where(diag, acc / l, 0.0)
    out = jnp.concatenate([jnp.sum(full[t * H_B:(t + 1) * H_B, :], axis=0, keepdims=True)
                           for t in range(steps)], axis=0)
    o_ref[0] = (out * _sigmoid(og_ref[0].astype(F32))).astype(BF16)


def _attn_sample(q, kn, vn, lfn, og, cache_k, cache_v, cache_logf, page_table, batch, steps):
    wide = H_B * DH_B
    n_phys, page = cache_k.shape[0], cache_k.shape[1]
    npages = page_table.shape[1]
    assert page == LANES
    ck = jnp.transpose(cache_k, (0, 2, 3, 1)).reshape(n_phys, wide, page)
    cv = jnp.transpose(cache_v, (0, 2, 3, 1)).reshape(n_phys, wide, page)
    clf = jnp.transpose(cache_logf, (0, 2, 1))
    tok = lambda b, pt: (b, 0, 0)
    page_of = lambda r: (lambda b, pt: (pt[b, r], 0, 0))
    out = pl.pallas_call(
        functools.partial(_attn_sample_body, steps=steps, npages=npages),
        grid_spec=pltpu.PrefetchScalarGridSpec(
            num_scalar_prefetch=1,
            grid=(batch,),
            in_specs=[pl.BlockSpec((1, steps, wide), tok), pl.BlockSpec((1, steps, wide), tok),
                      pl.BlockSpec((1, steps, wide), tok), pl.BlockSpec((1, steps, H_B), tok),
                      pl.BlockSpec((1, steps, wide), tok)]
            + [pl.BlockSpec((1, wide, page), page_of(r)) for r in range(npages)]
            + [pl.BlockSpec((1, wide, page), page_of(r)) for r in range(npages)]
            + [pl.BlockSpec((1, H_B, page), page_of(r)) for r in range(npages)],
            out_specs=pl.BlockSpec((1, steps, wide), tok),
            scratch_shapes=[pltpu.VMEM((LANES, LANES), F32)]),
        out_shape=jax.ShapeDtypeStruct((batch, steps, wide), BF16),
        compiler_params=_cparams("parallel"),
        name="attn_sample",
    )(page_table, q.reshape(batch, steps, wide), kn.reshape(batch, steps, wide),
      vn.reshape(batch, steps, wide), lfn.reshape(batch, steps, H_B), og.reshape(batch, steps, wide),
      *([ck] * npages), *([cv] * npages), *([clf] * npages))
    return out.reshape(batch * steps, wide)


def _prep_params(norm_a, w_in_a, b_ig_a, b_fg_a, mh_norm_a, w_out_a, norm_kv, w_kvf, b_fg_b,
                 k_norm_b, norm_b, w_qo_b, q_norm_b, w_out_b, norm_ffn, w_gate_up, w_down,
                 norm_final):
    d = w_in_a.shape[1]
    hk, hv, hd = H_A * DK_A, H_A * DV_A, H_B * DH_B
    w_in = w_in_a[0]
    wq = w_in[:, :hk].reshape(d, H_A, DK_A)
    wk = w_in[:, hk:2 * hk].reshape(d, H_A, DK_A)
    row = lambda a: a.reshape(1, -1).astype(F32)
    pad_cols = lambda a: jnp.pad(a, ((0, 0), (0, LANES - a.shape[1])))
    lane = jnp.arange(H_A * LANES) % LANES
    return dict(
        norm_a=row(norm_a[0]),
        wqk=jnp.concatenate([wq, wk], axis=2).reshape(d, H_A * LANES).astype(BF16),
        wv=w_in[:, 2 * hk:2 * hk + hv].astype(BF16),
        wog=w_in[:, 2 * hk + hv:2 * hk + 2 * hv].astype(BF16),
        wg=pad_cols(w_in[:, 2 * hk + 2 * hv:]).astype(BF16),
        bg=pad_cols(jnp.concatenate([b_ig_a[0], b_fg_a[0]]).reshape(1, -1).astype(F32)),
        qs=jnp.where(lane < DK_A, DK_A ** -0.5, 1.0).reshape(1, -1).astype(F32),
        mhg=mh_norm_a[0].astype(F32),
        wo_a=w_out_a[0].astype(BF16),
        gkv=row(norm_kv),
        wk=w_kvf[:, :hd].astype(BF16),
        wvs=w_kvf[:, hd:2 * hd].astype(BF16),
        wf=pad_cols(w_kvf[:, 2 * hd:]).astype(BF16),
        bf=pad_cols(b_fg_b.reshape(1, -1).astype(F32)),
        kg=row(jnp.tile(k_norm_b, H_B)),
        gb=row(norm_b[0]),
        wq=w_qo_b[0][:, :hd].astype(BF16),
        wog_b=w_qo_b[0][:, hd:].astype(BF16),
        qg=row(jnp.tile(q_norm_b[0], H_B)) * DH_B ** -0.5,
        wo_b=w_out_b[0].astype(BF16),
        gf=[row(norm_ffn[l]) for l in range(2)],
        wgu=[w_gate_up[l].astype(BF16) for l in range(2)],
        wd=[w_down[l].astype(BF16) for l in range(2)],
        gfin=row(norm_final),
    )


def _layer0(h, p, tm, mlstm):
    qk, v, og, gt = _proj_in(h, p["norm_a"], p["wqk"], p["wv"], p["wog"], p["wg"], p["bg"], p["qs"], tm)
    hg, c, n, m = mlstm(qk, v, og, gt)
    h2 = _mix_ffn(hg, h, p["wo_a"], p["gf"][0], p["wgu"][0], p["wd"][0], p["gfin"], tm, False)
    return h2, c, n, m


def _shared_and_q(h2, p, tm):
    return _kvq_proj(h2, p["gkv"], p["wk"], p["wvs"], p["wf"], p["bf"], p["kg"], p["gb"],
                     p["wq"], p["wog_b"], p["qg"], tm)


def _layer1_tail(o, h2, p, tm):
    return _mix_ffn(o, h2, p["wo_b"], p["gf"][1], p["wgu"][1], p["wd"][1], p["gfin"], tm, True)


def kernel(x_prompt, x_sample, state_C, state_n, state_m, cache_k, cache_v, cache_logf, page_table,
           meta_tokens, norm_a, w_in_a, b_ig_a, b_fg_a, mh_norm_a, w_out_a, norm_kv, w_kvf, b_fg_b,
           k_norm_b, norm_b, w_qo_b, q_norm_b, w_out_b, norm_ffn, w_gate_up, w_down, norm_final):
    assert w_in_a.shape[0] == 1 and w_qo_b.shape[0] == 1 and norm_ffn.shape[0] == 2
    p = _prep_params(norm_a, w_in_a, b_ig_a, b_fg_a, mh_norm_a, w_out_a, norm_kv, w_kvf, b_fg_b,
                     k_norm_b, norm_b, w_qo_b, q_norm_b, w_out_b, norm_ffn, w_gate_up, w_down,
                     norm_final)
    bp, sp, d = x_prompt.shape
    bs, ss, _ = x_sample.shape
    tp = sp + N_META
    hd = H_B * DH_B
    tm = 512

    meta = jnp.broadcast_to(meta_tokens[None].astype(F32), (bp, N_META, d))
    h0 = jnp.concatenate([meta, x_prompt], axis=1).reshape(bp * tp, d)
    h2, p_c, p_n, p_m = _layer0(h0, p, tm, functools.partial(_mlstm_prompt, mhg=p["mhg"], batch=bp, seq=tp))
    kt, vt, lft, lf, kb, vtb, qt, og = _kvq_proj_t(
        h2, p["gkv"], p["wk"], p["wvs"], p["wf"], p["bf"], p["kg"], p["gb"], p["wq"], p["wog_b"],
        p["qg"] * LOG2E, bp, tp)
    k0, k1 = _key_aug(lf, kb)
    o = _attn_prompt(qt, k0, k1, vtb, og, tp)
    y = _layer1_tail(o, h2, p, tm)
    y_prompt = y.reshape(bp, tp, d)[:, N_META:]
    p_k = jnp.transpose(kt.reshape(bp, H_B, DH_B, tp), (0, 3, 1, 2))
    p_v = jnp.transpose(vt.reshape(bp, H_B, DH_B, tp), (0, 3, 1, 2))
    p_lf = jnp.transpose(lft, (0, 2, 1))

    hs0 = x_sample.reshape(bs * ss, d)
    hs2, s_c, s_n, s_m = _layer0(
        hs0, p, tm, functools.partial(_mlstm_sample, mhg=p["mhg"], c0=state_C[0], n0=state_n[0],
                                      m0=state_m[0], batch=bs, steps=ss))
    ks, vs, lfs, kbs, vbs, qbs, ogs = _shared_and_q(hs2, p, tm)
    os_ = _attn_sample(qbs, kbs, vbs, lfs, ogs, cache_k, cache_v, cache_logf, page_table, bs, ss)
    y_sample = _layer1_tail(os_, hs2, p, tm).reshape(bs, ss, d)

    return (y_prompt, y_sample, p_c[None], p_n[None], p_m[None], p_k, p_v, p_lf,
            s_c[None], s_n[None], s_m[None], ks.reshape(bs, ss, H_B, DH_B),
            vs.reshape(bs, ss, H_B, DH_B), lfs.reshape(bs, ss, H_B))
```

```python
import functools

import jax
import jax.numpy as jnp
from jax import lax
from jax.experimental import pallas as pl
from jax.experimental.pallas import tpu as pltpu

F32 = jnp.float32
BF16 = jnp.bfloat16

N_META = 16
H_A = 8
DK_A = 64
DV_A = 128
GATE_CAP = 15.0
H_B = 16
DH_B = 64
EPS = 1e-6

LANES = 128
SUBLANES = 8
VMEM_LIMIT_BYTES = 56 * 1024 * 1024

MLSTM_CHUNK = 128
MLSTM_HEADS_PER_STEP = 4
SAMPLE_CHUNK_ALIGN = 8
ATTN_TQ = 512
ATTN_TK = 256
ATTN_HEADS = 4
KVQ_T_TILE = 384
N_SPLIT = 3
ROWSUM_ROWS = 16
LOG2E = 1.4426950408889634
FF_CHUNK = 256
MASKED_GATE = -1e30
NEG_INIT = -1e30


def _cparams(*sem):
    return pltpu.CompilerParams(dimension_semantics=sem, vmem_limit_bytes=VMEM_LIMIT_BYTES)


def _const_spec(shape):
    nd = len(shape)
    return pl.BlockSpec(shape, lambda *_: (0,) * nd, pipeline_mode=pl.Buffered(1))


def _rms_scale(x):
    return lax.rsqrt(jnp.mean(x * x, axis=-1, keepdims=True) + EPS)


def _log_sigmoid(x):
    return jnp.minimum(x, 0.0) - jnp.log1p(jnp.exp(-jnp.abs(x)))


def _sigmoid(x):
    return 1.0 / (1.0 + jnp.exp(-x))


def _dot(a, b):
    return jnp.dot(a, b, preferred_element_type=F32)


def _dot_nt(a, b):
    return lax.dot_general(a, b, (((1,), (1,)), ((), ())), preferred_element_type=F32)


def _dot_tn(a, b):
    return lax.dot_general(a, b, (((0,), (0,)), ((), ())), preferred_element_type=F32)


def _split3(x):
    hi = x.astype(BF16)
    r1 = x - hi.astype(F32)
    mid = r1.astype(BF16)
    lo = (r1 - mid.astype(F32)).astype(BF16)
    return hi, mid, lo


def _dot_by_01(x, m01):
    hi, mid, lo = _split3(x)
    return _dot(hi, m01) + _dot(mid, m01) + _dot(lo, m01)


def _iota(shape, dim):
    return lax.broadcasted_iota(jnp.int32, shape, dim)


def _proj_in_body(x_ref, g_ref, wqk_ref, wv_ref, wog_ref, wg_ref, bg_ref, qs_ref,
                  qk_ref, v_ref, og_ref, gt_ref):
    x = x_ref[...]
    xn = (x * _rms_scale(x) * g_ref[...]).astype(BF16)
    qk_ref[...] = (_dot(xn, wqk_ref[...]) * qs_ref[...]).astype(BF16)
    v_ref[...] = _dot(xn, wv_ref[...]).astype(BF16)
    og_ref[...] = _dot(xn, wog_ref[...]).astype(BF16)
    z = _dot(xn, wg_ref[...]) + bg_ref[...]
    cap = GATE_CAP * jnp.tanh(z / GATE_CAP)
    lane = _iota(cap.shape, 1)
    gt_ref[...] = jnp.where(lane < H_A, cap, _log_sigmoid(cap))


def _proj_in(x, g, wqk, wv, wog, wg, bg, qs, tm):
    n, d = x.shape
    row = lambda i: (i, 0)
    return pl.pallas_call(
        _proj_in_body,
        grid=(pl.cdiv(n, tm),),
        in_specs=[pl.BlockSpec((tm, d), row), _const_spec(g.shape), _const_spec(wqk.shape),
                  _const_spec(wv.shape), _const_spec(wog.shape), _const_spec(wg.shape),
                  _const_spec(bg.shape), _const_spec(qs.shape)],
        out_specs=[pl.BlockSpec((tm, wqk.shape[1]), row), pl.BlockSpec((tm, wv.shape[1]), row),
                   pl.BlockSpec((tm, wog.shape[1]), row), pl.BlockSpec((tm, LANES), row)],
        out_shape=[jax.ShapeDtypeStruct((n, wqk.shape[1]), BF16),
                   jax.ShapeDtypeStruct((n, wv.shape[1]), BF16),
                   jax.ShapeDtypeStruct((n, wog.shape[1]), BF16),
                   jax.ShapeDtypeStruct((n, LANES), F32)],
        compiler_params=_cparams("parallel"),
        name="proj_in",
    )(x, g, wqk, wv, wog, wg, bg, qs)


def _mix_ffn_body(a_ref, h_ref, wo_ref, gf_ref, wgu_ref, wd_ref, gout_ref, o_ref, *, d_ff, final):
    h1 = h_ref[...] + _dot(a_ref[...], wo_ref[...])
    xn = (h1 * _rms_scale(h1) * gf_ref[...]).astype(BF16)
    acc = h1
    for c in range(d_ff // FF_CHUNK):
        lo = c * FF_CHUNK
        gate = _dot(xn, wgu_ref[:, lo:lo + FF_CHUNK])
        up = _dot(xn, wgu_ref[:, d_ff + lo:d_ff + lo + FF_CHUNK])
        act = (gate * _sigmoid(gate) * up).astype(BF16)
        acc = acc + _dot(act, wd_ref[lo:lo + FF_CHUNK, :])
    if final:
        acc = acc * _rms_scale(acc) * gout_ref[...]
    o_ref[...] = acc


def _mix_ffn(a, h, wo, gf, wgu, wd, gout, tm, final):
    n, d = h.shape
    d_ff = wd.shape[0]
    row = lambda i: (i, 0)
    return pl.pallas_call(
        functools.partial(_mix_ffn_body, d_ff=d_ff, final=final),
        grid=(pl.cdiv(n, tm),),
        in_specs=[pl.BlockSpec((tm, a.shape[1]), row), pl.BlockSpec((tm, d), row),
                  _const_spec(wo.shape), _const_spec(gf.shape), _const_spec(wgu.shape),
                  _const_spec(wd.shape), _const_spec(gout.shape)],
        out_specs=pl.BlockSpec((tm, d), row),
        out_shape=jax.ShapeDtypeStruct((n, d), F32),
        compiler_params=_cparams("parallel"),
        name="mix_ffn_final" if final else "mix_ffn",
    )(a, h, wo, gf, wgu, wd, gout)


def _head_rmsnorm64(x):
    outs = []
    for j in range(x.shape[1] // LANES):
        blk = x[:, j * LANES:(j + 1) * LANES]
        sq = blk * blk
        lane = _iota(blk.shape, 1)
        s_all = jnp.sum(sq, axis=1, keepdims=True)
        s_lo = jnp.sum(jnp.where(lane < DH_B, sq, 0.0), axis=1, keepdims=True)
        ms = jnp.where(lane < DH_B, s_lo, s_all - s_lo) / DH_B
        outs.append(blk * lax.rsqrt(ms + EPS))
    return jnp.concatenate(outs, axis=1)


def _kvq_compute(h, gkv_ref, wk_ref, wv_ref, wf_ref, bf_ref, kg_ref, gb_ref, wq_ref, wog_ref, qg_ref):
    hr = h * _rms_scale(h)
    xs = (hr * gkv_ref[...]).astype(BF16)
    k = _head_rmsnorm64(_dot(xs, wk_ref[...])) * kg_ref[...]
    v = _dot(xs, wv_ref[...])
    lf = _log_sigmoid(_dot(xs, wf_ref[...]) + bf_ref[...])
    xq = (hr * gb_ref[...]).astype(BF16)
    q = _head_rmsnorm64(_dot(xq, wq_ref[...])) * qg_ref[...]
    og = _dot(xq, wog_ref[...])
    return k, v, lf, q, og


def _kvq_body(h_ref, gkv_ref, wk_ref, wv_ref, wf_ref, bf_ref, kg_ref, gb_ref, wq_ref, wog_ref,
              qg_ref, k_ref, v_ref, lf_ref, kb_ref, vb_ref, qb_ref, og_ref):
    k, v, lf, q, og = _kvq_compute(h_ref[...], gkv_ref, wk_ref, wv_ref, wf_ref, bf_ref, kg_ref,
                                   gb_ref, wq_ref, wog_ref, qg_ref)
    k_ref[...] = k
    v_ref[...] = v
    kb_ref[...] = k.astype(BF16)
    vb_ref[...] = v.astype(BF16)
    lf_ref[...] = lf[:, :H_B]
    qb_ref[...] = q.astype(BF16)
    og_ref[...] = og.astype(BF16)


def _kvq_t_body(h_ref, gkv_ref, wk_ref, wv_ref, wf_ref, bf_ref, kg_ref, gb_ref, wq_ref, wog_ref,
                qg_ref, kt_ref, vt_ref, lft_ref, lf_ref, kb_ref, vtb_ref, qt_ref, og_ref, *, seq):
    tm = h_ref.shape[1]
    k, v, lf, q, og = _kvq_compute(h_ref[0], gkv_ref, wk_ref, wv_ref, wf_ref, bf_ref, kg_ref,
                                   gb_ref, wq_ref, wog_ref, qg_ref)
    valid = pl.program_id(1) * tm + _iota((tm, 1), 0) < seq
    k = jnp.where(valid, k, 0.0)
    v = jnp.where(valid, v, 0.0)
    lf = jnp.where(valid, lf, 0.0)
    vt = v.T
    kt_ref[0] = k.T
    vt_ref[0] = vt
    vtb_ref[0] = vt.astype(BF16)
    qt_ref[0] = jnp.where(valid, q, 0.0).T.astype(BF16)
    kb_ref[0] = k.astype(BF16)
    lf_ref[0] = lf[:, :H_B]
    lft_ref[0] = lf.T[:H_B, :]
    og_ref[0] = og.astype(BF16)


def _kvq_proj_t(h, gkv, wk, wv, wf, bf, kg, gb, wq, wog, qg, batch, seq):
    d = h.shape[1]
    hd = wk.shape[1]
    tm = KVQ_T_TILE
    nt = pl.cdiv(seq, tm)
    tpad = nt * tm
    rows = lambda b, i: (b, i, 0)
    cols = lambda b, i: (b, 0, i)
    return pl.pallas_call(
        functools.partial(_kvq_t_body, seq=seq),
        grid=(batch, nt),
        in_specs=[pl.BlockSpec((1, tm, d), rows)] + [_const_spec(w.shape) for w in
                                                      (gkv, wk, wv, wf, bf, kg, gb, wq, wog, qg)],
        out_specs=[pl.BlockSpec((1, hd, tm), cols), pl.BlockSpec((1, hd, tm), cols),
                   pl.BlockSpec((1, H_B, tm), cols), pl.BlockSpec((1, tm, H_B), rows),
                   pl.BlockSpec((1, tm, hd), rows), pl.BlockSpec((1, hd, tm), cols),
                   pl.BlockSpec((1, hd, tm), cols), pl.BlockSpec((1, tm, wog.shape[1]), rows)],
        out_shape=[jax.ShapeDtypeStruct((batch, hd, seq), F32), jax.ShapeDtypeStruct((batch, hd, seq), F32),
                   jax.ShapeDtypeStruct((batch, H_B, seq), F32), jax.ShapeDtypeStruct((batch, tpad, H_B), F32),
                   jax.ShapeDtypeStruct((batch, tpad, hd), BF16), jax.ShapeDtypeStruct((batch, hd, tpad), BF16),
                   jax.ShapeDtypeStruct((batch, hd, tpad), BF16),
                   jax.ShapeDtypeStruct((batch, seq, wog.shape[1]), BF16)],
        compiler_params=_cparams("parallel", "parallel"),
        name="kvq_proj_t",
    )(h.reshape(batch, seq, d), gkv, wk, wv, wf, bf, kg, gb, wq, wog, qg)


def _kvq_proj(h, gkv, wk, wv, wf, bf, kg, gb, wq, wog, qg, tm):
    n, d = h.shape
    hd = wk.shape[1]
    row = lambda i: (i, 0)
    wide = pl.BlockSpec((tm, hd), row)
    return pl.pallas_call(
        _kvq_body,
        grid=(pl.cdiv(n, tm),),
        in_specs=[pl.BlockSpec((tm, d), row)] + [_const_spec(w.shape) for w in
                                                 (gkv, wk, wv, wf, bf, kg, gb, wq, wog, qg)],
        out_specs=[wide, wide, pl.BlockSpec((tm, H_B), row), wide, wide, wide,
                   pl.BlockSpec((tm, wog.shape[1]), row)],
        out_shape=[jax.ShapeDtypeStruct((n, hd), F32), jax.ShapeDtypeStruct((n, hd), F32),
                   jax.ShapeDtypeStruct((n, H_B), F32), jax.ShapeDtypeStruct((n, hd), BF16),
                   jax.ShapeDtypeStruct((n, hd), BF16), jax.ShapeDtypeStruct((n, hd), BF16),
                   jax.ShapeDtypeStruct((n, wog.shape[1]), BF16)],
        compiler_params=_cparams("parallel"),
        name="kvq_proj",
    )(h, gkv, wk, wv, wf, bf, kg, gb, wq, wog, qg)


def _col_to_row(col):
    n = col.shape[0]
    eye = _iota((n, n), 0) == _iota((n, n), 1)
    return jnp.sum(jnp.where(eye, col, 0.0), axis=0, keepdims=True)


def _row_to_col(row):
    n = row.shape[1]
    eye = _iota((n, n), 0) == _iota((n, n), 1)
    return jnp.sum(jnp.where(eye, row, 0.0), axis=1, keepdims=True)


def _mlstm_chunk_heads(qk, v, li, lf, cfull, m):
    heads = range(len(qk))
    L = qk[0].shape[0]
    tril = _iota((L, L), 0) >= _iota((L, L), 1)
    lane = _iota((L, LANES), 1)
    ones_col = jnp.where(lane == 0, 1.0, 0.0)

    q_lo = [jnp.where(lane < DK_A, qk[i], 0.0).astype(BF16) for i in heads]
    kq = [pltpu.roll(qk[i], DK_A, 1) for i in heads]
    vaug = [jnp.concatenate([v[i], ones_col], axis=1).astype(BF16) for i in heads]
    qk_t = [_dot_nt(q_lo[i], kq[i].astype(BF16)) for i in heads]
    q_c = [_dot(q_lo[i], cfull[i].astype(BF16)) for i in heads]

    lf_row = [_col_to_row(lf[i]) for i in heads]
    li_row = [_col_to_row(li[i]) for i in heads]
    b = [jnp.sum(jnp.where(tril, lf_row[i], 0.0), axis=1, keepdims=True) for i in heads]
    b_row = [_col_to_row(b[i]) for i in heads]
    dmat = [jnp.where(tril, b[i] - b_row[i] + li_row[i], -jnp.inf) for i in heads]
    dmax = [jnp.max(dmat[i], axis=1, keepdims=True) for i in heads]

    b_end = [b[i][L - 1:L, :] for i in heads]
    g = [b_end[i] - b[i] + li[i] for i in heads]
    m_new = [jnp.maximum(b_end[i] + m[i], jnp.max(g[i], axis=0, keepdims=True)) for i in heads]
    w_c = [jnp.exp(b_end[i] + m[i] - m_new[i]) for i in heads]
    upd = [_dot_tn((jnp.exp(g[i] - m_new[i]) * kq[i]).astype(BF16), vaug[i]) for i in heads]
    keep = _iota(cfull[0].shape, 0) < DK_A
    cfull_new = [jnp.where(keep, w_c[i] * cfull[i] + upd[i], 0.0) for i in heads]

    inter = [b[i] + m[i] for i in heads]
    m_t = [jnp.maximum(inter[i], dmax[i]) for i in heads]
    s = [(qk_t[i] * jnp.exp(dmat[i] - m_t[i])).astype(BF16) for i in heads]
    tot = [jnp.exp(inter[i] - m_t[i]) * q_c[i] + _dot(s[i], vaug[i]) for i in heads]
    h = [tot[i][:, :DV_A] / jnp.maximum(jnp.abs(tot[i][:, DV_A:DV_A + 1]), jnp.exp(-m_t[i])) for i in heads]
    return h, cfull_new, m_new


def _mlstm_head_out(h, og, gain):
    return h * _rms_scale(h) * gain * _sigmoid(og)


def _mlstm_prompt_body(qk_ref, v_ref, og_ref, gt_ref, mhg_ref, hg_ref, c_ref, n_ref, m_ref,
                       cst, mst, *, seq, hb):
    grp = pl.program_id(1)
    ch = MLSTM_CHUNK
    nfull, tail = seq // ch, seq % ch
    cst[...] = jnp.zeros(cst.shape, F32)
    mst[...] = jnp.zeros(mst.shape, F32)

    def chunk(r0, first_valid):
        gt = gt_ref[0, pl.ds(r0, ch), :]
        lane = _iota(gt.shape, 1)
        rowi = _iota((ch, 1), 0)
        li, lf, qk, v = [], [], [], []
        for hh in range(hb):
            head = grp * hb + hh
            li_h = jnp.sum(jnp.where(lane == head, gt, 0.0), axis=1, keepdims=True)
            lf_h = jnp.sum(jnp.where(lane == head + H_A, gt, 0.0), axis=1, keepdims=True)
            if first_valid:
                li_h = jnp.where(rowi >= first_valid, li_h, MASKED_GATE)
                lf_h = jnp.where(rowi >= first_valid, lf_h, 0.0)
            sl = slice(hh * LANES, (hh + 1) * LANES)
            li.append(li_h)
            lf.append(lf_h)
            qk.append(qk_ref[0, pl.ds(r0, ch), sl].astype(F32))
            v.append(v_ref[0, pl.ds(r0, ch), sl].astype(F32))
        hs, cnew, mnew = _mlstm_chunk_heads(qk, v, li, lf, [cst[hh] for hh in range(hb)],
                                            [mst[hh, 0:1, 0:1] for hh in range(hb)])
        outs = []
        for hh in range(hb):
            cst[hh] = cnew[hh]
            mst[hh] = jnp.broadcast_to(mnew[hh], mst.shape[1:])
            og = og_ref[0, pl.ds(r0, ch), hh * LANES:(hh + 1) * LANES].astype(F32)
            outs.append(_mlstm_head_out(hs[hh], og, mhg_ref[hh]))
        return jnp.concatenate(outs, axis=1).astype(BF16)

    def loop_body(j, carry):
        r0 = pl.multiple_of(j * ch, ch)
        hg_ref[0, pl.ds(r0, ch), :] = chunk(r0, 0)
        return carry

    lax.fori_loop(0, nfull, loop_body, 0)
    if tail:
        out = chunk(seq - ch, ch - tail)
        hg_ref[0, seq - tail:seq, :] = out[ch - tail:, :]

    for hh in range(hb):
        cfull = cst[hh]
        c_ref[0, hh] = cfull[:DK_A, :DV_A]
        n_ref[0, hh] = _col_to_row(cfull[:DK_A, DV_A:DV_A + 1])
        m_ref[0, hh] = mst[hh, 0:1, 0:1]


def _mlstm_prompt(qk, v, og, gt, mhg, batch, seq):
    hb = MLSTM_HEADS_PER_STEP
    w = hb * LANES
    qk3, v3, og3 = (a.reshape(batch, seq, a.shape[-1]) for a in (qk, v, og))
    gt3 = gt.reshape(batch, seq, LANES)
    seq_blk = lambda b, g: (b, 0, g)
    hg, c, n, m = pl.pallas_call(
        functools.partial(_mlstm_prompt_body, seq=seq, hb=hb),
        grid=(batch, H_A // hb),
        in_specs=[pl.BlockSpec((1, seq, w), seq_blk), pl.BlockSpec((1, seq, w), seq_blk),
                  pl.BlockSpec((1, seq, w), seq_blk),
                  pl.BlockSpec((1, seq, LANES), lambda b, g: (b, 0, 0)),
                  pl.BlockSpec((hb, 1, DV_A), lambda b, g: (g, 0, 0))],
        out_specs=[pl.BlockSpec((1, seq, w), seq_blk),
                   pl.BlockSpec((1, hb, DK_A, DV_A), lambda b, g: (b, g, 0, 0)),
                   pl.BlockSpec((1, hb, 1, DK_A), lambda b, g: (b, g, 0, 0)),
                   pl.BlockSpec((1, hb, 1, 1), lambda b, g: (b, g, 0, 0))],
        out_shape=[jax.ShapeDtypeStruct((batch, seq, H_A * DV_A), BF16),
                   jax.ShapeDtypeStruct((batch, H_A, DK_A, DV_A), F32),
                   jax.ShapeDtypeStruct((batch, H_A, 1, DK_A), F32),
                   jax.ShapeDtypeStruct((batch, H_A, 1, 1), F32)],
        scratch_shapes=[pltpu.VMEM((hb, LANES, 2 * LANES), F32), pltpu.VMEM((hb, SUBLANES, LANES), F32)],
        compiler_params=_cparams("parallel", "arbitrary"),
        name="mlstm_prompt",
    )(qk3, v3, og3, gt3, mhg.reshape(H_A, 1, DV_A))
    return (hg.reshape(batch * seq, H_A * DV_A), c, n.reshape(batch, H_A, DK_A),
            m.reshape(batch, H_A))


def _mlstm_sample_body(qk_ref, v_ref, og_ref, gt_ref, mhg_ref, c0_ref, n0_ref, m0_ref,
                       hg_ref, c_ref, n_ref, m_ref, *, steps):
    ch = pl.cdiv(steps, SAMPLE_CHUNK_ALIGN) * SAMPLE_CHUNK_ALIGN
    pad = jnp.zeros((ch - steps, LANES), F32)
    padded = (lambda a: jnp.concatenate([a, pad], axis=0)) if ch > steps else (lambda a: a)
    rowi = _iota((ch, 1), 0)
    gt = padded(gt_ref[0])
    lane1 = _iota((1, LANES), 1)
    lane_c = _iota((DK_A, LANES), 1)
    m_out = jnp.zeros((1, LANES), F32)
    li, lf, qk, v, cfull, m0 = [], [], [], [], [], []
    for hh in range(H_A):
        sl = slice(hh * LANES, (hh + 1) * LANES)
        li.append(jnp.where(rowi < steps, gt[:, hh:hh + 1], MASKED_GATE))
        lf.append(jnp.where(rowi < steps, gt[:, H_A + hh:H_A + hh + 1], 0.0))
        qk.append(padded(qk_ref[0, :, sl].astype(F32)))
        v.append(padded(v_ref[0, :, sl].astype(F32)))
        ncol = _row_to_col(n0_ref[0, hh:hh + 1, :])
        top = jnp.concatenate([c0_ref[0, hh], jnp.where(lane_c == 0, ncol, 0.0)], axis=1)
        cfull.append(jnp.concatenate([top, jnp.zeros((LANES - DK_A, 2 * LANES), F32)], axis=0))
        m0.append(m0_ref[0, :, hh:hh + 1])
    hs, cnew, mnew = _mlstm_chunk_heads(qk, v, li, lf, cfull, m0)
    outs = []
    for hh in range(H_A):
        og = og_ref[0, :, hh * LANES:(hh + 1) * LANES].astype(F32)
        outs.append(_mlstm_head_out(hs[hh][:steps, :], og, mhg_ref[hh]))
        c_ref[0, hh] = cnew[hh][:DK_A, :DV_A]
        n_ref[0, hh:hh + 1, :] = _col_to_row(cnew[hh][:DK_A, DV_A:DV_A + 1])
        m_out = jnp.where(lane1 == hh, mnew[hh], m_out)
    hg_ref[0] = jnp.concatenate(outs, axis=1).astype(BF16)
    m_ref[0] = m_out[:, :H_A]


def _mlstm_sample(qk, v, og, gt, mhg, c0, n0, m0, batch, steps):
    wide = H_A * LANES
    blk3 = lambda b: (b, 0, 0)
    hg, c, n, m = pl.pallas_call(
        functools.partial(_mlstm_sample_body, steps=steps),
        grid=(batch,),
        in_specs=[pl.BlockSpec((1, steps, wide), blk3), pl.BlockSpec((1, steps, wide), blk3),
                  pl.BlockSpec((1, steps, wide), blk3), pl.BlockSpec((1, steps, LANES), blk3),
                  _const_spec((H_A, 1, DV_A)),
                  pl.BlockSpec((1, H_A, DK_A, DV_A), lambda b: (b, 0, 0, 0)),
                  pl.BlockSpec((1, H_A, DK_A), blk3), pl.BlockSpec((1, 1, H_A), blk3)],
        out_specs=[pl.BlockSpec((1, steps, wide), blk3),
                   pl.BlockSpec((1, H_A, DK_A, DV_A), lambda b: (b, 0, 0, 0)),
                   pl.BlockSpec((1, H_A, DK_A), blk3), pl.BlockSpec((1, 1, H_A), blk3)],
        out_shape=[jax.ShapeDtypeStruct((batch, steps, wide), BF16),
                   jax.ShapeDtypeStruct((batch, H_A, DK_A, DV_A), F32),
                   jax.ShapeDtypeStruct((batch, H_A, DK_A), F32),
                   jax.ShapeDtypeStruct((batch, 1, H_A), F32)],
        compiler_params=_cparams("parallel"),
        name="mlstm_sample",
    )(qk.reshape(batch, steps, wide), v.reshape(batch, steps, wide), og.reshape(batch, steps, wide),
      gt.reshape(batch, steps, LANES), mhg.reshape(H_A, 1, DV_A), c0, n0, m0.reshape(batch, 1, H_A))
    return hg.reshape(batch * steps, wide), c, n, m.reshape(batch, H_A)


def _rows_to_lanes(x16, staging_ref):
    staging_ref[...] = jnp.zeros(staging_ref.shape, F32)
    staging_ref[:, 0:x16.shape[1]] = x16
    return staging_ref[...].T[0:x16.shape[1], :]


def _placement():
    h = jnp.arange(H_B)[None, :, None]
    part = jnp.arange(N_SPLIT)[:, None, None]
    col = jnp.arange(H_B * DH_B)[None, None, :]
    lane = jnp.where(h % 2 == 0, DH_B, 0) + part
    return (col == (h // 2) * LANES + lane).astype(BF16)


def _key_aug_body(lf_ref, kb_ref, place_ref, k0_ref, k1_ref, carry):
    @pl.when(pl.program_id(1) == 0)
    def _():
        carry[...] = jnp.zeros(carry.shape, F32)

    tc, wide = kb_ref.shape[1], kb_ref.shape[2]
    tril = jnp.where(_iota((LANES, LANES), 0) >= _iota((LANES, LANES), 1), 1.0, 0.0).astype(BF16)
    lo_half = (_iota((LANES, wide), 1) & (LANES - 1)) < DH_B
    for sub in range(tc // LANES):
        rs = slice(sub * LANES, (sub + 1) * LANES)
        hi, mid, lo = _split3(lf_ref[0, rs, :])
        c = _dot(tril, hi) + _dot(tril, mid) + _dot(tril, lo) + carry[0:1, 0:H_B]
        carry[0:1, 0:H_B] = c[LANES - 1:LANES, :]
        parts = _split3(c * LOG2E)
        bias = sum(_dot(parts[j], place_ref[j]) for j in range(N_SPLIT))
        k = kb_ref[0, rs, :].astype(F32)
        k0_ref[0, rs, :] = jnp.where(lo_half, k, bias).astype(BF16)
        k1_ref[0, rs, :] = jnp.where(lo_half, bias, k).astype(BF16)


def _key_aug(lf, kb):
    batch, tpad, wide = kb.shape
    tc = KVQ_T_TILE
    blk = lambda b, i: (b, i, 0)
    place = _placement()
    return pl.pallas_call(
        _key_aug_body,
        grid=(batch, tpad // tc),
        in_specs=[pl.BlockSpec((1, tc, H_B), blk), pl.BlockSpec((1, tc, wide), blk),
                  _const_spec(place.shape)],
        out_specs=[pl.BlockSpec((1, tc, wide), blk), pl.BlockSpec((1, tc, wide), blk)],
        out_shape=[jax.ShapeDtypeStruct((batch, tpad, wide), BF16)] * 2,
        scratch_shapes=[pltpu.VMEM((SUBLANES, LANES), F32)],
        compiler_params=_cparams("parallel", "arbitrary"),
        name="key_aug",
    )(lf, kb, place)


def _attn_prompt_body(qt_ref, k0_ref, k1_ref, vt_ref, og_ref, o_ref, s_scr, p_scr, acc_scr, qa_scr,
                      mask_scr, *, seq):
    tq, tk = ATTN_TQ, ATTN_TK
    tpad = qt_ref.shape[2]
    nfull = seq // tq
    nh = ATTN_HEADS
    heads = range(nh)

    def keys(h, rows):
        pair = slice((h // 2) * LANES, (h // 2 + 1) * LANES)
        return (k0_ref if h % 2 == 0 else k1_ref)[0, rows, pair]

    def augmented_queries(qt):
        out = []
        for pp in range(nh // 2):
            blk = qt[pp * LANES:(pp + 1) * LANES, :]
            row = _iota(blk.shape, 0)
            out.append(jnp.where(row < DH_B, blk, jnp.where(row < DH_B + N_SPLIT, -1.0, 0.0)).astype(BF16))
            out.append(jnp.where(row >= DH_B, blk, jnp.where(row < N_SPLIT, -1.0, 0.0)).astype(BF16))
        return out

    def with_ones(vt):
        return jnp.concatenate([vt, jnp.ones((ROWSUM_ROWS, vt.shape[1]), BF16)], axis=0)

    def tail_tile(q0, width, rows_out):
        qa = augmented_queries(qt_ref[0, :, pl.ds(q0, width)].astype(F32))
        causal = _iota((tpad, width), 0) <= q0 + _iota((tpad, width), 1)
        outs = []
        for hh in heads:
            s = jnp.where(causal, _dot(keys(hh, slice(None)), qa[hh]), -jnp.inf)
            p = jnp.exp2(s - jnp.max(s, axis=0, keepdims=True)).astype(BF16)
            full = _dot(with_ones(vt_ref[0, hh * DH_B:(hh + 1) * DH_B, :]), p)
            outs.append(full[:DH_B, :] / full[DH_B:DH_B + 1, :])
        out = jnp.concatenate(outs, axis=0).T[:rows_out, :]
        gate = _sigmoid(og_ref[0, pl.ds(q0, rows_out), :].astype(F32))
        o_ref[0, pl.ds(q0, rows_out), :] = (out * gate).astype(BF16)

    per = tq // tk
    for d in range(per):
        key = d * tk + _iota((tk, tq), 0)
        mask_scr[d] = jnp.where(key <= _iota((tk, tq), 1), 0.0, -jnp.inf)

    def value_product(hh, k0, slot):
        return _dot(with_ones(vt_ref[0, hh * DH_B:(hh + 1) * DH_B, pl.ds(k0, tk)]), p_scr[slot, hh])

    def pipe_step(ms, kidx, slot, diag_idx, issue_next):
        k0 = pl.multiple_of(kidx * tk, tk)
        if issue_next:
            for hh in heads:
                s_scr[1 - slot, hh] = _dot(keys(hh, pl.ds(k0 + tk, tk)), qa_scr[hh])
        kprev = pl.multiple_of(jnp.maximum(kidx - 1, 0) * tk, tk)
        pvs = [value_product(hh, kprev, 1 - slot) for hh in heads]
        new = []
        for hh in heads:
            s = s_scr[slot, hh]
            if diag_idx is not None:
                s = s + mask_scr[diag_idx]
            m_new = jnp.maximum(ms[hh], jnp.max(s, axis=0, keepdims=True))
            p_scr[slot, hh] = jnp.exp2(s - m_new).astype(BF16)
            acc_scr[hh] = (acc_scr[hh] + pvs[hh]) * jnp.exp2(ms[hh] - m_new)
            new.append(m_new)
        return tuple(new)

    def q_tile_pipelined(i):
        q0 = pl.multiple_of(i * tq, tq)
        qa = augmented_queries(qt_ref[0, :, pl.ds(q0, tq)].astype(F32))
        for hh in heads:
            qa_scr[hh] = qa[hh]
        p_scr[1] = jnp.zeros(p_scr.shape[1:], BF16)
        acc_scr[...] = jnp.zeros(acc_scr.shape, F32)
        for hh in heads:
            s_scr[0, hh] = _dot(keys(hh, pl.ds(0, tk)), qa_scr[hh])
        ms = tuple(jnp.full((1, tq), NEG_INIT, F32) for _ in heads)

        def group(jj, st):
            for d in range(per):
                st = pipe_step(st, jj * per + d, d % 2, None, True)
            return st

        ms = lax.fori_loop(0, i, group, ms)
        for d in range(per):
            ms = pipe_step(ms, i * per + d, d % 2, d, d < per - 1)
        last = (per - 1) % 2
        klast = pl.multiple_of((i * per + per - 1) * tk, tk)
        outs = []
        for hh in heads:
            full = acc_scr[hh] + value_product(hh, klast, last)
            outs.append(full[:DH_B, :] / full[DH_B:DH_B + 1, :])
        out = jnp.concatenate(outs, axis=0).T
        gate = _sigmoid(og_ref[0, pl.ds(q0, tq), :].astype(F32))
        o_ref[0, pl.ds(q0, tq), :] = (out * gate).astype(BF16)

    def qbody(i, _):
        q_tile_pipelined(i)
        return 0

    lax.fori_loop(0, nfull, qbody, 0)
    if seq > nfull * tq:
        q0 = nfull * tq
        tail_tile(q0, tpad - q0, seq - q0)


def _attn_prompt(qt, k0, k1, vt, og, seq):
    batch, wide, tpad = qt.shape
    tail_w = tpad - (seq // ATTN_TQ) * ATTN_TQ
    assert ATTN_TQ % (2 * ATTN_TK) == 0 and 0 <= tail_w and tail_w % LANES == 0
    rows = lambda b, p: (b, 0, p)
    cols = lambda b, p: (b, p, 0)
    nh = ATTN_HEADS
    w = nh * DH_B
    out = pl.pallas_call(
        functools.partial(_attn_prompt_body, seq=seq),
        grid=(batch, H_B // nh),
        in_specs=[pl.BlockSpec((1, w, tpad), cols), pl.BlockSpec((1, tpad, w), rows),
                  pl.BlockSpec((1, tpad, w), rows), pl.BlockSpec((1, w, tpad), cols),
                  pl.BlockSpec((1, seq, w), rows)],
        out_specs=pl.BlockSpec((1, seq, w), rows),
        out_shape=jax.ShapeDtypeStruct((batch, seq, wide), BF16),
        scratch_shapes=[pltpu.VMEM((2, nh, ATTN_TK, ATTN_TQ), F32),
                        pltpu.VMEM((2, nh, ATTN_TK, ATTN_TQ), BF16),
                        pltpu.VMEM((nh, DH_B + ROWSUM_ROWS, ATTN_TQ), F32),
                        pltpu.VMEM((nh, LANES, ATTN_TQ), BF16),
                        pltpu.VMEM((ATTN_TQ // ATTN_TK, ATTN_TK, ATTN_TQ), F32)],
        compiler_params=_cparams("parallel", "parallel"),
        name="attn_prompt",
    )(qt, k0, k1, vt, og)
    return out.reshape(batch * seq, wide)


def _attn_sample_body(pt_ref, q_ref, kn_ref, vn_ref, lfn_ref, og_ref, *rest, steps, npages):
    k_refs, v_refs, lf_refs = rest[:npages], rest[npages:2 * npages], rest[2 * npages:3 * npages]
    o_ref, stage = rest[3 * npages], rest[3 * npages + 1]
    rows = H_B * steps
    wide = H_B * DH_B
    u = _iota((LANES, LANES), 0)
    s_ = _iota((LANES, LANES), 1)

    q = q_ref[0].astype(F32)
    qrep = jnp.concatenate([jnp.broadcast_to(q[t:t + 1, :], (H_B, wide)) for t in range(steps)], axis=0)
    diag = _iota((rows, wide), 0) % H_B == _iota((rows, wide), 1) // DH_B
    qbd = jnp.where(diag, qrep, 0.0).astype(BF16)

    lf_all = jnp.concatenate([r[0] for r in lf_refs], axis=0)
    later_and_ones = jnp.concatenate([jnp.where(u > s_, 1.0, 0.0), jnp.ones((LANES, LANES), F32)],
                                     axis=1).astype(BF16)
    wt = _dot_by_01(lf_all, later_and_ones)
    pr = _iota((npages * H_B, npages * H_B), 0)
    pc = _iota((npages * H_B, npages * H_B), 1)
    later_pages = jnp.where((pc % H_B == pr % H_B) & (pc // H_B > pr // H_B), 1.0, 0.0).astype(BF16)
    hi, mid, lo = _split3(wt[:, LANES:])
    rsum = wt[:, :LANES] + _dot(later_pages, hi) + _dot(later_pages, mid) + _dot(later_pages, lo)
    bias_past = jnp.concatenate(
        [jnp.concatenate([rsum[r * H_B:(r + 1) * H_B, :]] * steps, axis=0) for r in range(npages)], axis=1)

    kcat = jnp.concatenate([r[0].astype(BF16) for r in k_refs], axis=1)
    s_past = _dot(qbd, kcat) + bias_past

    zpad = jnp.zeros((LANES - steps, wide), F32)
    kn = jnp.concatenate([kn_ref[0].astype(F32), zpad], axis=0).astype(BF16)
    vn = jnp.concatenate([vn_ref[0].astype(F32), zpad], axis=0).astype(BF16)
    lfn = jnp.concatenate([lfn_ref[0], jnp.zeros((LANES - steps, H_B), F32)], axis=0)
    incl = jnp.where(u <= s_, 1.0, 0.0).astype(BF16)
    cnew = _dot_by_01(_rows_to_lanes(lfn, stage), incl)
    key = _iota((rows, LANES), 1)
    qry = _iota((rows, LANES), 0) // H_B
    bias_new = jnp.where(key <= qry, -jnp.concatenate([cnew] * steps, axis=0), -jnp.inf)
    s_new = _dot_nt(qbd, kn) + bias_new

    m = jnp.maximum(jnp.max(s_past, axis=1, keepdims=True), jnp.max(s_new, axis=1, keepdims=True))
    p_past = jnp.exp(s_past - m)
    p_new = jnp.exp(s_new - m)
    l = jnp.sum(p_past, axis=1, keepdims=True) + jnp.sum(p_new, axis=1, keepdims=True)
    vcat = jnp.concatenate([r[0].astype(BF16) for r in v_refs], axis=1)
    acc = _dot_nt(p_past.astype(BF16), vcat) + _dot(p_new.astype(BF16), vn)
    full = jnp.where(diag, acc / l, 0.0)
    out = jnp.concatenate([jnp.sum(full[t * H_B:(t + 1) * H_B, :], axis=0, keepdims=True)
                           for t in range(steps)], axis=0)
    o_ref[0] = (out * _sigmoid(og_ref[0].astype(F32))).astype(BF16)


def _attn_sample(q, kn, vn, lfn, og, cache_k, cache_v, cache_logf, page_table, batch, steps):
    wide = H_B * DH_B
    n_phys, page = cache_k.shape[0], cache_k.shape[1]
    npages = page_table.shape[1]
    assert page == LANES
    ck = jnp.transpose(cache_k, (0, 2, 3, 1)).reshape(n_phys, wide, page)
    cv = jnp.transpose(cache_v, (0, 2, 3, 1)).reshape(n_phys, wide, page)
    clf = jnp.transpose(cache_logf, (0, 2, 1))
    tok = lambda b, pt: (b, 0, 0)
    page_of = lambda r: (lambda b, pt: (pt[b, r], 0, 0))
    out = pl.pallas_call(
        functools.partial(_attn_sample_body, steps=steps, npages=npages),
        grid_spec=pltpu.PrefetchScalarGridSpec(
            num_scalar_prefetch=1,
            grid=(batch,),
            in_specs=[pl.BlockSpec((1, steps, wide), tok), pl.BlockSpec((1, steps, wide), tok),
                      pl.BlockSpec((1, steps, wide), tok), pl.BlockSpec((1, steps, H_B), tok),
                      pl.BlockSpec((1, steps, wide), tok)]
            + [pl.BlockSpec((1, wide, page), page_of(r)) for r in range(npages)]
            + [pl.BlockSpec((1, wide, page), page_of(r)) for r in range(npages)]
            + [pl.BlockSpec((1, H_B, page), page_of(r)) for r in range(npages)],
            out_specs=pl.BlockSpec((1, steps, wide), tok),
            scratch_shapes=[pltpu.VMEM((LANES, LANES), F32)]),
        out_shape=jax.ShapeDtypeStruct((batch, steps, wide), BF16),
        compiler_params=_cparams("parallel"),
        name="attn_sample",
    )(page_table, q.reshape(batch, steps, wide), kn.reshape(batch, steps, wide),
      vn.reshape(batch, steps, wide), lfn.reshape(batch, steps, H_B), og.reshape(batch, steps, wide),
      *([ck] * npages), *([cv] * npages), *([clf] * npages))
    return out.reshape(batch * steps, wide)


def _prep_params(norm_a, w_in_a, b_ig_a, b_fg_a, mh_norm_a, w_out_a, norm_kv, w_kvf, b_fg_b,
                 k_norm_b, norm_b, w_qo_b, q_norm_b, w_out_b, norm_ffn, w_gate_up, w_down,
                 norm_final):
    d = w_in_a.shape[1]
    hk, hv, hd = H_A * DK_A, H_A * DV_A, H_B * DH_B
    w_in = w_in_a[0]
    wq = w_in[:, :hk].reshape(d, H_A, DK_A)
    wk = w_in[:, hk:2 * hk].reshape(d, H_A, DK_A)
    row = lambda a: a.reshape(1, -1).astype(F32)
    pad_cols = lambda a: jnp.pad(a, ((0, 0), (0, LANES - a.shape[1])))
    lane = jnp.arange(H_A * LANES) % LANES
    return dict(
        norm_a=row(norm_a[0]),
        wqk=jnp.concatenate([wq, wk], axis=2).reshape(d, H_A * LANES).astype(BF16),
        wv=w_in[:, 2 * hk:2 * hk + hv].astype(BF16),
        wog=w_in[:, 2 * hk + hv:2 * hk + 2 * hv].astype(BF16),
        wg=pad_cols(w_in[:, 2 * hk + 2 * hv:]).astype(BF16),
        bg=pad_cols(jnp.concatenate([b_ig_a[0], b_fg_a[0]]).reshape(1, -1).astype(F32)),
        qs=jnp.where(lane < DK_A, DK_A ** -0.5, 1.0).reshape(1, -1).astype(F32),
        mhg=mh_norm_a[0].astype(F32),
        wo_a=w_out_a[0].astype(BF16),
        gkv=row(norm_kv),
        wk=w_kvf[:, :hd].astype(BF16),
        wvs=w_kvf[:, hd:2 * hd].astype(BF16),
        wf=pad_cols(w_kvf[:, 2 * hd:]).astype(BF16),
        bf=pad_cols(b_fg_b.reshape(1, -1).astype(F32)),
        kg=row(jnp.tile(k_norm_b, H_B)),
        gb=row(norm_b[0]),
        wq=w_qo_b[0][:, :hd].astype(BF16),
        wog_b=w_qo_b[0][:, hd:].astype(BF16),
        qg=row(jnp.tile(q_norm_b[0], H_B)) * DH_B ** -0.5,
        wo_b=w_out_b[0].astype(BF16),
        gf=[row(norm_ffn[l]) for l in range(2)],
        wgu=[w_gate_up[l].astype(BF16) for l in range(2)],
        wd=[w_down[l].astype(BF16) for l in range(2)],
        gfin=row(norm_final),
    )


def _layer0(h, p, tm, mlstm):
    qk, v, og, gt = _proj_in(h, p["norm_a"], p["wqk"], p["wv"], p["wog"], p["wg"], p["bg"], p["qs"], tm)
    hg, c, n, m = mlstm(qk, v, og, gt)
    h2 = _mix_ffn(hg, h, p["wo_a"], p["gf"][0], p["wgu"][0], p["wd"][0], p["gfin"], tm, False)
    return h2, c, n, m


def _shared_and_q(h2, p, tm):
    return _kvq_proj(h2, p["gkv"], p["wk"], p["wvs"], p["wf"], p["bf"], p["kg"], p["gb"],
                     p["wq"], p["wog_b"], p["qg"], tm)


def _layer1_tail(o, h2, p, tm):
    return _mix_ffn(o, h2, p["wo_b"], p["gf"][1], p["wgu"][1], p["wd"][1], p["gfin"], tm, True)


def kernel(x_prompt, x_sample, state_C, state_n, state_m, cache_k, cache_v, cache_logf, page_table,
           meta_tokens, norm_a, w_in_a, b_ig_a, b_fg_a, mh_norm_a, w_out_a, norm_kv, w_kvf, b_fg_b,
           k_norm_b, norm_b, w_qo_b, q_norm_b, w_out_b, norm_ffn, w_gate_up, w_down, norm_final):
    assert w_in_a.shape[0] == 1 and w_qo_b.shape[0] == 1 and norm_ffn.shape[0] == 2
    p = _prep_params(norm_a, w_in_a, b_ig_a, b_fg_a, mh_norm_a, w_out_a, norm_kv, w_kvf, b_fg_b,
                     k_norm_b, norm_b, w_qo_b, q_norm_b, w_out_b, norm_ffn, w_gate_up, w_down,
                     norm_final)
    bp, sp, d = x_prompt.shape
    bs, ss, _ = x_sample.shape
    tp = sp + N_META
    hd = H_B * DH_B
    tm = 512

    meta = jnp.broadcast_to(meta_tokens[None].astype(F32), (bp, N_META, d))
    h0 = jnp.concatenate([meta, x_prompt], axis=1).reshape(bp * tp, d)
    h2, p_c, p_n, p_m = _layer0(h0, p, tm, functools.partial(_mlstm_prompt, mhg=p["mhg"], batch=bp, seq=tp))
    kt, vt, lft, lf, kb, vtb, qt, og = _kvq_proj_t(
        h2, p["gkv"], p["wk"], p["wvs"], p["wf"], p["bf"], p["kg"], p["gb"], p["wq"], p["wog_b"],
        p["qg"] * LOG2E, bp, tp)
    k0, k1 = _key_aug(lf, kb)
    o = _attn_prompt(qt, k0, k1, vtb, og, tp)
    y = _layer1_tail(o, h2, p, tm)
    y_prompt = y.reshape(bp, tp, d)[:, N_META:]
    p_k = jnp.transpose(kt.reshape(bp, H_B, DH_B, tp), (0, 3, 1, 2))
    p_v = jnp.transpose(vt.reshape(bp, H_B, DH_B, tp), (0, 3, 1, 2))
    p_lf = jnp.transpose(lft, (0, 2, 1))

    hs0 = x_sample.reshape(bs * ss, d)
    hs2, s_c, s_n, s_m = _layer0(
        hs0, p, tm, functools.partial(_mlstm_sample, mhg=p["mhg"], c0=state_C[0], n0=state_n[0],
                                      m0=state_m[0], batch=bs, steps=ss))
    ks, vs, lfs, kbs, vbs, qbs, ogs = _shared_and_q(hs2, p, tm)
    os_ = _attn_sample(qbs, kbs, vbs, lfs, ogs, cache_k, cache_v, cache_logf, page_table, bs, ss)
    y_sample = _layer1_tail(os_, hs2, p, tm).reshape(bs, ss, d)

    return (y_prompt, y_sample, p_c[None], p_n[None], p_m[None], p_k, p_v, p_lf,
            s_c[None], s_n[None], s_m[None], ks.reshape(bs, ss, H_B, DH_B),
            vs.reshape(bs, ss, H_B, DH_B), lfs.reshape(bs, ss, H_B))
```

```python
import functools

import jax
import jax.numpy as jnp
from jax import lax
from jax.experimental import pallas as pl
from jax.experimental.pallas import tpu as pltpu

F32 = jnp.float32
BF16 = jnp.bfloat16

N_META = 16
H_A = 8
DK_A = 64
DV_A = 128
GATE_CAP = 15.0
H_B = 16
DH_B = 64
EPS = 1e-6

LANES = 128
SUBLANES = 8
VMEM_LIMIT_BYTES = 56 * 1024 * 1024

MLSTM_CHUNK = 128
MLSTM_HEADS_PER_STEP = 4
SAMPLE_CHUNK_ALIGN = 8
ATTN_TQ = 512
ATTN_TK = 256
ATTN_HEADS = 4
KVQ_T_TILE = 384
N_SPLIT = 3
ROWSUM_ROWS = 16
LOG2E = 1.4426950408889634
FF_CHUNK = 256
MASKED_GATE = -1e30
NEG_INIT = -1e30


def _cparams(*sem):
    return pltpu.CompilerParams(dimension_semantics=sem, vmem_limit_bytes=VMEM_LIMIT_BYTES)


def _const_spec(shape):
    nd = len(shape)
    return pl.BlockSpec(shape, lambda *_: (0,) * nd, pipeline_mode=pl.Buffered(1))


def _rms_scale(x):
    return lax.rsqrt(jnp.mean(x * x, axis=-1, keepdims=True) + EPS)


def _log_sigmoid(x):
    return jnp.minimum(x, 0.0) - jnp.log1p(jnp.exp(-jnp.abs(x)))


def _sigmoid(x):
    return 1.0 / (1.0 + jnp.exp(-x))


def _dot(a, b):
    return jnp.dot(a, b, preferred_element_type=F32)


def _dot_nt(a, b):
    return lax.dot_general(a, b, (((1,), (1,)), ((), ())), preferred_element_type=F32)


def _dot_tn(a, b):
    return lax.dot_general(a, b, (((0,), (0,)), ((), ())), preferred_element_type=F32)


def _split3(x):
    hi = x.astype(BF16)
    r1 = x - hi.astype(F32)
    mid = r1.astype(BF16)
    lo = (r1 - mid.astype(F32)).astype(BF16)
    return hi, mid, lo


def _dot_by_01(x, m01):
    hi, mid, lo = _split3(x)
    return _dot(hi, m01) + _dot(mid, m01) + _dot(lo, m01)


def _iota(shape, dim):
    return lax.broadcasted_iota(jnp.int32, shape, dim)


def _proj_in_body(x_ref, g_ref, wqk_ref, wv_ref, wog_ref, wg_ref, bg_ref, qs_ref,
                  qk_ref, v_ref, og_ref, gt_ref):
    x = x_ref[...]
    xn = (x * _rms_scale(x) * g_ref[...]).astype(BF16)
    qk_ref[...] = (_dot(xn, wqk_ref[...]) * qs_ref[...]).astype(BF16)
    v_ref[...] = _dot(xn, wv_ref[...]).astype(BF16)
    og_ref[...] = _dot(xn, wog_ref[...]).astype(BF16)
    z = _dot(xn, wg_ref[...]) + bg_ref[...]
    cap = GATE_CAP * jnp.tanh(z / GATE_CAP)
    lane = _iota(cap.shape, 1)
    gt_ref[...] = jnp.where(lane < H_A, cap, _log_sigmoid(cap))


def _proj_in(x, g, wqk, wv, wog, wg, bg, qs, tm):
    n, d = x.shape
    row = lambda i: (i, 0)
    return pl.pallas_call(
        _proj_in_body,
        grid=(pl.cdiv(n, tm),),
        in_specs=[pl.BlockSpec((tm, d), row), _const_spec(g.shape), _const_spec(wqk.shape),
                  _const_spec(wv.shape), _const_spec(wog.shape), _const_spec(wg.shape),
                  _const_spec(bg.shape), _const_spec(qs.shape)],
        out_specs=[pl.BlockSpec((tm, wqk.shape[1]), row), pl.BlockSpec((tm, wv.shape[1]), row),
                   pl.BlockSpec((tm, wog.shape[1]), row), pl.BlockSpec((tm, LANES), row)],
        out_shape=[jax.ShapeDtypeStruct((n, wqk.shape[1]), BF16),
                   jax.ShapeDtypeStruct((n, wv.shape[1]), BF16),
                   jax.ShapeDtypeStruct((n, wog.shape[1]), BF16),
                   jax.ShapeDtypeStruct((n, LANES), F32)],
        compiler_params=_cparams("parallel"),
        name="proj_in",
    )(x, g, wqk, wv, wog, wg, bg, qs)


def _mix_ffn_body(a_ref, h_ref, wo_ref, gf_ref, wgu_ref, wd_ref, gout_ref, o_ref, *, d_ff, final):
    h1 = h_ref[...] + _dot(a_ref[...], wo_ref[...])
    xn = (h1 * _rms_scale(h1) * gf_ref[...]).astype(BF16)
    acc = h1
    for c in range(d_ff // FF_CHUNK):
        lo = c * FF_CHUNK
        gate = _dot(xn, wgu_ref[:, lo:lo + FF_CHUNK])
        up = _dot(xn, wgu_ref[:, d_ff + lo:d_ff + lo + FF_CHUNK])
        act = (gate * _sigmoid(gate) * up).astype(BF16)
        acc = acc + _dot(act, wd_ref[lo:lo + FF_CHUNK, :])
    if final:
        acc = acc * _rms_scale(acc) * gout_ref[...]
    o_ref[...] = acc


def _mix_ffn(a, h, wo, gf, wgu, wd, gout, tm, final, drop_lead=None):
    n, d = h.shape
    d_ff = wd.shape[0]
    consts = [_const_spec(w.shape) for w in (wo, gf, wgu, wd, gout)]
    body = functools.partial(_mix_ffn_body, d_ff=d_ff, final=final)
    name = "mix_ffn_final" if final else "mix_ffn"
    if drop_lead is None:
        row = lambda i: (i, 0)
        return pl.pallas_call(
            body, grid=(pl.cdiv(n, tm),),
            in_specs=[pl.BlockSpec((tm, a.shape[1]), row), pl.BlockSpec((tm, d), row)] + consts,
            out_specs=pl.BlockSpec((tm, d), row),
            out_shape=jax.ShapeDtypeStruct((n, d), F32),
            compiler_params=_cparams("parallel"), name=name,
        )(a, h, wo, gf, wgu, wd, gout)
    batch, seq, lead = drop_lead
    nt = (seq - lead) // tm
    align = 2 * SUBLANES
    assert nt * tm == seq - lead and batch * seq == n and seq % align == 0 and lead % align == 0
    src = lambda b, i: (pl.multiple_of(b * seq + lead + i * tm, align), 0)
    return pl.pallas_call(
        body, grid=(batch, nt),
        in_specs=[pl.BlockSpec((pl.Element(tm), pl.Element(a.shape[1])), src),
                  pl.BlockSpec((pl.Element(tm), pl.Element(d)), src)] + consts,
        out_specs=pl.BlockSpec((tm, d), lambda b, i: (b * nt + i, 0)),
        out_shape=jax.ShapeDtypeStruct((batch * nt * tm, d), F32),
        compiler_params=_cparams("parallel", "parallel"), name=name,
    )(a, h, wo, gf, wgu, wd, gout)


def _head_rmsnorm64(x):
    outs = []
    for j in range(x.shape[1] // LANES):
        blk = x[:, j * LANES:(j + 1) * LANES]
        sq = blk * blk
        lane = _iota(blk.shape, 1)
        s_all = jnp.sum(sq, axis=1, keepdims=True)
        s_lo = jnp.sum(jnp.where(lane < DH_B, sq, 0.0), axis=1, keepdims=True)
        ms = jnp.where(lane < DH_B, s_lo, s_all - s_lo) / DH_B
        outs.append(blk * lax.rsqrt(ms + EPS))
    return jnp.concatenate(outs, axis=1)


def _kvq_compute(h, gkv_ref, wk_ref, wv_ref, wf_ref, bf_ref, kg_ref, gb_ref, wq_ref, wog_ref, qg_ref):
    hr = h * _rms_scale(h)
    xs = (hr * gkv_ref[...]).astype(BF16)
    k = _head_rmsnorm64(_dot(xs, wk_ref[...])) * kg_ref[...]
    v = _dot(xs, wv_ref[...])
    lf = _log_sigmoid(_dot(xs, wf_ref[...]) + bf_ref[...])
    xq = (hr * gb_ref[...]).astype(BF16)
    q = _head_rmsnorm64(_dot(xq, wq_ref[...])) * qg_ref[...]
    og = _dot(xq, wog_ref[...])
    return k, v, lf, q, og


def _kvq_body(h_ref, gkv_ref, wk_ref, wv_ref, wf_ref, bf_ref, kg_ref, gb_ref, wq_ref, wog_ref,
              qg_ref, k_ref, v_ref, lf_ref, kb_ref, vb_ref, qb_ref, og_ref):
    k, v, lf, q, og = _kvq_compute(h_ref[...], gkv_ref, wk_ref, wv_ref, wf_ref, bf_ref, kg_ref,
                                   gb_ref, wq_ref, wog_ref, qg_ref)
    k_ref[...] = k
    v_ref[...] = v
    kb_ref[...] = k.astype(BF16)
    vb_ref[...] = v.astype(BF16)
    lf_ref[...] = lf[:, :H_B]
    qb_ref[...] = q.astype(BF16)
    og_ref[...] = og.astype(BF16)


def _placement():
    h = jnp.arange(H_B)[None, :, None]
    part = jnp.arange(N_SPLIT)[:, None, None]
    col = jnp.arange(H_B * DH_B)[None, None, :]
    lane = jnp.where(h % 2 == 0, DH_B, 0) + part
    return (col == (h // 2) * LANES + lane).astype(BF16)


def _kvq_t_body(h_ref, gkv_ref, wk_ref, wv_ref, wf_ref, bf_ref, kg_ref, gb_ref, wq_ref, wog_ref,
                qg_ref, place_ref, kt_ref, vt_ref, lft_ref, k0_ref, k1_ref, vtb_ref, qt_ref, og_ref,
                carry, *, seq):
    tm = h_ref.shape[1]
    k, v, lf, q, og = _kvq_compute(h_ref[0], gkv_ref, wk_ref, wv_ref, wf_ref, bf_ref, kg_ref,
                                   gb_ref, wq_ref, wog_ref, qg_ref)
    valid = pl.program_id(1) * tm + _iota((tm, 1), 0) < seq
    k = jnp.where(valid, k, 0.0)
    v = jnp.where(valid, v, 0.0)
    lf = jnp.where(valid, lf, 0.0)
    vt = v.T
    kt_ref[0] = k.T
    vt_ref[0] = vt
    vtb_ref[0] = vt.astype(BF16)
    qt_ref[0] = jnp.where(valid, q, 0.0).T.astype(BF16)
    lft_ref[0] = lf.T[:H_B, :]
    og_ref[0] = og.astype(BF16)

    @pl.when(pl.program_id(1) == 0)
    def _():
        carry[...] = jnp.zeros(carry.shape, F32)

    tril = jnp.where(_iota((LANES, LANES), 0) >= _iota((LANES, LANES), 1), 1.0, 0.0).astype(BF16)
    lo_half = (_iota((LANES, k.shape[1]), 1) & (LANES - 1)) < DH_B
    for sub in range(tm // LANES):
        rs = slice(sub * LANES, (sub + 1) * LANES)
        hi, mid, lo = _split3(lf[rs, :H_B])
        c = _dot(tril, hi) + _dot(tril, mid) + _dot(tril, lo) + carry[0:1, 0:H_B]
        carry[0:1, 0:H_B] = c[LANES - 1:LANES, :]
        parts = _split3(c * LOG2E)
        bias = sum(_dot(parts[j], place_ref[j]) for j in range(N_SPLIT))
        k0_ref[0, rs, :] = jnp.where(lo_half, k[rs, :], bias).astype(BF16)
        k1_ref[0, rs, :] = jnp.where(lo_half, bias, k[rs, :]).astype(BF16)


def _kvq_proj_t(h, gkv, wk, wv, wf, bf, kg, gb, wq, wog, qg, batch, seq):
    d = h.shape[1]
    hd = wk.shape[1]
    tm = KVQ_T_TILE
    nt = pl.cdiv(seq, tm)
    tpad = nt * tm
    rows = lambda b, i: (b, i, 0)
    cols = lambda b, i: (b, 0, i)
    place = _placement()
    return pl.pallas_call(
        functools.partial(_kvq_t_body, seq=seq),
        grid=(batch, nt),
        in_specs=[pl.BlockSpec((1, tm, d), rows)] + [_const_spec(w.shape) for w in
                                                      (gkv, wk, wv, wf, bf, kg, gb, wq, wog, qg, place)],
        out_specs=[pl.BlockSpec((1, hd, tm), cols), pl.BlockSpec((1, hd, tm), cols),
                   pl.BlockSpec((1, H_B, tm), cols), pl.BlockSpec((1, tm, hd), rows),
                   pl.BlockSpec((1, tm, hd), rows), pl.BlockSpec((1, hd, tm), cols),
                   pl.BlockSpec((1, hd, tm), cols), pl.BlockSpec((1, tm, wog.shape[1]), rows)],
        out_shape=[jax.ShapeDtypeStruct((batch, hd, seq), F32), jax.ShapeDtypeStruct((batch, hd, seq), F32),
                   jax.ShapeDtypeStruct((batch, H_B, seq), F32), jax.ShapeDtypeStruct((batch, tpad, hd), BF16),
                   jax.ShapeDtypeStruct((batch, tpad, hd), BF16), jax.ShapeDtypeStruct((batch, hd, tpad), BF16),
                   jax.ShapeDtypeStruct((batch, hd, tpad), BF16),
                   jax.ShapeDtypeStruct((batch, seq, wog.shape[1]), BF16)],
        scratch_shapes=[pltpu.VMEM((SUBLANES, LANES), F32)],
        compiler_params=_cparams("parallel", "arbitrary"),
        name="kvq_proj_t",
    )(h.reshape(batch, seq, d), gkv, wk, wv, wf, bf, kg, gb, wq, wog, qg, place)


def _kvq_proj(h, gkv, wk, wv, wf, bf, kg, gb, wq, wog, qg, tm):
    n, d = h.shape
    hd = wk.shape[1]
    row = lambda i: (i, 0)
    wide = pl.BlockSpec((tm, hd), row)
    return pl.pallas_call(
        _kvq_body,
        grid=(pl.cdiv(n, tm),),
        in_specs=[pl.BlockSpec((tm, d), row)] + [_const_spec(w.shape) for w in
                                                 (gkv, wk, wv, wf, bf, kg, gb, wq, wog, qg)],
        out_specs=[wide, wide, pl.BlockSpec((tm, H_B), row), wide, wide, wide,
                   pl.BlockSpec((tm, wog.shape[1]), row)],
        out_shape=[jax.ShapeDtypeStruct((n, hd), F32), jax.ShapeDtypeStruct((n, hd), F32),
                   jax.ShapeDtypeStruct((n, H_B), F32), jax.ShapeDtypeStruct((n, hd), BF16),
                   jax.ShapeDtypeStruct((n, hd), BF16), jax.ShapeDtypeStruct((n, hd), BF16),
                   jax.ShapeDtypeStruct((n, wog.shape[1]), BF16)],
        compiler_params=_cparams("parallel"),
        name="kvq_proj",
    )(h, gkv, wk, wv, wf, bf, kg, gb, wq, wog, qg)


def _col_to_row(col):
    n = col.shape[0]
    eye = _iota((n, n), 0) == _iota((n, n), 1)
    return jnp.sum(jnp.where(eye, col, 0.0), axis=0, keepdims=True)


def _row_to_col(row):
    n = row.shape[1]
    eye = _iota((n, n), 0) == _iota((n, n), 1)
    return jnp.sum(jnp.where(eye, row, 0.0), axis=1, keepdims=True)


def _mlstm_chunk_heads(qk, v, li, lf, cfull, m):
    heads = range(len(qk))
    L = qk[0].shape[0]
    tril = _iota((L, L), 0) >= _iota((L, L), 1)
    lane = _iota((L, LANES), 1)
    ones_col = jnp.where(lane == 0, 1.0, 0.0)

    q_lo = [jnp.where(lane < DK_A, qk[i], 0.0).astype(BF16) for i in heads]
    kq = [pltpu.roll(qk[i], DK_A, 1) for i in heads]
    vaug = [jnp.concatenate([v[i], ones_col], axis=1).astype(BF16) for i in heads]
    qk_t = [_dot_nt(q_lo[i], kq[i].astype(BF16)) for i in heads]
    q_c = [_dot(q_lo[i], cfull[i].astype(BF16)) for i in heads]

    lf_row = [_col_to_row(lf[i]) for i in heads]
    li_row = [_col_to_row(li[i]) for i in heads]
    b = [jnp.sum(jnp.where(tril, lf_row[i], 0.0), axis=1, keepdims=True) for i in heads]
    b_row = [_col_to_row(b[i]) for i in heads]
    dmat = [jnp.where(tril, b[i] - b_row[i] + li_row[i], -jnp.inf) for i in heads]
    dmax = [jnp.max(dmat[i], axis=1, keepdims=True) for i in heads]

    b_end = [b[i][L - 1:L, :] for i in heads]
    g = [b_end[i] - b[i] + li[i] for i in heads]
    m_new = [jnp.maximum(b_end[i] + m[i], jnp.max(g[i], axis=0, keepdims=True)) for i in heads]
    w_c = [jnp.exp(b_end[i] + m[i] - m_new[i]) for i in heads]
    upd = [_dot_tn((jnp.exp(g[i] - m_new[i]) * kq[i]).astype(BF16), vaug[i]) for i in heads]
    keep = _iota(cfull[0].shape, 0) < DK_A
    cfull_new = [jnp.where(keep, w_c[i] * cfull[i] + upd[i], 0.0) for i in heads]

    inter = [b[i] + m[i] for i in heads]
    m_t = [jnp.maximum(inter[i], dmax[i]) for i in heads]
    s = [(qk_t[i] * jnp.exp(dmat[i] - m_t[i])).astype(BF16) for i in heads]
    tot = [jnp.exp(inter[i] - m_t[i]) * q_c[i] + _dot(s[i], vaug[i]) for i in heads]
    h = [tot[i][:, :DV_A] / jnp.maximum(jnp.abs(tot[i][:, DV_A:DV_A + 1]), jnp.exp(-m_t[i])) for i in heads]
    return h, cfull_new, m_new


def _mlstm_head_out(h, og, gain):
    return h * _rms_scale(h) * gain * _sigmoid(og)


def _mlstm_prompt_body(qk_ref, v_ref, og_ref, gt_ref, mhg_ref, hg_ref, c_ref, n_ref, m_ref,
                       cst, mst, *, seq, hb):
    grp = pl.program_id(1)
    ch = MLSTM_CHUNK
    nfull, tail = seq // ch, seq % ch
    cst[...] = jnp.zeros(cst.shape, F32)
    mst[...] = jnp.zeros(mst.shape, F32)

    def chunk(r0, first_valid):
        gt = gt_ref[0, pl.ds(r0, ch), :]
        lane = _iota(gt.shape, 1)
        rowi = _iota((ch, 1), 0)
        li, lf, qk, v = [], [], [], []
        for hh in range(hb):
            head = grp * hb + hh
            li_h = jnp.sum(jnp.where(lane == head, gt, 0.0), axis=1, keepdims=True)
            lf_h = jnp.sum(jnp.where(lane == head + H_A, gt, 0.0), axis=1, keepdims=True)
            if first_valid:
                li_h = jnp.where(rowi >= first_valid, li_h, MASKED_GATE)
                lf_h = jnp.where(rowi >= first_valid, lf_h, 0.0)
            sl = slice(hh * LANES, (hh + 1) * LANES)
            li.append(li_h)
            lf.append(lf_h)
            qk.append(qk_ref[0, pl.ds(r0, ch), sl].astype(F32))
            v.append(v_ref[0, pl.ds(r0, ch), sl].astype(F32))
        hs, cnew, mnew = _mlstm_chunk_heads(qk, v, li, lf, [cst[hh] for hh in range(hb)],
                                            [mst[hh, 0:1, 0:1] for hh in range(hb)])
        outs = []
        for hh in range(hb):
            cst[hh] = cnew[hh]
            mst[hh] = jnp.broadcast_to(mnew[hh], mst.shape[1:])
            og = og_ref[0, pl.ds(r0, ch), hh * LANES:(hh + 1) * LANES].astype(F32)
            outs.append(_mlstm_head_out(hs[hh], og, mhg_ref[hh]))
        return jnp.concatenate(outs, axis=1).astype(BF16)

    def loop_body(j, carry):
        r0 = pl.multiple_of(j * ch, ch)
        hg_ref[0, pl.ds(r0, ch), :] = chunk(r0, 0)
        return carry

    lax.fori_loop(0, nfull, loop_body, 0)
    if tail:
        out = chunk(seq - ch, ch - tail)
        hg_ref[0, seq - tail:seq, :] = out[ch - tail:, :]

    for hh in range(hb):
        cfull = cst[hh]
        c_ref[0, hh] = cfull[:DK_A, :DV_A]
        n_ref[0, hh] = _col_to_row(cfull[:DK_A, DV_A:DV_A + 1])
        m_ref[0, hh] = mst[hh, 0:1, 0:1]


def _mlstm_prompt(qk, v, og, gt, mhg, batch, seq):
    hb = MLSTM_HEADS_PER_STEP
    w = hb * LANES
    qk3, v3, og3 = (a.reshape(batch, seq, a.shape[-1]) for a in (qk, v, og))
    gt3 = gt.reshape(batch, seq, LANES)
    seq_blk = lambda b, g: (b, 0, g)
    hg, c, n, m = pl.pallas_call(
        functools.partial(_mlstm_prompt_body, seq=seq, hb=hb),
        grid=(batch, H_A // hb),
        in_specs=[pl.BlockSpec((1, seq, w), seq_blk), pl.BlockSpec((1, seq, w), seq_blk),
                  pl.BlockSpec((1, seq, w), seq_blk),
                  pl.BlockSpec((1, seq, LANES), lambda b, g: (b, 0, 0)),
                  pl.BlockSpec((hb, 1, DV_A), lambda b, g: (g, 0, 0))],
        out_specs=[pl.BlockSpec((1, seq, w), seq_blk),
                   pl.BlockSpec((1, hb, DK_A, DV_A), lambda b, g: (b, g, 0, 0)),
                   pl.BlockSpec((1, hb, 1, DK_A), lambda b, g: (b, g, 0, 0)),
                   pl.BlockSpec((1, hb, 1, 1), lambda b, g: (b, g, 0, 0))],
        out_shape=[jax.ShapeDtypeStruct((batch, seq, H_A * DV_A), BF16),
                   jax.ShapeDtypeStruct((batch, H_A, DK_A, DV_A), F32),
                   jax.ShapeDtypeStruct((batch, H_A, 1, DK_A), F32),
                   jax.ShapeDtypeStruct((batch, H_A, 1, 1), F32)],
        scratch_shapes=[pltpu.VMEM((hb, LANES, 2 * LANES), F32), pltpu.VMEM((hb, SUBLANES, LANES), F32)],
        compiler_params=_cparams("parallel", "arbitrary"),
        name="mlstm_prompt",
    )(qk3, v3, og3, gt3, mhg.reshape(H_A, 1, DV_A))
    return (hg.reshape(batch * seq, H_A * DV_A), c, n.reshape(batch, H_A, DK_A),
            m.reshape(batch, H_A))


def _mlstm_sample_body(qk_ref, v_ref, og_ref, gt_ref, mhg_ref, c0_ref, n0_ref, m0_ref,
                       hg_ref, c_ref, n_ref, m_ref, *, steps):
    ch = pl.cdiv(steps, SAMPLE_CHUNK_ALIGN) * SAMPLE_CHUNK_ALIGN
    pad = jnp.zeros((ch - steps, LANES), F32)
    padded = (lambda a: jnp.concatenate([a, pad], axis=0)) if ch > steps else (lambda a: a)
    rowi = _iota((ch, 1), 0)
    gt = padded(gt_ref[0])
    lane1 = _iota((1, LANES), 1)
    lane_c = _iota((DK_A, LANES), 1)
    m_out = jnp.zeros((1, LANES), F32)
    li, lf, qk, v, cfull, m0 = [], [], [], [], [], []
    for hh in range(H_A):
        sl = slice(hh * LANES, (hh + 1) * LANES)
        li.append(jnp.where(rowi < steps, gt[:, hh:hh + 1], MASKED_GATE))
        lf.append(jnp.where(rowi < steps, gt[:, H_A + hh:H_A + hh + 1], 0.0))
        qk.append(padded(qk_ref[0, :, sl].astype(F32)))
        v.append(padded(v_ref[0, :, sl].astype(F32)))
        ncol = _row_to_col(n0_ref[0, hh:hh + 1, :])
        top = jnp.concatenate([c0_ref[0, hh], jnp.where(lane_c == 0, ncol, 0.0)], axis=1)
        cfull.append(jnp.concatenate([top, jnp.zeros((LANES - DK_A, 2 * LANES), F32)], axis=0))
        m0.append(m0_ref[0, :, hh:hh + 1])
    hs, cnew, mnew = _mlstm_chunk_heads(qk, v, li, lf, cfull, m0)
    outs = []
    for hh in range(H_A):
        og = og_ref[0, :, hh * LANES:(hh + 1) * LANES].astype(F32)
        outs.append(_mlstm_head_out(hs[hh][:steps, :], og, mhg_ref[hh]))
        c_ref[0, hh] = cnew[hh][:DK_A, :DV_A]
        n_ref[0, hh:hh + 1, :] = _col_to_row(cnew[hh][:DK_A, DV_A:DV_A + 1])
        m_out = jnp.where(lane1 == hh, mnew[hh], m_out)
    hg_ref[0] = jnp.concatenate(outs, axis=1).astype(BF16)
    m_ref[0] = m_out[:, :H_A]


def _mlstm_sample(qk, v, og, gt, mhg, c0, n0, m0, batch, steps):
    wide = H_A * LANES
    blk3 = lambda b: (b, 0, 0)
    hg, c, n, m = pl.pallas_call(
        functools.partial(_mlstm_sample_body, steps=steps),
        grid=(batch,),
        in_specs=[pl.BlockSpec((1, steps, wide), blk3), pl.BlockSpec((1, steps, wide), blk3),
                  pl.BlockSpec((1, steps, wide), blk3), pl.BlockSpec((1, steps, LANES), blk3),
                  _const_spec((H_A, 1, DV_A)),
                  pl.BlockSpec((1, H_A, DK_A, DV_A), lambda b: (b, 0, 0, 0)),
                  pl.BlockSpec((1, H_A, DK_A), blk3), pl.BlockSpec((1, 1, H_A), blk3)],
        out_specs=[pl.BlockSpec((1, steps, wide), blk3),
                   pl.BlockSpec((1, H_A, DK_A, DV_A), lambda b: (b, 0, 0, 0)),
                   pl.BlockSpec((1, H_A, DK_A), blk3), pl.BlockSpec((1, 1, H_A), blk3)],
        out_shape=[jax.ShapeDtypeStruct((batch, steps, wide), BF16),
                   jax.ShapeDtypeStruct((batch, H_A, DK_A, DV_A), F32),
                   jax.ShapeDtypeStruct((batch, H_A, DK_A), F32),
                   jax.ShapeDtypeStruct((batch, 1, H_A), F32)],
        compiler_params=_cparams("parallel"),
        name="mlstm_sample",
    )(qk.reshape(batch, steps, wide), v.reshape(batch, steps, wide), og.reshape(batch, steps, wide),
      gt.reshape(batch, steps, LANES), mhg.reshape(H_A, 1, DV_A), c0, n0, m0.reshape(batch, 1, H_A))
    return hg.reshape(batch * steps, wide), c, n, m.reshape(batch, H_A)


def _rows_to_lanes(x16, staging_ref):
    staging_ref[...] = jnp.zeros(staging_ref.shape, F32)
    staging_ref[:, 0:x16.shape[1]] = x16
    return staging_ref[...].T[0:x16.shape[1], :]


def _attn_prompt_body(qt_ref, k0_ref, k1_ref, vt_ref, og_ref, o_ref, s_scr, p_scr, acc_scr, qa_scr,
                      mask_scr, *, seq):
    tq, tk = ATTN_TQ, ATTN_TK
    tpad = qt_ref.shape[2]
    nfull = seq // tq
    nh = ATTN_HEADS
    heads = range(nh)

    def keys(h, rows):
        pair = slice((h // 2) * LANES, (h // 2 + 1) * LANES)
        return (k0_ref if h % 2 == 0 else k1_ref)[0, rows, pair]

    def augmented_queries(qt):
        out = []
        for pp in range(nh // 2):
            blk = qt[pp * LANES:(pp + 1) * LANES, :]
            row = _iota(blk.shape, 0)
            out.append(jnp.where(row < DH_B, blk, jnp.where(row < DH_B + N_SPLIT, -1.0, 0.0)).astype(BF16))
            out.append(jnp.where(row >= DH_B, blk, jnp.where(row < N_SPLIT, -1.0, 0.0)).astype(BF16))
        return out

    def with_ones(vt):
        return jnp.concatenate([vt, jnp.ones((ROWSUM_ROWS, vt.shape[1]), BF16)], axis=0)

    def tail_tile(q0, width, rows_out):
        qa = augmented_queries(qt_ref[0, :, pl.ds(q0, width)].astype(F32))
        causal = _iota((tpad, width), 0) <= q0 + _iota((tpad, width), 1)
        outs = []
        for hh in heads:
            s = jnp.where(causal, _dot(keys(hh, slice(None)), qa[hh]), -jnp.inf)
            p = jnp.exp2(s - jnp.max(s, axis=0, keepdims=True)).astype(BF16)
            full = _dot(with_ones(vt_ref[0, hh * DH_B:(hh + 1) * DH_B, :]), p)
            outs.append(full[:DH_B, :] / full[DH_B:DH_B + 1, :])
        out = jnp.concatenate(outs, axis=0).T[:rows_out, :]
        gate = _sigmoid(og_ref[0, pl.ds(q0, rows_out), :].astype(F32))
        o_ref[0, pl.ds(q0, rows_out), :] = (out * gate).astype(BF16)

    per = tq // tk
    for d in range(per):
        key = d * tk + _iota((tk, tq), 0)
        mask_scr[d] = jnp.where(key <= _iota((tk, tq), 1), 0.0, -jnp.inf)

    def value_product(hh, k0, slot):
        return _dot(with_ones(vt_ref[0, hh * DH_B:(hh + 1) * DH_B, pl.ds(k0, tk)]), p_scr[slot, hh])

    def pipe_step(ms, kidx, slot, diag_idx, issue_next):
        k0 = pl.multiple_of(kidx * tk, tk)
        if issue_next:
            for hh in heads:
                s_scr[1 - slot, hh] = _dot(keys(hh, pl.ds(k0 + tk, tk)), qa_scr[hh])
        kprev = pl.multiple_of(jnp.maximum(kidx - 1, 0) * tk, tk)
        pvs = [value_product(hh, kprev, 1 - slot) for hh in heads]
        def scores(hh):
            s = s_scr[slot, hh]
            return s if diag_idx is None else s + mask_scr[diag_idx]

        new = [jnp.maximum(ms[hh], jnp.max(scores(hh), axis=0, keepdims=True)) for hh in heads]
        for hh in heads:
            p_scr[slot, hh] = jnp.exp2(scores(hh) - new[hh]).astype(BF16)
        for hh in heads:
            acc_scr[hh] = (acc_scr[hh] + pvs[hh]) * jnp.exp2(ms[hh] - new[hh])
        return tuple(new)

    def q_tile_pipelined(i):
        q0 = pl.multiple_of(i * tq, tq)
        qa = augmented_queries(qt_ref[0, :, pl.ds(q0, tq)].astype(F32))
        for hh in heads:
            qa_scr[hh] = qa[hh]
        p_scr[1] = jnp.zeros(p_scr.shape[1:], BF16)
        acc_scr[...] = jnp.zeros(acc_scr.shape, F32)
        for hh in heads:
            s_scr[0, hh] = _dot(keys(hh, pl.ds(0, tk)), qa_scr[hh])
        ms = tuple(jnp.full((1, tq), NEG_INIT, F32) for _ in heads)

        def group(jj, st):
            for d in range(per):
                st = pipe_step(st, jj * per + d, d % 2, None, True)
            return st

        ms = lax.fori_loop(0, i, group, ms)
        for d in range(per):
            ms = pipe_step(ms, i * per + d, d % 2, d, d < per - 1)
        last = (per - 1) % 2
        klast = pl.multiple_of((i * per + per - 1) * tk, tk)
        outs = []
        for hh in heads:
            full = acc_scr[hh] + value_product(hh, klast, last)
            outs.append(full[:DH_B, :] / full[DH_B:DH_B + 1, :])
        out = jnp.concatenate(outs, axis=0).T
        gate = _sigmoid(og_ref[0, pl.ds(q0, tq), :].astype(F32))
        o_ref[0, pl.ds(q0, tq), :] = (out * gate).astype(BF16)

    def qbody(i, _):
        q_tile_pipelined(i)
        return 0

    lax.fori_loop(0, nfull, qbody, 0)
    if seq > nfull * tq:
        q0 = nfull * tq
        tail_tile(q0, tpad - q0, seq - q0)


def _attn_prompt(qt, k0, k1, vt, og, seq):
    batch, wide, tpad = qt.shape
    tail_w = tpad - (seq // ATTN_TQ) * ATTN_TQ
    assert ATTN_TQ % (2 * ATTN_TK) == 0 and 0 <= tail_w and tail_w % LANES == 0
    rows = lambda b, p: (b, 0, p)
    cols = lambda b, p: (b, p, 0)
    nh = ATTN_HEADS
    w = nh * DH_B
    out = pl.pallas_call(
        functools.partial(_attn_prompt_body, seq=seq),
        grid=(batch, H_B // nh),
        in_specs=[pl.BlockSpec((1, w, tpad), cols), pl.BlockSpec((1, tpad, w), rows),
                  pl.BlockSpec((1, tpad, w), rows), pl.BlockSpec((1, w, tpad), cols),
                  pl.BlockSpec((1, seq, w), rows)],
        out_specs=pl.BlockSpec((1, seq, w), rows),
        out_shape=jax.ShapeDtypeStruct((batch, seq, wide), BF16),
        scratch_shapes=[pltpu.VMEM((2, nh, ATTN_TK, ATTN_TQ), F32),
                        pltpu.VMEM((2, nh, ATTN_TK, ATTN_TQ), BF16),
                        pltpu.VMEM((nh, DH_B + ROWSUM_ROWS, ATTN_TQ), F32),
                        pltpu.VMEM((nh, LANES, ATTN_TQ), BF16),
                        pltpu.VMEM((ATTN_TQ // ATTN_TK, ATTN_TK, ATTN_TQ), F32)],
        compiler_params=_cparams("parallel", "parallel"),
        name="attn_prompt",
    )(qt, k0, k1, vt, og)
    return out.reshape(batch * seq, wide)


def _attn_sample_body(pt_ref, q_ref, kn_ref, vn_ref, lfn_ref, og_ref, *rest, steps, npages):
    k_refs, v_refs, lf_refs = rest[:npages], rest[npages:2 * npages], rest[2 * npages:3 * npages]
    o_ref, stage = rest[3 * npages], rest[3 * npages + 1]
    rows = H_B * steps
    wide = H_B * DH_B
    u = _iota((LANES, LANES), 0)
    s_ = _iota((LANES, LANES), 1)

    q = q_ref[0].astype(F32)
    qrep = jnp.concatenate([jnp.broadcast_to(q[t:t + 1, :], (H_B, wide)) for t in range(steps)], axis=0)
    diag = _iota((rows, wide), 0) % H_B == _iota((rows, wide), 1) // DH_B
    qbd = jnp.where(diag, qrep, 0.0).astype(BF16)

    lf_all = jnp.concatenate([r[0] for r in lf_refs], axis=0)
    later_and_ones = jnp.concatenate([jnp.where(u > s_, 1.0, 0.0), jnp.ones((LANES, LANES), F32)],
                                     axis=1).astype(BF16)
    wt = _dot_by_01(lf_all, later_and_ones)
    pr = _iota((npages * H_B, npages * H_B), 0)
    pc = _iota((npages * H_B, npages * H_B), 1)
    later_pages = jnp.where((pc % H_B == pr % H_B) & (pc // H_B > pr // H_B), 1.0, 0.0).astype(BF16)
    hi, mid, lo = _split3(wt[:, LANES:])
    rsum = wt[:, :LANES] + _dot(later_pages, hi) + _dot(later_pages, mid) + _dot(later_pages, lo)
    bias_past = jnp.concatenate(
        [jnp.concatenate([rsum[r * H_B:(r + 1) * H_B, :]] * steps, axis=0) for r in range(npages)], axis=1)

    kcat = jnp.concatenate([r[0].astype(BF16) for r in k_refs], axis=1)
    s_past = _dot(qbd, kcat) + bias_past

    zpad = jnp.zeros((LANES - steps, wide), F32)
    kn = jnp.concatenate([kn_ref[0].astype(F32), zpad], axis=0).astype(BF16)
    vn = jnp.concatenate([vn_ref[0].astype(F32), zpad], axis=0).astype(BF16)
    lfn = jnp.concatenate([lfn_ref[0], jnp.zeros((LANES - steps, H_B), F32)], axis=0)
    incl = jnp.where(u <= s_, 1.0, 0.0).astype(BF16)
    cnew = _dot_by_01(_rows_to_lanes(lfn, stage), incl)
    key = _iota((rows, LANES), 1)
    qry = _iota((rows, LANES), 0) // H_B
    bias_new = jnp.where(key <= qry, -jnp.concatenate([cnew] * steps, axis=0), -jnp.inf)
    s_new = _dot_nt(qbd, kn) + bias_new

    m = jnp.maximum(jnp.max(s_past, axis=1, keepdims=True), jnp.max(s_new, axis=1, keepdims=True))
    p_past = jnp.exp(s_past - m)
    p_new = jnp.exp(s_new - m)
    l = jnp.sum(p_past, axis=1, keepdims=True) + jnp.sum(p_new, axis=1, keepdims=True)
    vcat = jnp.concatenate([r[0].astype(BF16) for r in v_refs], axis=1)
    acc = _dot_nt(p_past.astype(BF16), vcat) + _dot(p_new.astype(BF16), vn)
    full = jnp.where(diag, acc / l, 0.0)
    out = jnp.concatenate([jnp.sum(full[t * H_B:(t + 1) * H_B, :], axis=0, keepdims=True)
                           for t in range(steps)], axis=0)
    o_ref[0] = (out * _sigmoid(og_ref[0].astype(F32))).astype(BF16)


def _attn_sample(q, kn, vn, lfn, og, cache_k, cache_v, cache_logf, page_table, batch, steps):
    wide = H_B * DH_B
    n_phys, page = cache_k.shape[0], cache_k.shape[1]
    npages = page_table.shape[1]
    assert page == LANES
    ck = jnp.transpose(cache_k, (0, 2, 3, 1)).reshape(n_phys, wide, page)
    cv = jnp.transpose(cache_v, (0, 2, 3, 1)).reshape(n_phys, wide, page)
    clf = jnp.transpose(cache_logf, (0, 2, 1))
    tok = lambda b, pt: (b, 0, 0)
    page_of = lambda r: (lambda b, pt: (pt[b, r], 0, 0))
    out = pl.pallas_call(
        functools.partial(_attn_sample_body, steps=steps, npages=npages),
        grid_spec=pltpu.PrefetchScalarGridSpec(
            num_scalar_prefetch=1,
            grid=(batch,),
            in_specs=[pl.BlockSpec((1, steps, wide), tok), pl.BlockSpec((1, steps, wide), tok),
                      pl.BlockSpec((1, steps, wide), tok), pl.BlockSpec((1, steps, H_B), tok),
                      pl.BlockSpec((1, steps, wide), tok)]
            + [pl.BlockSpec((1, wide, page), page_of(r)) for r in range(npages)]
            + [pl.BlockSpec((1, wide, page), page_of(r)) for r in range(npages)]
            + [pl.BlockSpec((1, H_B, page), page_of(r)) for r in range(npages)],
            out_specs=pl.BlockSpec((1, steps, wide), tok),
            scratch_shapes=[pltpu.VMEM((LANES, LANES), F32)]),
        out_shape=jax.ShapeDtypeStruct((batch, steps, wide), BF16),
        compiler_params=_cparams("parallel"),
        name="attn_sample",
    )(page_table, q.reshape(batch, steps, wide), kn.reshape(batch, steps, wide),
      vn.reshape(batch, steps, wide), lfn.reshape(batch, steps, H_B), og.reshape(batch, steps, wide),
      *([ck] * npages), *([cv] * npages), *([clf] * npages))
    return out.reshape(batch * steps, wide)


def _prep_params(norm_a, w_in_a, b_ig_a, b_fg_a, mh_norm_a, w_out_a, norm_kv, w_kvf, b_fg_b,
                 k_norm_b, norm_b, w_qo_b, q_norm_b, w_out_b, norm_ffn, w_gate_up, w_down,
                 norm_final):
    d = w_in_a.shape[1]
    hk, hv, hd = H_A * DK_A, H_A * DV_A, H_B * DH_B
    w_in = w_in_a[0]
    wq = w_in[:, :hk].reshape(d, H_A, DK_A)
    wk = w_in[:, hk:2 * hk].reshape(d, H_A, DK_A)
    row = lambda a: a.reshape(1, -1).astype(F32)
    pad_cols = lambda a: jnp.pad(a, ((0, 0), (0, LANES - a.shape[1])))
    lane = jnp.arange(H_A * LANES) % LANES
    return dict(
        norm_a=row(norm_a[0]),
        wqk=jnp.concatenate([wq, wk], axis=2).reshape(d, H_A * LANES).astype(BF16),
        wv=w_in[:, 2 * hk:2 * hk + hv].astype(BF16),
        wog=w_in[:, 2 * hk + hv:2 * hk + 2 * hv].astype(BF16),
        wg=pad_cols(w_in[:, 2 * hk + 2 * hv:]).astype(BF16),
        bg=pad_cols(jnp.concatenate([b_ig_a[0], b_fg_a[0]]).reshape(1, -1).astype(F32)),
        qs=jnp.where(lane < DK_A, DK_A ** -0.5, 1.0).reshape(1, -1).astype(F32),
        mhg=mh_norm_a[0].astype(F32),
        wo_a=w_out_a[0].astype(BF16),
        gkv=row(norm_kv),
        wk=w_kvf[:, :hd].astype(BF16),
        wvs=w_kvf[:, hd:2 * hd].astype(BF16),
        wf=pad_cols(w_kvf[:, 2 * hd:]).astype(BF16),
        bf=pad_cols(b_fg_b.reshape(1, -1).astype(F32)),
        kg=row(jnp.tile(k_norm_b, H_B)),
        gb=row(norm_b[0]),
        wq=w_qo_b[0][:, :hd].astype(BF16),
        wog_b=w_qo_b[0][:, hd:].astype(BF16),
        qg=row(jnp.tile(q_norm_b[0], H_B)) * DH_B ** -0.5,
        wo_b=w_out_b[0].astype(BF16),
        gf=[row(norm_ffn[l]) for l in range(2)],
        wgu=[w_gate_up[l].astype(BF16) for l in range(2)],
        wd=[w_down[l].astype(BF16) for l in range(2)],
        gfin=row(norm_final),
    )


def _layer0(h, p, tm, mlstm):
    qk, v, og, gt = _proj_in(h, p["norm_a"], p["wqk"], p["wv"], p["wog"], p["wg"], p["bg"], p["qs"], tm)
    hg, c, n, m = mlstm(qk, v, og, gt)
    h2 = _mix_ffn(hg, h, p["wo_a"], p["gf"][0], p["wgu"][0], p["wd"][0], p["gfin"], tm, False)
    return h2, c, n, m


def _shared_and_q(h2, p, tm):
    return _kvq_proj(h2, p["gkv"], p["wk"], p["wvs"], p["wf"], p["bf"], p["kg"], p["gb"],
                     p["wq"], p["wog_b"], p["qg"], tm)


def _layer1_tail(o, h2, p, tm, drop_lead=None):
    return _mix_ffn(o, h2, p["wo_b"], p["gf"][1], p["wgu"][1], p["wd"][1], p["gfin"], tm, True, drop_lead)


def kernel(x_prompt, x_sample, state_C, state_n, state_m, cache_k, cache_v, cache_logf, page_table,
           meta_tokens, norm_a, w_in_a, b_ig_a, b_fg_a, mh_norm_a, w_out_a, norm_kv, w_kvf, b_fg_b,
           k_norm_b, norm_b, w_qo_b, q_norm_b, w_out_b, norm_ffn, w_gate_up, w_down, norm_final):
    assert w_in_a.shape[0] == 1 and w_qo_b.shape[0] == 1 and norm_ffn.shape[0] == 2
    p = _prep_params(norm_a, w_in_a, b_ig_a, b_fg_a, mh_norm_a, w_out_a, norm_kv, w_kvf, b_fg_b,
                     k_norm_b, norm_b, w_qo_b, q_norm_b, w_out_b, norm_ffn, w_gate_up, w_down,
                     norm_final)
    bp, sp, d = x_prompt.shape
    bs, ss, _ = x_sample.shape
    tp = sp + N_META
    hd = H_B * DH_B
    tm = 512

    meta = jnp.broadcast_to(meta_tokens[None].astype(F32), (bp, N_META, d))
    h0 = jnp.concatenate([meta, x_prompt], axis=1).reshape(bp * tp, d)
    h2, p_c, p_n, p_m = _layer0(h0, p, tm, functools.partial(_mlstm_prompt, mhg=p["mhg"], batch=bp, seq=tp))
    kt, vt, lft, k0, k1, vtb, qt, og = _kvq_proj_t(
        h2, p["gkv"], p["wk"], p["wvs"], p["wf"], p["bf"], p["kg"], p["gb"], p["wq"], p["wog_b"],
        p["qg"] * LOG2E, bp, tp)
    o = _attn_prompt(qt, k0, k1, vtb, og, tp)
    y_prompt = _layer1_tail(o, h2, p, tm, drop_lead=(bp, tp, N_META)).reshape(bp, sp, d)
    p_k = jnp.transpose(kt.reshape(bp, H_B, DH_B, tp), (0, 3, 1, 2))
    p_v = jnp.transpose(vt.reshape(bp, H_B, DH_B, tp), (0, 3, 1, 2))
    p_lf = jnp.transpose(lft, (0, 2, 1))

    hs0 = x_sample.reshape(bs * ss, d)
    hs2, s_c, s_n, s_m = _layer0(
        hs0, p, tm, functools.partial(_mlstm_sample, mhg=p["mhg"], c0=state_C[0], n0=state_n[0],
                                      m0=state_m[0], batch=bs, steps=ss))
    ks, vs, lfs, kbs, vbs, qbs, ogs = _shared_and_q(hs2, p, tm)
    os_ = _attn_sample(qbs, kbs, vbs, lfs, ogs, cache_k, cache_v, cache_logf, page_table, bs, ss)
    y_sample = _layer1_tail(os_, hs2, p, tm).reshape(bs, ss, d)

    return (y_prompt, y_sample, p_c[None], p_n[None], p_m[None], p_k, p_v, p_lf,
            s_c[None], s_n[None], s_m[None], ks.reshape(bs, ss, H_B, DH_B),
            vs.reshape(bs, ss, H_B, DH_B), lfs.reshape(bs, ss, H_B))
```

```python
import functools

import jax
import jax.numpy as jnp
from jax import lax
from jax.experimental import pallas as pl
from jax.experimental.pallas import tpu as pltpu

F32 = jnp.float32
BF16 = jnp.bfloat16

N_META = 16
H_A = 8
DK_A = 64
DV_A = 128
GATE_CAP = 15.0
H_B = 16
DH_B = 64
EPS = 1e-6

LANES = 128
SUBLANES = 8
VMEM_LIMIT_BYTES = 56 * 1024 * 1024

MLSTM_CHUNK = 128
MLSTM_HEADS_PER_STEP = 4
MLSTM_SAMPLE_SEQS = 4
SAMPLE_CHUNK_ALIGN = 8
ATTN_TQ = 512
ATTN_TK = 256
ATTN_HEADS = 4
KVQ_T_TILE = 384
N_SPLIT = 3
ROWSUM_ROWS = 16
LOG2E = 1.4426950408889634
FF_CHUNK = 256
MASKED_GATE = -1e30
NEG_INIT = -1e30


def _cparams(*sem):
    return pltpu.CompilerParams(dimension_semantics=sem, vmem_limit_bytes=VMEM_LIMIT_BYTES)


def _const_spec(shape):
    nd = len(shape)
    return pl.BlockSpec(shape, lambda *_: (0,) * nd, pipeline_mode=pl.Buffered(1))


def _rms_scale(x):
    return lax.rsqrt(jnp.mean(x * x, axis=-1, keepdims=True) + EPS)


def _log_sigmoid(x):
    return jnp.minimum(x, 0.0) - jnp.log1p(jnp.exp(-jnp.abs(x)))


def _sigmoid(x):
    return 1.0 / (1.0 + jnp.exp(-x))


def _dot(a, b):
    return jnp.dot(a, b, preferred_element_type=F32)


def _dot_nt(a, b):
    return lax.dot_general(a, b, (((1,), (1,)), ((), ())), preferred_element_type=F32)


def _dot_tn(a, b):
    return lax.dot_general(a, b, (((0,), (0,)), ((), ())), preferred_element_type=F32)


def _split3(x):
    hi = x.astype(BF16)
    r1 = x - hi.astype(F32)
    mid = r1.astype(BF16)
    lo = (r1 - mid.astype(F32)).astype(BF16)
    return hi, mid, lo


def _dot_by_01(x, m01):
    hi, mid, lo = _split3(x)
    return _dot(hi, m01) + _dot(mid, m01) + _dot(lo, m01)


def _iota(shape, dim):
    return lax.broadcasted_iota(jnp.int32, shape, dim)


def _proj_in_body(x_ref, g_ref, wqk_ref, wv_ref, wog_ref, wg_ref, bg_ref, qs_ref,
                  qk_ref, v_ref, og_ref, gt_ref):
    x = x_ref[...]
    xn = (x * _rms_scale(x) * g_ref[...]).astype(BF16)
    qk_ref[...] = (_dot(xn, wqk_ref[...]) * qs_ref[...]).astype(BF16)
    v_ref[...] = _dot(xn, wv_ref[...]).astype(BF16)
    og_ref[...] = _dot(xn, wog_ref[...]).astype(BF16)
    z = _dot(xn, wg_ref[...]) + bg_ref[...]
    cap = GATE_CAP * jnp.tanh(z / GATE_CAP)
    lane = _iota(cap.shape, 1)
    gt_ref[...] = jnp.where(lane < H_A, cap, _log_sigmoid(cap))


def _proj_in(x, g, wqk, wv, wog, wg, bg, qs, tm):
    n, d = x.shape
    row = lambda i: (i, 0)
    return pl.pallas_call(
        _proj_in_body,
        grid=(pl.cdiv(n, tm),),
        in_specs=[pl.BlockSpec((tm, d), row), _const_spec(g.shape), _const_spec(wqk.shape),
                  _const_spec(wv.shape), _const_spec(wog.shape), _const_spec(wg.shape),
                  _const_spec(bg.shape), _const_spec(qs.shape)],
        out_specs=[pl.BlockSpec((tm, wqk.shape[1]), row), pl.BlockSpec((tm, wv.shape[1]), row),
                   pl.BlockSpec((tm, wog.shape[1]), row), pl.BlockSpec((tm, LANES), row)],
        out_shape=[jax.ShapeDtypeStruct((n, wqk.shape[1]), BF16),
                   jax.ShapeDtypeStruct((n, wv.shape[1]), BF16),
                   jax.ShapeDtypeStruct((n, wog.shape[1]), BF16),
                   jax.ShapeDtypeStruct((n, LANES), F32)],
        compiler_params=_cparams("parallel"),
        name="proj_in",
    )(x, g, wqk, wv, wog, wg, bg, qs)


def _mix_ffn_body(a_ref, h_ref, wo_ref, gf_ref, wgu_ref, wd_ref, gout_ref, o_ref, *, d_ff, final):
    h1 = h_ref[...] + _dot(a_ref[...], wo_ref[...])
    xn = (h1 * _rms_scale(h1) * gf_ref[...]).astype(BF16)
    acc = h1
    for c in range(d_ff // FF_CHUNK):
        lo = c * FF_CHUNK
        gate = _dot(xn, wgu_ref[:, lo:lo + FF_CHUNK])
        up = _dot(xn, wgu_ref[:, d_ff + lo:d_ff + lo + FF_CHUNK])
        act = (gate * _sigmoid(gate) * up).astype(BF16)
        acc = acc + _dot(act, wd_ref[lo:lo + FF_CHUNK, :])
    if final:
        acc = acc * _rms_scale(acc) * gout_ref[...]
    o_ref[...] = acc


def _mix_ffn(a, h, wo, gf, wgu, wd, gout, tm, final, drop_lead=None):
    n, d = h.shape
    d_ff = wd.shape[0]
    consts = [_const_spec(w.shape) for w in (wo, gf, wgu, wd, gout)]
    body = functools.partial(_mix_ffn_body, d_ff=d_ff, final=final)
    name = "mix_ffn_final" if final else "mix_ffn"
    if drop_lead is None:
        row = lambda i: (i, 0)
        return pl.pallas_call(
            body, grid=(pl.cdiv(n, tm),),
            in_specs=[pl.BlockSpec((tm, a.shape[1]), row), pl.BlockSpec((tm, d), row)] + consts,
            out_specs=pl.BlockSpec((tm, d), row),
            out_shape=jax.ShapeDtypeStruct((n, d), F32),
            compiler_params=_cparams("parallel"), name=name,
        )(a, h, wo, gf, wgu, wd, gout)
    batch, seq, lead = drop_lead
    nt = (seq - lead) // tm
    align = 2 * SUBLANES
    assert nt * tm == seq - lead and batch * seq == n and seq % align == 0 and lead % align == 0
    src = lambda b, i: (pl.multiple_of(b * seq + lead + i * tm, align), 0)
    return pl.pallas_call(
        body, grid=(batch, nt),
        in_specs=[pl.BlockSpec((pl.Element(tm), pl.Element(a.shape[1])), src),
                  pl.BlockSpec((pl.Element(tm), pl.Element(d)), src)] + consts,
        out_specs=pl.BlockSpec((tm, d), lambda b, i: (b * nt + i, 0)),
        out_shape=jax.ShapeDtypeStruct((batch * nt * tm, d), F32),
        compiler_params=_cparams("parallel", "parallel"), name=name,
    )(a, h, wo, gf, wgu, wd, gout)


def _head_rmsnorm64(x):
    outs = []
    for j in range(x.shape[1] // LANES):
        blk = x[:, j * LANES:(j + 1) * LANES]
        sq = blk * blk
        lane = _iota(blk.shape, 1)
        s_all = jnp.sum(sq, axis=1, keepdims=True)
        s_lo = jnp.sum(jnp.where(lane < DH_B, sq, 0.0), axis=1, keepdims=True)
        ms = jnp.where(lane < DH_B, s_lo, s_all - s_lo) / DH_B
        outs.append(blk * lax.rsqrt(ms + EPS))
    return jnp.concatenate(outs, axis=1)


def _kvq_compute(h, gkv_ref, wk_ref, wv_ref, wf_ref, bf_ref, kg_ref, gb_ref, wq_ref, wog_ref, qg_ref):
    hr = h * _rms_scale(h)
    xs = (hr * gkv_ref[...]).astype(BF16)
    k = _head_rmsnorm64(_dot(xs, wk_ref[...])) * kg_ref[...]
    v = _dot(xs, wv_ref[...])
    lf = _log_sigmoid(_dot(xs, wf_ref[...]) + bf_ref[...])
    xq = (hr * gb_ref[...]).astype(BF16)
    q = _head_rmsnorm64(_dot(xq, wq_ref[...])) * qg_ref[...]
    og = _dot(xq, wog_ref[...])
    return k, v, lf, q, og


def _kvq_body(h_ref, gkv_ref, wk_ref, wv_ref, wf_ref, bf_ref, kg_ref, gb_ref, wq_ref, wog_ref,
              qg_ref, k_ref, v_ref, lf_ref, kb_ref, vb_ref, qb_ref, og_ref):
    k, v, lf, q, og = _kvq_compute(h_ref[...], gkv_ref, wk_ref, wv_ref, wf_ref, bf_ref, kg_ref,
                                   gb_ref, wq_ref, wog_ref, qg_ref)
    k_ref[...] = k
    v_ref[...] = v
    kb_ref[...] = k.astype(BF16)
    vb_ref[...] = v.astype(BF16)
    lf_ref[...] = lf[:, :H_B]
    qb_ref[...] = q.astype(BF16)
    og_ref[...] = og.astype(BF16)


def _placement():
    row = jnp.arange(LANES)[:, None]
    h, part = row % H_B, row // H_B
    col = jnp.arange(H_B * DH_B)[None, :]
    lane = jnp.where(h % 2 == 0, DH_B, 0) + part
    return ((col == (h // 2) * LANES + lane) & (part < N_SPLIT)).astype(BF16)


def _pack3(x):
    hi, mid, lo = (p.astype(F32) for p in _split3(x))
    return (hi + pltpu.roll(mid, H_B, 1) + pltpu.roll(lo, 2 * H_B, 1)).astype(BF16)


def _unpack3_sum(y):
    s = y + pltpu.roll(y, LANES - H_B, 1) + pltpu.roll(y, LANES - 2 * H_B, 1)
    return jnp.where(_iota(y.shape, 1) < H_B, s, 0.0)


def _kvq_t_body(h_ref, gkv_ref, wk_ref, wv_ref, wf_ref, bf_ref, kg_ref, gb_ref, wq_ref, wog_ref,
                qg_ref, place_ref, kt_ref, vt_ref, lft_ref, k0_ref, k1_ref, vtb_ref, qt_ref, og_ref,
                carry, *, seq):
    tm = h_ref.shape[1]
    valid = pl.program_id(1) * tm + _iota((tm, 1), 0) < seq
    h = h_ref[0]
    hr = h * _rms_scale(h)
    xs = (hr * gkv_ref[...]).astype(BF16)
    k = jnp.where(valid, _head_rmsnorm64(_dot(xs, wk_ref[...])) * kg_ref[...], 0.0)
    lf = jnp.where(valid, _log_sigmoid(_dot(xs, wf_ref[...]) + bf_ref[...]), 0.0)

    @pl.when(pl.program_id(1) == 0)
    def _():
        carry[...] = jnp.zeros(carry.shape, F32)

    tril = jnp.where(_iota((LANES, LANES), 0) >= _iota((LANES, LANES), 1), 1.0, 0.0).astype(BF16)
    lo_half = (_iota((LANES, k.shape[1]), 1) & (LANES - 1)) < DH_B
    lf_heads = jnp.where(_iota(lf.shape, 1) < H_B, lf, 0.0)
    for sub in range(tm // LANES):
        rs = slice(sub * LANES, (sub + 1) * LANES)
        c = _unpack3_sum(_dot(tril, _pack3(lf_heads[rs, :]))) + carry[0:1, :]
        carry[0:1, :] = c[LANES - 1:LANES, :]
        bias = _dot(_pack3(c * LOG2E), place_ref[...])
        k0_ref[0, rs, :] = jnp.where(lo_half, k[rs, :], bias).astype(BF16)
        k1_ref[0, rs, :] = jnp.where(lo_half, bias, k[rs, :]).astype(BF16)

    kt_ref[0] = k.T
    lft_ref[0] = lf.T[:H_B, :]
    vt = jnp.where(valid, _dot(xs, wv_ref[...]), 0.0).T
    vt_ref[0] = vt
    vtb_ref[0] = vt.astype(BF16)
    xq = (hr * gb_ref[...]).astype(BF16)
    q = _head_rmsnorm64(_dot(xq, wq_ref[...])) * qg_ref[...]
    qt_ref[0] = jnp.where(valid, q, 0.0).T.astype(BF16)
    og_ref[0] = _dot(xq, wog_ref[...]).astype(BF16)


def _kvq_proj_t(h, gkv, wk, wv, wf, bf, kg, gb, wq, wog, qg, batch, seq):
    d = h.shape[1]
    hd = wk.shape[1]
    tm = KVQ_T_TILE
    nt = pl.cdiv(seq, tm)
    tpad = nt * tm
    rows = lambda b, i: (b, i, 0)
    cols = lambda b, i: (b, 0, i)
    place = _placement()
    return pl.pallas_call(
        functools.partial(_kvq_t_body, seq=seq),
        grid=(batch, nt),
        in_specs=[pl.BlockSpec((1, tm, d), rows)] + [_const_spec(w.shape) for w in
                                                      (gkv, wk, wv, wf, bf, kg, gb, wq, wog, qg, place)],
        out_specs=[pl.BlockSpec((1, hd, tm), cols), pl.BlockSpec((1, hd, tm), cols),
                   pl.BlockSpec((1, H_B, tm), cols), pl.BlockSpec((1, tm, hd), rows),
                   pl.BlockSpec((1, tm, hd), rows), pl.BlockSpec((1, hd, tm), cols),
                   pl.BlockSpec((1, hd, tm), cols), pl.BlockSpec((1, tm, wog.shape[1]), rows)],
        out_shape=[jax.ShapeDtypeStruct((batch, hd, seq), F32), jax.ShapeDtypeStruct((batch, hd, seq), F32),
                   jax.ShapeDtypeStruct((batch, H_B, seq), F32), jax.ShapeDtypeStruct((batch, tpad, hd), BF16),
                   jax.ShapeDtypeStruct((batch, tpad, hd), BF16), jax.ShapeDtypeStruct((batch, hd, tpad), BF16),
                   jax.ShapeDtypeStruct((batch, hd, tpad), BF16),
                   jax.ShapeDtypeStruct((batch, seq, wog.shape[1]), BF16)],
        scratch_shapes=[pltpu.VMEM((SUBLANES, LANES), F32)],
        compiler_params=_cparams("parallel", "arbitrary"),
        name="kvq_proj_t",
    )(h.reshape(batch, seq, d), gkv, wk, wv, wf, bf, kg, gb, wq, wog, qg, place)


def _kvq_proj(h, gkv, wk, wv, wf, bf, kg, gb, wq, wog, qg, tm):
    n, d = h.shape
    hd = wk.shape[1]
    row = lambda i: (i, 0)
    wide = pl.BlockSpec((tm, hd), row)
    return pl.pallas_call(
        _kvq_body,
        grid=(pl.cdiv(n, tm),),
        in_specs=[pl.BlockSpec((tm, d), row)] + [_const_spec(w.shape) for w in
                                                 (gkv, wk, wv, wf, bf, kg, gb, wq, wog, qg)],
        out_specs=[wide, wide, pl.BlockSpec((tm, H_B), row), wide, wide, wide,
                   pl.BlockSpec((tm, wog.shape[1]), row)],
        out_shape=[jax.ShapeDtypeStruct((n, hd), F32), jax.ShapeDtypeStruct((n, hd), F32),
                   jax.ShapeDtypeStruct((n, H_B), F32), jax.ShapeDtypeStruct((n, hd), BF16),
                   jax.ShapeDtypeStruct((n, hd), BF16), jax.ShapeDtypeStruct((n, hd), BF16),
                   jax.ShapeDtypeStruct((n, wog.shape[1]), BF16)],
        compiler_params=_cparams("parallel"),
        name="kvq_proj",
    )(h, gkv, wk, wv, wf, bf, kg, gb, wq, wog, qg)


def _col_to_row(col):
    n = col.shape[0]
    eye = _iota((n, n), 0) == _iota((n, n), 1)
    return jnp.sum(jnp.where(eye, col, 0.0), axis=0, keepdims=True)


def _row_to_col(row):
    n = row.shape[1]
    eye = _iota((n, n), 0) == _iota((n, n), 1)
    return jnp.sum(jnp.where(eye, row, 0.0), axis=1, keepdims=True)


def _mlstm_chunk_heads(qk, v, li, lf, cfull, m):
    heads = range(len(qk))
    L = qk[0].shape[0]
    tril = _iota((L, L), 0) >= _iota((L, L), 1)
    lane = _iota((L, LANES), 1)
    ones_col = jnp.where(lane == 0, 1.0, 0.0)

    q_lo = [jnp.where(lane < DK_A, qk[i], 0.0).astype(BF16) for i in heads]
    kq = [pltpu.roll(qk[i], DK_A, 1) for i in heads]
    vaug = [jnp.concatenate([v[i], ones_col], axis=1).astype(BF16) for i in heads]
    qk_t = [_dot_nt(q_lo[i], kq[i].astype(BF16)) for i in heads]
    q_c = [_dot(q_lo[i], cfull[i].astype(BF16)) for i in heads]

    lf_row = [_col_to_row(lf[i]) for i in heads]
    li_row = [_col_to_row(li[i]) for i in heads]
    b = [jnp.sum(jnp.where(tril, lf_row[i], 0.0), axis=1, keepdims=True) for i in heads]
    b_row = [_col_to_row(b[i]) for i in heads]
    dmat = [jnp.where(tril, b[i] - b_row[i] + li_row[i], -jnp.inf) for i in heads]
    dmax = [jnp.max(dmat[i], axis=1, keepdims=True) for i in heads]

    b_end = [b[i][L - 1:L, :] for i in heads]
    g = [b_end[i] - b[i] + li[i] for i in heads]
    m_new = [jnp.maximum(b_end[i] + m[i], jnp.max(g[i], axis=0, keepdims=True)) for i in heads]
    w_c = [jnp.exp(b_end[i] + m[i] - m_new[i]) for i in heads]
    upd = [_dot_tn((jnp.exp(g[i] - m_new[i]) * kq[i]).astype(BF16), vaug[i]) for i in heads]
    keep = _iota(cfull[0].shape, 0) < DK_A
    cfull_new = [jnp.where(keep, w_c[i] * cfull[i] + upd[i], 0.0) for i in heads]

    inter = [b[i] + m[i] for i in heads]
    m_t = [jnp.maximum(inter[i], dmax[i]) for i in heads]
    s = [(qk_t[i] * jnp.exp(dmat[i] - m_t[i])).astype(BF16) for i in heads]
    tot = [jnp.exp(inter[i] - m_t[i]) * q_c[i] + _dot(s[i], vaug[i]) for i in heads]
    h = [tot[i][:, :DV_A] / jnp.maximum(jnp.abs(tot[i][:, DV_A:DV_A + 1]), jnp.exp(-m_t[i])) for i in heads]
    return h, cfull_new, m_new


def _mlstm_head_out(h, og, gain):
    return h * _rms_scale(h) * gain * _sigmoid(og)


def _mlstm_prompt_body(qk_ref, v_ref, og_ref, gt_ref, mhg_ref, hg_ref, c_ref, n_ref, m_ref,
                       cst, mst, *, seq, hb):
    grp = pl.program_id(1)
    ch = MLSTM_CHUNK
    nfull, tail = seq // ch, seq % ch
    cst[...] = jnp.zeros(cst.shape, F32)
    mst[...] = jnp.zeros(mst.shape, F32)

    def chunk(r0, first_valid):
        gt = gt_ref[0, pl.ds(r0, ch), :]
        lane = _iota(gt.shape, 1)
        rowi = _iota((ch, 1), 0)
        li, lf, qk, v = [], [], [], []
        for hh in range(hb):
            head = grp * hb + hh
            li_h = jnp.sum(jnp.where(lane == head, gt, 0.0), axis=1, keepdims=True)
            lf_h = jnp.sum(jnp.where(lane == head + H_A, gt, 0.0), axis=1, keepdims=True)
            if first_valid:
                li_h = jnp.where(rowi >= first_valid, li_h, MASKED_GATE)
                lf_h = jnp.where(rowi >= first_valid, lf_h, 0.0)
            sl = slice(hh * LANES, (hh + 1) * LANES)
            li.append(li_h)
            lf.append(lf_h)
            qk.append(qk_ref[0, pl.ds(r0, ch), sl].astype(F32))
            v.append(v_ref[0, pl.ds(r0, ch), sl].astype(F32))
        hs, cnew, mnew = _mlstm_chunk_heads(qk, v, li, lf, [cst[hh] for hh in range(hb)],
                                            [mst[hh, 0:1, 0:1] for hh in range(hb)])
        outs = []
        for hh in range(hb):
            cst[hh] = cnew[hh]
            mst[hh] = jnp.broadcast_to(mnew[hh], mst.shape[1:])
            og = og_ref[0, pl.ds(r0, ch), hh * LANES:(hh + 1) * LANES].astype(F32)
            outs.append(_mlstm_head_out(hs[hh], og, mhg_ref[hh]))
        return jnp.concatenate(outs, axis=1).astype(BF16)

    def loop_body(j, carry):
        r0 = pl.multiple_of(j * ch, ch)
        hg_ref[0, pl.ds(r0, ch), :] = chunk(r0, 0)
        return carry

    lax.fori_loop(0, nfull, loop_body, 0)
    if tail:
        out = chunk(seq - ch, ch - tail)
        hg_ref[0, seq - tail:seq, :] = out[ch - tail:, :]

    for hh in range(hb):
        cfull = cst[hh]
        c_ref[0, hh] = cfull[:DK_A, :DV_A]
        n_ref[0, hh] = _col_to_row(cfull[:DK_A, DV_A:DV_A + 1])
        m_ref[0, hh] = mst[hh, 0:1, 0:1]


def _mlstm_prompt(qk, v, og, gt, mhg, batch, seq):
    hb = MLSTM_HEADS_PER_STEP
    w = hb * LANES
    qk3, v3, og3 = (a.reshape(batch, seq, a.shape[-1]) for a in (qk, v, og))
    gt3 = gt.reshape(batch, seq, LANES)
    seq_blk = lambda b, g: (b, 0, g)
    hg, c, n, m = pl.pallas_call(
        functools.partial(_mlstm_prompt_body, seq=seq, hb=hb),
        grid=(batch, H_A // hb),
        in_specs=[pl.BlockSpec((1, seq, w), seq_blk), pl.BlockSpec((1, seq, w), seq_blk),
                  pl.BlockSpec((1, seq, w), seq_blk),
                  pl.BlockSpec((1, seq, LANES), lambda b, g: (b, 0, 0)),
                  pl.BlockSpec((hb, 1, DV_A), lambda b, g: (g, 0, 0))],
        out_specs=[pl.BlockSpec((1, seq, w), seq_blk),
                   pl.BlockSpec((1, hb, DK_A, DV_A), lambda b, g: (b, g, 0, 0)),
                   pl.BlockSpec((1, hb, 1, DK_A), lambda b, g: (b, g, 0, 0)),
                   pl.BlockSpec((1, hb, 1, 1), lambda b, g: (b, g, 0, 0))],
        out_shape=[jax.ShapeDtypeStruct((batch, seq, H_A * DV_A), BF16),
                   jax.ShapeDtypeStruct((batch, H_A, DK_A, DV_A), F32),
                   jax.ShapeDtypeStruct((batch, H_A, 1, DK_A), F32),
                   jax.ShapeDtypeStruct((batch, H_A, 1, 1), F32)],
        scratch_shapes=[pltpu.VMEM((hb, LANES, 2 * LANES), F32), pltpu.VMEM((hb, SUBLANES, LANES), F32)],
        compiler_params=_cparams("parallel", "arbitrary"),
        name="mlstm_prompt",
    )(qk3, v3, og3, gt3, mhg.reshape(H_A, 1, DV_A))
    return (hg.reshape(batch * seq, H_A * DV_A), c, n.reshape(batch, H_A, DK_A),
            m.reshape(batch, H_A))


def _mlstm_sample_body(qk_ref, v_ref, og_ref, gt_ref, mhg_ref, c0_ref, n0_ref, m0_ref,
                       hg_ref, c_ref, n_ref, m_ref, *, steps):
    ch = pl.cdiv(steps, SAMPLE_CHUNK_ALIGN) * SAMPLE_CHUNK_ALIGN
    pad = jnp.zeros((ch - steps, LANES), F32)
    padded = (lambda a: jnp.concatenate([a, pad], axis=0)) if ch > steps else (lambda a: a)
    rowi = _iota((ch, 1), 0)
    lane1 = _iota((1, LANES), 1)
    lane_c = _iota((DK_A, LANES), 1)
    nseq = qk_ref.shape[0]
    li, lf, qk, v, cfull, m0 = [], [], [], [], [], []
    for sq in range(nseq):
        gt = padded(gt_ref[sq])
        for hh in range(H_A):
            sl = slice(hh * LANES, (hh + 1) * LANES)
            li.append(jnp.where(rowi < steps, gt[:, hh:hh + 1], MASKED_GATE))
            lf.append(jnp.where(rowi < steps, gt[:, H_A + hh:H_A + hh + 1], 0.0))
            qk.append(padded(qk_ref[sq, :, sl].astype(F32)))
            v.append(padded(v_ref[sq, :, sl].astype(F32)))
            ncol = _row_to_col(n0_ref[sq, hh:hh + 1, :])
            top = jnp.concatenate([c0_ref[sq, hh], jnp.where(lane_c == 0, ncol, 0.0)], axis=1)
            cfull.append(jnp.concatenate([top, jnp.zeros((LANES - DK_A, 2 * LANES), F32)], axis=0))
            m0.append(m0_ref[sq, :, hh:hh + 1])
    hs, cnew, mnew = _mlstm_chunk_heads(qk, v, li, lf, cfull, m0)
    for sq in range(nseq):
        m_out = jnp.zeros((1, LANES), F32)
        outs = []
        for hh in range(H_A):
            i = sq * H_A + hh
            og = og_ref[sq, :, hh * LANES:(hh + 1) * LANES].astype(F32)
            outs.append(_mlstm_head_out(hs[i][:steps, :], og, mhg_ref[hh]))
            c_ref[sq, hh] = cnew[i][:DK_A, :DV_A]
            n_ref[sq, hh:hh + 1, :] = _col_to_row(cnew[i][:DK_A, DV_A:DV_A + 1])
            m_out = jnp.where(lane1 == hh, mnew[i], m_out)
        hg_ref[sq] = jnp.concatenate(outs, axis=1).astype(BF16)
        m_ref[sq] = m_out[:, :H_A]


def _mlstm_sample(qk, v, og, gt, mhg, c0, n0, m0, batch, steps):
    wide = H_A * LANES
    ns = MLSTM_SAMPLE_SEQS if batch % MLSTM_SAMPLE_SEQS == 0 else 1
    blk3 = lambda b: (b, 0, 0)
    hg, c, n, m = pl.pallas_call(
        functools.partial(_mlstm_sample_body, steps=steps),
        grid=(batch // ns,),
        in_specs=[pl.BlockSpec((ns, steps, wide), blk3), pl.BlockSpec((ns, steps, wide), blk3),
                  pl.BlockSpec((ns, steps, wide), blk3), pl.BlockSpec((ns, steps, LANES), blk3),
                  _const_spec((H_A, 1, DV_A)),
                  pl.BlockSpec((ns, H_A, DK_A, DV_A), lambda b: (b, 0, 0, 0)),
                  pl.BlockSpec((ns, H_A, DK_A), blk3), pl.BlockSpec((ns, 1, H_A), blk3)],
        out_specs=[pl.BlockSpec((ns, steps, wide), blk3),
                   pl.BlockSpec((ns, H_A, DK_A, DV_A), lambda b: (b, 0, 0, 0)),
                   pl.BlockSpec((ns, H_A, DK_A), blk3), pl.BlockSpec((ns, 1, H_A), blk3)],
        out_shape=[jax.ShapeDtypeStruct((batch, steps, wide), BF16),
                   jax.ShapeDtypeStruct((batch, H_A, DK_A, DV_A), F32),
                   jax.ShapeDtypeStruct((batch, H_A, DK_A), F32),
                   jax.ShapeDtypeStruct((batch, 1, H_A), F32)],
        compiler_params=_cparams("parallel"),
        name="mlstm_sample",
    )(qk.reshape(batch, steps, wide), v.reshape(batch, steps, wide), og.reshape(batch, steps, wide),
      gt.reshape(batch, steps, LANES), mhg.reshape(H_A, 1, DV_A), c0, n0, m0.reshape(batch, 1, H_A))
    return hg.reshape(batch * steps, wide), c, n, m.reshape(batch, H_A)


def _rows_to_lanes(x16, staging_ref):
    staging_ref[...] = jnp.zeros(staging_ref.shape, F32)
    staging_ref[:, 0:x16.shape[1]] = x16
    return staging_ref[...].T[0:x16.shape[1], :]


def _attn_prompt_body(qt_ref, k0_ref, k1_ref, vt_ref, og_ref, o_ref, s_scr, p_scr, acc_scr, qa_scr,
                      mask_scr, *, seq):
    tq, tk = ATTN_TQ, ATTN_TK
    tpad = qt_ref.shape[2]
    nfull = seq // tq
    nh = ATTN_HEADS
    heads = range(nh)

    def keys(h, rows):
        pair = slice((h // 2) * LANES, (h // 2 + 1) * LANES)
        return (k0_ref if h % 2 == 0 else k1_ref)[0, rows, pair]

    def augmented_queries(qt):
        out = []
        for pp in range(nh // 2):
            blk = qt[pp * LANES:(pp + 1) * LANES, :]
            row = _iota(blk.shape, 0)
            out.append(jnp.where(row < DH_B, blk, jnp.where(row < DH_B + N_SPLIT, -1.0, 0.0)).astype(BF16))
            out.append(jnp.where(row >= DH_B, blk, jnp.where(row < N_SPLIT, -1.0, 0.0)).astype(BF16))
        return out

    def with_ones(vt):
        return jnp.concatenate([vt, jnp.ones((ROWSUM_ROWS, vt.shape[1]), BF16)], axis=0)

    def tail_tile(q0, width, rows_out):
        qa = augmented_queries(qt_ref[0, :, pl.ds(q0, width)].astype(F32))
        causal = _iota((tpad, width), 0) <= q0 + _iota((tpad, width), 1)
        outs = []
        for hh in heads:
            s = jnp.where(causal, _dot(keys(hh, slice(None)), qa[hh]), -jnp.inf)
            p = jnp.exp2(s - jnp.max(s, axis=0, keepdims=True)).astype(BF16)
            full = _dot(with_ones(vt_ref[0, hh * DH_B:(hh + 1) * DH_B, :]), p)
            outs.append(full[:DH_B, :] / full[DH_B:DH_B + 1, :])
        out = jnp.concatenate(outs, axis=0).T[:rows_out, :]
        gate = _sigmoid(og_ref[0, pl.ds(q0, rows_out), :].astype(F32))
        o_ref[0, pl.ds(q0, rows_out), :] = (out * gate).astype(BF16)

    per = tq // tk
    for d in range(per):
        key = d * tk + _iota((tk, tq), 0)
        mask_scr[d] = jnp.where(key <= _iota((tk, tq), 1), 0.0, -jnp.inf)

    def value_product(hh, k0, slot):
        return _dot(with_ones(vt_ref[0, hh * DH_B:(hh + 1) * DH_B, pl.ds(k0, tk)]), p_scr[slot, hh])

    def pipe_step(ms, kidx, slot, diag_idx, issue_next):
        k0 = pl.multiple_of(kidx * tk, tk)
        if issue_next:
            for hh in heads:
                s_scr[1 - slot, hh] = _dot(keys(hh, pl.ds(k0 + tk, tk)), qa_scr[hh])
        kprev = pl.multiple_of(jnp.maximum(kidx - 1, 0) * tk, tk)
        pvs = [value_product(hh, kprev, 1 - slot) for hh in heads]
        def scores(hh):
            s = s_scr[slot, hh]
            return s if diag_idx is None else s + mask_scr[diag_idx]

        new = [jnp.maximum(ms[hh], jnp.max(scores(hh), axis=0, keepdims=True)) for hh in heads]
        for hh in heads:
            p_scr[slot, hh] = jnp.exp2(scores(hh) - new[hh]).astype(BF16)
        for hh in heads:
            acc_scr[hh] = (acc_scr[hh] + pvs[hh]) * jnp.exp2(ms[hh] - new[hh])
        return tuple(new)

    def q_tile_pipelined(i):
        q0 = pl.multiple_of(i * tq, tq)
        qa = augmented_queries(qt_ref[0, :, pl.ds(q0, tq)].astype(F32))
        for hh in heads:
            qa_scr[hh] = qa[hh]
        p_scr[1] = jnp.zeros(p_scr.shape[1:], BF16)
        acc_scr[...] = jnp.zeros(acc_scr.shape, F32)
        for hh in heads:
            s_scr[0, hh] = _dot(keys(hh, pl.ds(0, tk)), qa_scr[hh])
        ms = tuple(jnp.full((1, tq), NEG_INIT, F32) for _ in heads)

        def group(jj, st):
            for d in range(per):
                st = pipe_step(st, jj * per + d, d % 2, None, True)
            return st

        ms = lax.fori_loop(0, i, group, ms)
        for d in range(per):
            ms = pipe_step(ms, i * per + d, d % 2, d, d < per - 1)
        last = (per - 1) % 2
        klast = pl.multiple_of((i * per + per - 1) * tk, tk)
        outs = []
        for hh in heads:
            full = acc_scr[hh] + value_product(hh, klast, last)
            outs.append(full[:DH_B, :] / full[DH_B:DH_B + 1, :])
        out = jnp.concatenate(outs, axis=0).T
        gate = _sigmoid(og_ref[0, pl.ds(q0, tq), :].astype(F32))
        o_ref[0, pl.ds(q0, tq), :] = (out * gate).astype(BF16)

    def qbody(i, _):
        q_tile_pipelined(i)
        return 0

    lax.fori_loop(0, nfull, qbody, 0)
    if seq > nfull * tq:
        q0 = nfull * tq
        tail_tile(q0, tpad - q0, seq - q0)


def _attn_prompt(qt, k0, k1, vt, og, seq):
    batch, wide, tpad = qt.shape
    tail_w = tpad - (seq // ATTN_TQ) * ATTN_TQ
    assert ATTN_TQ % (2 * ATTN_TK) == 0 and 0 <= tail_w and tail_w % LANES == 0
    rows = lambda b, p: (b, 0, p)
    cols = lambda b, p: (b, p, 0)
    nh = ATTN_HEADS
    w = nh * DH_B
    out = pl.pallas_call(
        functools.partial(_attn_prompt_body, seq=seq),
        grid=(batch, H_B // nh),
        in_specs=[pl.BlockSpec((1, w, tpad), cols), pl.BlockSpec((1, tpad, w), rows),
                  pl.BlockSpec((1, tpad, w), rows), pl.BlockSpec((1, w, tpad), cols),
                  pl.BlockSpec((1, seq, w), rows)],
        out_specs=pl.BlockSpec((1, seq, w), rows),
        out_shape=jax.ShapeDtypeStruct((batch, seq, wide), BF16),
        scratch_shapes=[pltpu.VMEM((2, nh, ATTN_TK, ATTN_TQ), F32),
                        pltpu.VMEM((2, nh, ATTN_TK, ATTN_TQ), BF16),
                        pltpu.VMEM((nh, DH_B + ROWSUM_ROWS, ATTN_TQ), F32),
                        pltpu.VMEM((nh, LANES, ATTN_TQ), BF16),
                        pltpu.VMEM((ATTN_TQ // ATTN_TK, ATTN_TK, ATTN_TQ), F32)],
        compiler_params=_cparams("parallel", "parallel"),
        name="attn_prompt",
    )(qt, k0, k1, vt, og)
    return out.reshape(batch * seq, wide)


def _attn_sample_body(pt_ref, q_ref, kn_ref, vn_ref, lfn_ref, og_ref, *rest, steps, npages):
    k_refs, v_refs, lf_refs = rest[:npages], rest[npages:2 * npages], rest[2 * npages:3 * npages]
    o_ref, stage = rest[3 * npages], rest[3 * npages + 1]
    rows = H_B * steps
    wide = H_B * DH_B
    u = _iota((LANES, LANES), 0)
    s_ = _iota((LANES, LANES), 1)

    q = q_ref[0].astype(F32)
    qrep = jnp.concatenate([jnp.broadcast_to(q[t:t + 1, :], (H_B, wide)) for t in range(steps)], axis=0)
    diag = _iota((rows, wide), 0) % H_B == _iota((rows, wide), 1) // DH_B
    qbd = jnp.where(diag, qrep, 0.0).astype(BF16)

    lf_all = jnp.concatenate([r[0] for r in lf_refs], axis=0)
    later_and_ones = jnp.concatenate([jnp.where(u > s_, 1.0, 0.0), jnp.ones((LANES, LANES), F32)],
                                     axis=1).astype(BF16)
    wt = _dot_by_01(lf_all, later_and_ones)
    pr = _iota((npages * H_B, npages * H_B), 0)
    pc = _iota((npages * H_B, npages * H_B), 1)
    later_pages = jnp.where((pc % H_B == pr % H_B) & (pc // H_B > pr // H_B), 1.0, 0.0).astype(BF16)
    hi, mid, lo = _split3(wt[:, LANES:])
    rsum = wt[:, :LANES] + _dot(later_pages, hi) + _dot(later_pages, mid) + _dot(later_pages, lo)
    bias_past = jnp.concatenate(
        [jnp.concatenate([rsum[r * H_B:(r + 1) * H_B, :]] * steps, axis=0) for r in range(npages)], axis=1)

    kcat = jnp.concatenate([r[0].astype(BF16) for r in k_refs], axis=1)
    s_past = _dot(qbd, kcat) + bias_past

    zpad = jnp.zeros((LANES - steps, wide), F32)
    kn = jnp.concatenate([kn_ref[0].astype(F32), zpad], axis=0).astype(BF16)
    vn = jnp.concatenate([vn_ref[0].astype(F32), zpad], axis=0).astype(BF16)
    lfn = jnp.concatenate([lfn_ref[0], jnp.zeros((LANES - steps, H_B), F32)], axis=0)
    incl = jnp.where(u <= s_, 1.0, 0.0).astype(BF16)
    cnew = _dot_by_01(_rows_to_lanes(lfn, stage), incl)
    key = _iota((rows, LANES), 1)
    qry = _iota((rows, LANES), 0) // H_B
    bias_new = jnp.where(key <= qry, -jnp.concatenate([cnew] * steps, axis=0), -jnp.inf)
    s_new = _dot_nt(qbd, kn) + bias_new

    m = jnp.maximum(jnp.max(s_past, axis=1, keepdims=True), jnp.max(s_new, axis=1, keepdims=True))
    p_past = jnp.exp(s_past - m)
    p_new = jnp.exp(s_new - m)
    l = jnp.sum(p_past, axis=1, keepdims=True) + jnp.sum(p_new, axis=1, keepdims=True)
    vcat = jnp.concatenate([r[0].astype(BF16) for r in v_refs], axis=1)
    acc = _dot_nt(p_past.astype(BF16), vcat) + _dot(p_new.astype(BF16), vn)
    full = jnp.where(diag, acc / l, 0.0)
    out = jnp.concatenate([jnp.sum(full[t * H_B:(t + 1) * H_B, :], axis=0, keepdims=True)
                           for t in range(steps)], axis=0)
    o_ref[0] = (out * _sigmoid(og_ref[0].astype(F32))).astype(BF16)


def _attn_sample(q, kn, vn, lfn, og, cache_k, cache_v, cache_logf, page_table, batch, steps):
    wide = H_B * DH_B
    n_phys, page = cache_k.shape[0], cache_k.shape[1]
    npages = page_table.shape[1]
    assert page == LANES
    ck = jnp.transpose(cache_k, (0, 2, 3, 1)).reshape(n_phys, wide, page)
    cv = jnp.transpose(cache_v, (0, 2, 3, 1)).reshape(n_phys, wide, page)
    clf = jnp.transpose(cache_logf, (0, 2, 1))
    tok = lambda b, pt: (b, 0, 0)
    page_of = lambda r: (lambda b, pt: (pt[b, r], 0, 0))
    out = pl.pallas_call(
        functools.partial(_attn_sample_body, steps=steps, npages=npages),
        grid_spec=pltpu.PrefetchScalarGridSpec(
            num_scalar_prefetch=1,
            grid=(batch,),
            in_specs=[pl.BlockSpec((1, steps, wide), tok), pl.BlockSpec((1, steps, wide), tok),
                      pl.BlockSpec((1, steps, wide), tok), pl.BlockSpec((1, steps, H_B), tok),
                      pl.BlockSpec((1, steps, wide), tok)]
            + [pl.BlockSpec((1, wide, page), page_of(r)) for r in range(npages)]
            + [pl.BlockSpec((1, wide, page), page_of(r)) for r in range(npages)]
            + [pl.BlockSpec((1, H_B, page), page_of(r)) for r in range(npages)],
            out_specs=pl.BlockSpec((1, steps, wide), tok),
            scratch_shapes=[pltpu.VMEM((LANES, LANES), F32)]),
        out_shape=jax.ShapeDtypeStruct((batch, steps, wide), BF16),
        compiler_params=_cparams("parallel"),
        name="attn_sample",
    )(page_table, q.reshape(batch, steps, wide), kn.reshape(batch, steps, wide),
      vn.reshape(batch, steps, wide), lfn.reshape(batch, steps, H_B), og.reshape(batch, steps, wide),
      *([ck] * npages), *([cv] * npages), *([clf] * npages))
    return out.reshape(batch * steps, wide)


def _prep_params(norm_a, w_in_a, b_ig_a, b_fg_a, mh_norm_a, w_out_a, norm_kv, w_kvf, b_fg_b,
                 k_norm_b, norm_b, w_qo_b, q_norm_b, w_out_b, norm_ffn, w_gate_up, w_down,
                 norm_final):
    d = w_in_a.shape[1]
    hk, hv, hd = H_A * DK_A, H_A * DV_A, H_B * DH_B
    w_in = w_in_a[0]
    wq = w_in[:, :hk].reshape(d, H_A, DK_A)
    wk = w_in[:, hk:2 * hk].reshape(d, H_A, DK_A)
    row = lambda a: a.reshape(1, -1).astype(F32)
    pad_cols = lambda a: jnp.pad(a, ((0, 0), (0, LANES - a.shape[1])))
    lane = jnp.arange(H_A * LANES) % LANES
    return dict(
        norm_a=row(norm_a[0]),
        wqk=jnp.concatenate([wq, wk], axis=2).reshape(d, H_A * LANES).astype(BF16),
        wv=w_in[:, 2 * hk:2 * hk + hv].astype(BF16),
        wog=w_in[:, 2 * hk + hv:2 * hk + 2 * hv].astype(BF16),
        wg=pad_cols(w_in[:, 2 * hk + 2 * hv:]).astype(BF16),
        bg=pad_cols(jnp.concatenate([b_ig_a[0], b_fg_a[0]]).reshape(1, -1).astype(F32)),
        qs=jnp.where(lane < DK_A, DK_A ** -0.5, 1.0).reshape(1, -1).astype(F32),
        mhg=mh_norm_a[0].astype(F32),
        wo_a=w_out_a[0].astype(BF16),
        gkv=row(norm_kv),
        wk=w_kvf[:, :hd].astype(BF16),
        wvs=w_kvf[:, hd:2 * hd].astype(BF16),
        wf=pad_cols(w_kvf[:, 2 * hd:]).astype(BF16),
        bf=pad_cols(b_fg_b.reshape(1, -1).astype(F32)),
        kg=row(jnp.tile(k_norm_b, H_B)),
        gb=row(norm_b[0]),
        wq=w_qo_b[0][:, :hd].astype(BF16),
        wog_b=w_qo_b[0][:, hd:].astype(BF16),
        qg=row(jnp.tile(q_norm_b[0], H_B)) * DH_B ** -0.5,
        wo_b=w_out_b[0].astype(BF16),
        gf=[row(norm_ffn[l]) for l in range(2)],
        wgu=[w_gate_up[l].astype(BF16) for l in range(2)],
        wd=[w_down[l].astype(BF16) for l in range(2)],
        gfin=row(norm_final),
    )


def _layer0(h, p, tm, mlstm):
    qk, v, og, gt = _proj_in(h, p["norm_a"], p["wqk"], p["wv"], p["wog"], p["wg"], p["bg"], p["qs"], tm)
    hg, c, n, m = mlstm(qk, v, og, gt)
    h2 = _mix_ffn(hg, h, p["wo_a"], p["gf"][0], p["wgu"][0], p["wd"][0], p["gfin"], tm, False)
    return h2, c, n, m


def _shared_and_q(h2, p, tm):
    return _kvq_proj(h2, p["gkv"], p["wk"], p["wvs"], p["wf"], p["bf"], p["kg"], p["gb"],
                     p["wq"], p["wog_b"], p["qg"], tm)


def _layer1_tail(o, h2, p, tm, drop_lead=None):
    return _mix_ffn(o, h2, p["wo_b"], p["gf"][1], p["wgu"][1], p["wd"][1], p["gfin"], tm, True, drop_lead)


def kernel(x_prompt, x_sample, state_C, state_n, state_m, cache_k, cache_v, cache_logf, page_table,
           meta_tokens, norm_a, w_in_a, b_ig_a, b_fg_a, mh_norm_a, w_out_a, norm_kv, w_kvf, b_fg_b,
           k_norm_b, norm_b, w_qo_b, q_norm_b, w_out_b, norm_ffn, w_gate_up, w_down, norm_final):
    assert w_in_a.shape[0] == 1 and w_qo_b.shape[0] == 1 and norm_ffn.shape[0] == 2
    p = _prep_params(norm_a, w_in_a, b_ig_a, b_fg_a, mh_norm_a, w_out_a, norm_kv, w_kvf, b_fg_b,
                     k_norm_b, norm_b, w_qo_b, q_norm_b, w_out_b, norm_ffn, w_gate_up, w_down,
                     norm_final)
    bp, sp, d = x_prompt.shape
    bs, ss, _ = x_sample.shape
    tp = sp + N_META
    hd = H_B * DH_B
    tm = 512

    meta = jnp.broadcast_to(meta_tokens[None].astype(F32), (bp, N_META, d))
    h0 = jnp.concatenate([meta, x_prompt], axis=1).reshape(bp * tp, d)
    h2, p_c, p_n, p_m = _layer0(h0, p, tm, functools.partial(_mlstm_prompt, mhg=p["mhg"], batch=bp, seq=tp))
    kt, vt, lft, k0, k1, vtb, qt, og = _kvq_proj_t(
        h2, p["gkv"], p["wk"], p["wvs"], p["wf"], p["bf"], p["kg"], p["gb"], p["wq"], p["wog_b"],
        p["qg"] * LOG2E, bp, tp)
    o = _attn_prompt(qt, k0, k1, vtb, og, tp)
    y_prompt = _layer1_tail(o, h2, p, tm, drop_lead=(bp, tp, N_META)).reshape(bp, sp, d)
    p_k = jnp.transpose(kt.reshape(bp, H_B, DH_B, tp), (0, 3, 1, 2))
    p_v = jnp.transpose(vt.reshape(bp, H_B, DH_B, tp), (0, 3, 1, 2))
    p_lf = jnp.transpose(lft, (0, 2, 1))

    hs0 = x_sample.reshape(bs * ss, d)
    hs2, s_c, s_n, s_m = _layer0(
        hs0, p, tm, functools.partial(_mlstm_sample, mhg=p["mhg"], c0=state_C[0], n0=state_n[0],
                                      m0=state_m[0], batch=bs, steps=ss))
    ks, vs, lfs, kbs, vbs, qbs, ogs = _shared_and_q(hs2, p, tm)
    os_ = _attn_sample(qbs, kbs, vbs, lfs, ogs, cache_k, cache_v, cache_logf, page_table, bs, ss)
    y_sample = _layer1_tail(os_, hs2, p, tm).reshape(bs, ss, d)

    return (y_prompt, y_sample, p_c[None], p_n[None], p_m[None], p_k, p_v, p_lf,
            s_c[None], s_n[None], s_m[None], ks.reshape(bs, ss, H_B, DH_B),
            vs.reshape(bs, ss, H_B, DH_B), lfs.reshape(bs, ss, H_B))
```

```python
import functools

import jax
import jax.numpy as jnp
from jax import lax
from jax.experimental import pallas as pl
from jax.experimental.pallas import tpu as pltpu

F32 = jnp.float32
BF16 = jnp.bfloat16

N_META = 16
H_A = 8
DK_A = 64
DV_A = 128
GATE_CAP = 15.0
H_B = 16
DH_B = 64
EPS = 1e-6

LANES = 128
SUBLANES = 8
VMEM_LIMIT_BYTES = 56 * 1024 * 1024

MLSTM_CHUNK = 128
MLSTM_HEADS_PER_STEP = 4
MLSTM_SAMPLE_SEQS = 4
SAMPLE_CHUNK_ALIGN = 8
ATTN_TQ = 512
ATTN_TK = 256
ATTN_HEADS = 4
KVQ_T_TILE = 384
N_SPLIT = 3
ROWSUM_ROWS = 16
LOG2E = 1.4426950408889634
FF_CHUNK = 256
MASKED_GATE = -1e30
NEG_INIT = -1e30


def _cparams(*sem):
    return pltpu.CompilerParams(dimension_semantics=sem, vmem_limit_bytes=VMEM_LIMIT_BYTES)


def _const_spec(shape):
    nd = len(shape)
    return pl.BlockSpec(shape, lambda *_: (0,) * nd, pipeline_mode=pl.Buffered(1))


def _rms_scale(x):
    return lax.rsqrt(jnp.mean(x * x, axis=-1, keepdims=True) + EPS)


def _log_sigmoid(x):
    return jnp.minimum(x, 0.0) - jnp.log1p(jnp.exp(-jnp.abs(x)))


def _sigmoid(x):
    return 1.0 / (1.0 + jnp.exp(-x))


def _dot(a, b):
    return jnp.dot(a, b, preferred_element_type=F32)


def _dot_nt(a, b):
    return lax.dot_general(a, b, (((1,), (1,)), ((), ())), preferred_element_type=F32)


def _dot_tn(a, b):
    return lax.dot_general(a, b, (((0,), (0,)), ((), ())), preferred_element_type=F32)


def _split3(x):
    hi = x.astype(BF16)
    r1 = x - hi.astype(F32)
    mid = r1.astype(BF16)
    lo = (r1 - mid.astype(F32)).astype(BF16)
    return hi, mid, lo


def _dot_by_01(x, m01):
    hi, mid, lo = _split3(x)
    return _dot(hi, m01) + _dot(mid, m01) + _dot(lo, m01)


def _iota(shape, dim):
    return lax.broadcasted_iota(jnp.int32, shape, dim)


def _proj_in_body(x_ref, g_ref, wqk_ref, wv_ref, wog_ref, wg_ref, bg_ref, qs_ref,
                  qk_ref, v_ref, og_ref, gt_ref):
    x = x_ref[...]
    xn = (x * _rms_scale(x) * g_ref[...]).astype(BF16)
    qk_ref[...] = (_dot(xn, wqk_ref[...]) * qs_ref[...]).astype(BF16)
    v_ref[...] = _dot(xn, wv_ref[...]).astype(BF16)
    og_ref[...] = _dot(xn, wog_ref[...]).astype(BF16)
    z = _dot(xn, wg_ref[...]) + bg_ref[...]
    cap = GATE_CAP * jnp.tanh(z / GATE_CAP)
    lane = _iota(cap.shape, 1)
    gt_ref[...] = jnp.where(lane < H_A, cap, _log_sigmoid(cap))


def _proj_in(x, g, wqk, wv, wog, wg, bg, qs, tm):
    n, d = x.shape
    row = lambda i: (i, 0)
    return pl.pallas_call(
        _proj_in_body,
        grid=(pl.cdiv(n, tm),),
        in_specs=[pl.BlockSpec((tm, d), row), _const_spec(g.shape), _const_spec(wqk.shape),
                  _const_spec(wv.shape), _const_spec(wog.shape), _const_spec(wg.shape),
                  _const_spec(bg.shape), _const_spec(qs.shape)],
        out_specs=[pl.BlockSpec((tm, wqk.shape[1]), row), pl.BlockSpec((tm, wv.shape[1]), row),
                   pl.BlockSpec((tm, wog.shape[1]), row), pl.BlockSpec((tm, LANES), row)],
        out_shape=[jax.ShapeDtypeStruct((n, wqk.shape[1]), BF16),
                   jax.ShapeDtypeStruct((n, wv.shape[1]), BF16),
                   jax.ShapeDtypeStruct((n, wog.shape[1]), BF16),
                   jax.ShapeDtypeStruct((n, LANES), F32)],
        compiler_params=_cparams("parallel"),
        name="proj_in",
    )(x, g, wqk, wv, wog, wg, bg, qs)


def _mix_ffn_body(a_ref, h_ref, wo_ref, gf_ref, wgu_ref, wd_ref, gout_ref, o_ref, *, d_ff, final):
    h1 = h_ref[...] + _dot(a_ref[...], wo_ref[...])
    xn = (h1 * _rms_scale(h1) * gf_ref[...]).astype(BF16)
    acc = h1
    for c in range(d_ff // FF_CHUNK):
        lo = c * FF_CHUNK
        gate = _dot(xn, wgu_ref[:, lo:lo + FF_CHUNK])
        up = _dot(xn, wgu_ref[:, d_ff + lo:d_ff + lo + FF_CHUNK])
        act = (gate * _sigmoid(gate) * up).astype(BF16)
        acc = acc + _dot(act, wd_ref[lo:lo + FF_CHUNK, :])
    if final:
        acc = acc * _rms_scale(acc) * gout_ref[...]
    o_ref[...] = acc


def _mix_ffn(a, h, wo, gf, wgu, wd, gout, tm, final, drop_lead=None):
    n, d = h.shape
    d_ff = wd.shape[0]
    consts = [_const_spec(w.shape) for w in (wo, gf, wgu, wd, gout)]
    body = functools.partial(_mix_ffn_body, d_ff=d_ff, final=final)
    name = "mix_ffn_final" if final else "mix_ffn"
    if drop_lead is None:
        row = lambda i: (i, 0)
        return pl.pallas_call(
            body, grid=(pl.cdiv(n, tm),),
            in_specs=[pl.BlockSpec((tm, a.shape[1]), row), pl.BlockSpec((tm, d), row)] + consts,
            out_specs=pl.BlockSpec((tm, d), row),
            out_shape=jax.ShapeDtypeStruct((n, d), F32),
            compiler_params=_cparams("parallel"), name=name,
        )(a, h, wo, gf, wgu, wd, gout)
    batch, seq, lead = drop_lead
    nt = (seq - lead) // tm
    align = 2 * SUBLANES
    assert nt * tm == seq - lead and batch * seq == n and seq % align == 0 and lead % align == 0
    src = lambda b, i: (pl.multiple_of(b * seq + lead + i * tm, align), 0)
    return pl.pallas_call(
        body, grid=(batch, nt),
        in_specs=[pl.BlockSpec((pl.Element(tm), pl.Element(a.shape[1])), src),
                  pl.BlockSpec((pl.Element(tm), pl.Element(d)), src)] + consts,
        out_specs=pl.BlockSpec((tm, d), lambda b, i: (b * nt + i, 0)),
        out_shape=jax.ShapeDtypeStruct((batch * nt * tm, d), F32),
        compiler_params=_cparams("parallel", "parallel"), name=name,
    )(a, h, wo, gf, wgu, wd, gout)


def _head_rmsnorm64(x):
    outs = []
    for j in range(x.shape[1] // LANES):
        blk = x[:, j * LANES:(j + 1) * LANES]
        sq = blk * blk
        lane = _iota(blk.shape, 1)
        s_all = jnp.sum(sq, axis=1, keepdims=True)
        s_lo = jnp.sum(jnp.where(lane < DH_B, sq, 0.0), axis=1, keepdims=True)
        ms = jnp.where(lane < DH_B, s_lo, s_all - s_lo) / DH_B
        outs.append(blk * lax.rsqrt(ms + EPS))
    return jnp.concatenate(outs, axis=1)


def _kvq_compute(h, gkv_ref, wk_ref, wv_ref, wf_ref, bf_ref, kg_ref, gb_ref, wq_ref, wog_ref, qg_ref):
    hr = h * _rms_scale(h)
    xs = (hr * gkv_ref[...]).astype(BF16)
    k = _head_rmsnorm64(_dot(xs, wk_ref[...])) * kg_ref[...]
    v = _dot(xs, wv_ref[...])
    lf = _log_sigmoid(_dot(xs, wf_ref[...]) + bf_ref[...])
    xq = (hr * gb_ref[...]).astype(BF16)
    q = _head_rmsnorm64(_dot(xq, wq_ref[...])) * qg_ref[...]
    og = _dot(xq, wog_ref[...])
    return k, v, lf, q, og


def _kvq_body(h_ref, gkv_ref, wk_ref, wv_ref, wf_ref, bf_ref, kg_ref, gb_ref, wq_ref, wog_ref,
              qg_ref, k_ref, v_ref, lf_ref, kb_ref, vb_ref, qb_ref, og_ref):
    k, v, lf, q, og = _kvq_compute(h_ref[...], gkv_ref, wk_ref, wv_ref, wf_ref, bf_ref, kg_ref,
                                   gb_ref, wq_ref, wog_ref, qg_ref)
    k_ref[...] = k
    v_ref[...] = v
    kb_ref[...] = k.astype(BF16)
    vb_ref[...] = v.astype(BF16)
    lf_ref[...] = lf[:, :H_B]
    qb_ref[...] = q.astype(BF16)
    og_ref[...] = og.astype(BF16)


def _placement():
    row = jnp.arange(LANES)[:, None]
    h, part = row % H_B, row // H_B
    col = jnp.arange(H_B * DH_B)[None, :]
    lane = jnp.where(h % 2 == 0, DH_B, 0) + part
    return ((col == (h // 2) * LANES + lane) & (part < N_SPLIT)).astype(BF16)


def _pack3(x):
    hi, mid, lo = (p.astype(F32) for p in _split3(x))
    return (hi + pltpu.roll(mid, H_B, 1) + pltpu.roll(lo, 2 * H_B, 1)).astype(BF16)


def _unpack3_sum(y):
    s = y + pltpu.roll(y, LANES - H_B, 1) + pltpu.roll(y, LANES - 2 * H_B, 1)
    return jnp.where(_iota(y.shape, 1) < H_B, s, 0.0)


def _kvq_t_body(h_ref, gkv_ref, wk_ref, wv_ref, wf_ref, bf_ref, kg_ref, gb_ref, wq_ref, wog_ref,
                qg_ref, place_ref, kt_ref, vt_ref, lft_ref, k0_ref, k1_ref, vtb_ref, qt_ref, og_ref,
                carry, *, seq):
    tm = h_ref.shape[1]
    valid = pl.program_id(1) * tm + _iota((tm, 1), 0) < seq
    h = h_ref[0]
    hr = h * _rms_scale(h)
    xs = (hr * gkv_ref[...]).astype(BF16)
    k = jnp.where(valid, _head_rmsnorm64(_dot(xs, wk_ref[...])) * kg_ref[...], 0.0)
    lf = jnp.where(valid, _log_sigmoid(_dot(xs, wf_ref[...]) + bf_ref[...]), 0.0)

    @pl.when(pl.program_id(1) == 0)
    def _():
        carry[...] = jnp.zeros(carry.shape, F32)

    tril = jnp.where(_iota((LANES, LANES), 0) >= _iota((LANES, LANES), 1), 1.0, 0.0).astype(BF16)
    lo_half = (_iota((LANES, k.shape[1]), 1) & (LANES - 1)) < DH_B
    lf_heads = jnp.where(_iota(lf.shape, 1) < H_B, lf, 0.0)
    for sub in range(tm // LANES):
        rs = slice(sub * LANES, (sub + 1) * LANES)
        c = _unpack3_sum(_dot(tril, _pack3(lf_heads[rs, :]))) + carry[0:1, :]
        carry[0:1, :] = c[LANES - 1:LANES, :]
        bias = _dot(_pack3(c * LOG2E), place_ref[...])
        k0_ref[0, rs, :] = jnp.where(lo_half, k[rs, :], bias).astype(BF16)
        k1_ref[0, rs, :] = jnp.where(lo_half, bias, k[rs, :]).astype(BF16)

    kt_ref[0] = k.T
    lft_ref[0] = lf.T[:H_B, :]
    vt = jnp.where(valid, _dot(xs, wv_ref[...]), 0.0).T
    vt_ref[0] = vt
    vtb_ref[0] = vt.astype(BF16)
    xq = (hr * gb_ref[...]).astype(BF16)
    q = _head_rmsnorm64(_dot(xq, wq_ref[...])) * qg_ref[...]
    qt_ref[0] = jnp.where(valid, q, 0.0).T.astype(BF16)
    og_ref[0] = _dot(xq, wog_ref[...]).astype(BF16)


def _kvq_proj_t(h, gkv, wk, wv, wf, bf, kg, gb, wq, wog, qg, batch, seq):
    d = h.shape[1]
    hd = wk.shape[1]
    tm = KVQ_T_TILE
    nt = pl.cdiv(seq, tm)
    tpad = nt * tm
    rows = lambda b, i: (b, i, 0)
    cols = lambda b, i: (b, 0, i)
    place = _placement()
    return pl.pallas_call(
        functools.partial(_kvq_t_body, seq=seq),
        grid=(batch, nt),
        in_specs=[pl.BlockSpec((1, tm, d), rows)] + [_const_spec(w.shape) for w in
                                                      (gkv, wk, wv, wf, bf, kg, gb, wq, wog, qg, place)],
        out_specs=[pl.BlockSpec((1, hd, tm), cols), pl.BlockSpec((1, hd, tm), cols),
                   pl.BlockSpec((1, H_B, tm), cols), pl.BlockSpec((1, tm, hd), rows),
                   pl.BlockSpec((1, tm, hd), rows), pl.BlockSpec((1, hd, tm), cols),
                   pl.BlockSpec((1, hd, tm), cols), pl.BlockSpec((1, tm, wog.shape[1]), rows)],
        out_shape=[jax.ShapeDtypeStruct((batch, hd, seq), F32), jax.ShapeDtypeStruct((batch, hd, seq), F32),
                   jax.ShapeDtypeStruct((batch, H_B, seq), F32), jax.ShapeDtypeStruct((batch, tpad, hd), BF16),
                   jax.ShapeDtypeStruct((batch, tpad, hd), BF16), jax.ShapeDtypeStruct((batch, hd, tpad), BF16),
                   jax.ShapeDtypeStruct((batch, hd, tpad), BF16),
                   jax.ShapeDtypeStruct((batch, seq, wog.shape[1]), BF16)],
        scratch_shapes=[pltpu.VMEM((SUBLANES, LANES), F32)],
        compiler_params=_cparams("parallel", "arbitrary"),
        name="kvq_proj_t",
    )(h.reshape(batch, seq, d), gkv, wk, wv, wf, bf, kg, gb, wq, wog, qg, place)


def _kvq_proj(h, gkv, wk, wv, wf, bf, kg, gb, wq, wog, qg, tm):
    n, d = h.shape
    hd = wk.shape[1]
    row = lambda i: (i, 0)
    wide = pl.BlockSpec((tm, hd), row)
    return pl.pallas_call(
        _kvq_body,
        grid=(pl.cdiv(n, tm),),
        in_specs=[pl.BlockSpec((tm, d), row)] + [_const_spec(w.shape) for w in
                                                 (gkv, wk, wv, wf, bf, kg, gb, wq, wog, qg)],
        out_specs=[wide, wide, pl.BlockSpec((tm, H_B), row), wide, wide, wide,
                   pl.BlockSpec((tm, wog.shape[1]), row)],
        out_shape=[jax.ShapeDtypeStruct((n, hd), F32), jax.ShapeDtypeStruct((n, hd), F32),
                   jax.ShapeDtypeStruct((n, H_B), F32), jax.ShapeDtypeStruct((n, hd), BF16),
                   jax.ShapeDtypeStruct((n, hd), BF16), jax.ShapeDtypeStruct((n, hd), BF16),
                   jax.ShapeDtypeStruct((n, wog.shape[1]), BF16)],
        compiler_params=_cparams("parallel"),
        name="kvq_proj",
    )(h, gkv, wk, wv, wf, bf, kg, gb, wq, wog, qg)


def _col_to_row(col):
    n = col.shape[0]
    eye = _iota((n, n), 0) == _iota((n, n), 1)
    return jnp.sum(jnp.where(eye, col, 0.0), axis=0, keepdims=True)


def _row_to_col(row):
    n = row.shape[1]
    eye = _iota((n, n), 0) == _iota((n, n), 1)
    return jnp.sum(jnp.where(eye, row, 0.0), axis=1, keepdims=True)


def _mlstm_chunk_heads(qk, v, li, lf, cfull, m):
    heads = range(len(qk))
    L = qk[0].shape[0]
    tril = _iota((L, L), 0) >= _iota((L, L), 1)
    lane = _iota((L, LANES), 1)
    ones_col = jnp.where(lane == 0, 1.0, 0.0)

    q_lo = [jnp.where(lane < DK_A, qk[i], 0.0).astype(BF16) for i in heads]
    kq = [pltpu.roll(qk[i], DK_A, 1) for i in heads]
    vaug = [jnp.concatenate([v[i], ones_col], axis=1).astype(BF16) for i in heads]
    qk_t = [_dot_nt(q_lo[i], kq[i].astype(BF16)) for i in heads]
    q_c = [_dot(q_lo[i], cfull[i].astype(BF16)) for i in heads]

    lf_row = [_col_to_row(lf[i]) for i in heads]
    li_row = [_col_to_row(li[i]) for i in heads]
    b = [jnp.sum(jnp.where(tril, lf_row[i], 0.0), axis=1, keepdims=True) for i in heads]
    b_row = [_col_to_row(b[i]) for i in heads]
    dmat = [jnp.where(tril, b[i] - b_row[i] + li_row[i], -jnp.inf) for i in heads]
    dmax = [jnp.max(dmat[i], axis=1, keepdims=True) for i in heads]

    b_end = [b[i][L - 1:L, :] for i in heads]
    g = [b_end[i] - b[i] + li[i] for i in heads]
    m_new = [jnp.maximum(b_end[i] + m[i], jnp.max(g[i], axis=0, keepdims=True)) for i in heads]
    w_c = [jnp.exp(b_end[i] + m[i] - m_new[i]) for i in heads]
    upd = [_dot_tn((jnp.exp(g[i] - m_new[i]) * kq[i]).astype(BF16), vaug[i]) for i in heads]
    keep = _iota(cfull[0].shape, 0) < DK_A
    cfull_new = [jnp.where(keep, w_c[i] * cfull[i] + upd[i], 0.0) for i in heads]

    inter = [b[i] + m[i] for i in heads]
    m_t = [jnp.maximum(inter[i], dmax[i]) for i in heads]
    s = [(qk_t[i] * jnp.exp(dmat[i] - m_t[i])).astype(BF16) for i in heads]
    tot = [jnp.exp(inter[i] - m_t[i]) * q_c[i] + _dot(s[i], vaug[i]) for i in heads]
    h = [tot[i][:, :DV_A] / jnp.maximum(jnp.abs(tot[i][:, DV_A:DV_A + 1]), jnp.exp(-m_t[i])) for i in heads]
    return h, cfull_new, m_new


def _mlstm_head_out(h, og, gain):
    return h * _rms_scale(h) * gain * _sigmoid(og)


def _mlstm_prompt_body(qk_ref, v_ref, og_ref, gt_ref, mhg_ref, hg_ref, c_ref, n_ref, m_ref,
                       cst, mst, *, seq, hb):
    grp = pl.program_id(1)
    ch = MLSTM_CHUNK
    nfull, tail = seq // ch, seq % ch
    cst[...] = jnp.zeros(cst.shape, F32)
    mst[...] = jnp.zeros(mst.shape, F32)

    def chunk(r0, first_valid):
        gt = gt_ref[0, pl.ds(r0, ch), :]
        lane = _iota(gt.shape, 1)
        rowi = _iota((ch, 1), 0)
        li, lf, qk, v = [], [], [], []
        for hh in range(hb):
            head = grp * hb + hh
            li_h = jnp.sum(jnp.where(lane == head, gt, 0.0), axis=1, keepdims=True)
            lf_h = jnp.sum(jnp.where(lane == head + H_A, gt, 0.0), axis=1, keepdims=True)
            if first_valid:
                li_h = jnp.where(rowi >= first_valid, li_h, MASKED_GATE)
                lf_h = jnp.where(rowi >= first_valid, lf_h, 0.0)
            sl = slice(hh * LANES, (hh + 1) * LANES)
            li.append(li_h)
            lf.append(lf_h)
            qk.append(qk_ref[0, pl.ds(r0, ch), sl].astype(F32))
            v.append(v_ref[0, pl.ds(r0, ch), sl].astype(F32))
        hs, cnew, mnew = _mlstm_chunk_heads(qk, v, li, lf, [cst[hh] for hh in range(hb)],
                                            [mst[hh, 0:1, 0:1] for hh in range(hb)])
        outs = []
        for hh in range(hb):
            cst[hh] = cnew[hh]
            mst[hh] = jnp.broadcast_to(mnew[hh], mst.shape[1:])
            og = og_ref[0, pl.ds(r0, ch), hh * LANES:(hh + 1) * LANES].astype(F32)
            outs.append(_mlstm_head_out(hs[hh], og, mhg_ref[hh]))
        return jnp.concatenate(outs, axis=1).astype(BF16)

    def loop_body(j, carry):
        r0 = pl.multiple_of(j * ch, ch)
        hg_ref[0, pl.ds(r0, ch), :] = chunk(r0, 0)
        return carry

    lax.fori_loop(0, nfull, loop_body, 0)
    if tail:
        out = chunk(seq - ch, ch - tail)
        hg_ref[0, seq - tail:seq, :] = out[ch - tail:, :]

    for hh in range(hb):
        cfull = cst[hh]
        c_ref[0, hh] = cfull[:DK_A, :DV_A]
        n_ref[0, hh] = _col_to_row(cfull[:DK_A, DV_A:DV_A + 1])
        m_ref[0, hh] = mst[hh, 0:1, 0:1]


def _mlstm_prompt(qk, v, og, gt, mhg, batch, seq):
    hb = MLSTM_HEADS_PER_STEP
    w = hb * LANES
    qk3, v3, og3 = (a.reshape(batch, seq, a.shape[-1]) for a in (qk, v, og))
    gt3 = gt.reshape(batch, seq, LANES)
    seq_blk = lambda b, g: (b, 0, g)
    hg, c, n, m = pl.pallas_call(
        functools.partial(_mlstm_prompt_body, seq=seq, hb=hb),
        grid=(batch, H_A // hb),
        in_specs=[pl.BlockSpec((1, seq, w), seq_blk), pl.BlockSpec((1, seq, w), seq_blk),
                  pl.BlockSpec((1, seq, w), seq_blk),
                  pl.BlockSpec((1, seq, LANES), lambda b, g: (b, 0, 0)),
                  pl.BlockSpec((hb, 1, DV_A), lambda b, g: (g, 0, 0))],
        out_specs=[pl.BlockSpec((1, seq, w), seq_blk),
                   pl.BlockSpec((1, hb, DK_A, DV_A), lambda b, g: (b, g, 0, 0)),
                   pl.BlockSpec((1, hb, 1, DK_A), lambda b, g: (b, g, 0, 0)),
                   pl.BlockSpec((1, hb, 1, 1), lambda b, g: (b, g, 0, 0))],
        out_shape=[jax.ShapeDtypeStruct((batch, seq, H_A * DV_A), BF16),
                   jax.ShapeDtypeStruct((batch, H_A, DK_A, DV_A), F32),
                   jax.ShapeDtypeStruct((batch, H_A, 1, DK_A), F32),
                   jax.ShapeDtypeStruct((batch, H_A, 1, 1), F32)],
        scratch_shapes=[pltpu.VMEM((hb, LANES, 2 * LANES), F32), pltpu.VMEM((hb, SUBLANES, LANES), F32)],
        compiler_params=_cparams("parallel", "arbitrary"),
        name="mlstm_prompt",
    )(qk3, v3, og3, gt3, mhg.reshape(H_A, 1, DV_A))
    return (hg.reshape(batch * seq, H_A * DV_A), c, n.reshape(batch, H_A, DK_A),
            m.reshape(batch, H_A))


def _mlstm_sample_body(qk_ref, v_ref, og_ref, gt_ref, mhg_ref, c0_ref, n0_ref, m0_ref,
                       hg_ref, c_ref, n_ref, m_ref, *, steps):
    ch = pl.cdiv(steps, SAMPLE_CHUNK_ALIGN) * SAMPLE_CHUNK_ALIGN
    pad = jnp.zeros((ch - steps, LANES), F32)
    padded = (lambda a: jnp.concatenate([a, pad], axis=0)) if ch > steps else (lambda a: a)
    rowi = _iota((ch, 1), 0)
    lane1 = _iota((1, LANES), 1)
    lane_c = _iota((DK_A, LANES), 1)
    nseq = qk_ref.shape[0]
    li, lf, qk, v, cfull, m0 = [], [], [], [], [], []
    for sq in range(nseq):
        gt = padded(gt_ref[sq])
        lane = _iota(gt.shape, 1)
        for hh in range(H_A):
            sl = slice(hh * LANES, (hh + 1) * LANES)
            li_h = jnp.sum(jnp.where(lane == hh, gt, 0.0), axis=1, keepdims=True)
            lf_h = jnp.sum(jnp.where(lane == hh + H_A, gt, 0.0), axis=1, keepdims=True)
            li.append(jnp.where(rowi < steps, li_h, MASKED_GATE))
            lf.append(jnp.where(rowi < steps, lf_h, 0.0))
            qk.append(padded(qk_ref[sq, :, sl].astype(F32)))
            v.append(padded(v_ref[sq, :, sl].astype(F32)))
            ncol = _row_to_col(n0_ref[sq, hh:hh + 1, :])
            top = jnp.concatenate([c0_ref[sq, hh], jnp.where(lane_c == 0, ncol, 0.0)], axis=1)
            cfull.append(jnp.concatenate([top, jnp.zeros((LANES - DK_A, 2 * LANES), F32)], axis=0))
            m0.append(m0_ref[sq, :, hh:hh + 1])
    hs, cnew, mnew = _mlstm_chunk_heads(qk, v, li, lf, cfull, m0)
    for sq in range(nseq):
        m_out = jnp.zeros((1, LANES), F32)
        outs = []
        for hh in range(H_A):
            i = sq * H_A + hh
            og = og_ref[sq, :, hh * LANES:(hh + 1) * LANES].astype(F32)
            outs.append(_mlstm_head_out(hs[i][:steps, :], og, mhg_ref[hh]))
            c_ref[sq, hh] = cnew[i][:DK_A, :DV_A]
            n_ref[sq, hh:hh + 1, :] = _col_to_row(cnew[i][:DK_A, DV_A:DV_A + 1])
            m_out = jnp.where(lane1 == hh, mnew[i], m_out)
        hg_ref[sq] = jnp.concatenate(outs, axis=1).astype(BF16)
        m_ref[sq] = m_out[:, :H_A]


def _mlstm_sample(qk, v, og, gt, mhg, c0, n0, m0, batch, steps):
    wide = H_A * LANES
    ns = MLSTM_SAMPLE_SEQS if batch % MLSTM_SAMPLE_SEQS == 0 else 1
    blk3 = lambda b: (b, 0, 0)
    hg, c, n, m = pl.pallas_call(
        functools.partial(_mlstm_sample_body, steps=steps),
        grid=(batch // ns,),
        in_specs=[pl.BlockSpec((ns, steps, wide), blk3), pl.BlockSpec((ns, steps, wide), blk3),
                  pl.BlockSpec((ns, steps, wide), blk3), pl.BlockSpec((ns, steps, LANES), blk3),
                  _const_spec((H_A, 1, DV_A)),
                  pl.BlockSpec((ns, H_A, DK_A, DV_A), lambda b: (b, 0, 0, 0)),
                  pl.BlockSpec((ns, H_A, DK_A), blk3), pl.BlockSpec((ns, 1, H_A), blk3)],
        out_specs=[pl.BlockSpec((ns, steps, wide), blk3),
                   pl.BlockSpec((ns, H_A, DK_A, DV_A), lambda b: (b, 0, 0, 0)),
                   pl.BlockSpec((ns, H_A, DK_A), blk3), pl.BlockSpec((ns, 1, H_A), blk3)],
        out_shape=[jax.ShapeDtypeStruct((batch, steps, wide), BF16),
                   jax.ShapeDtypeStruct((batch, H_A, DK_A, DV_A), F32),
                   jax.ShapeDtypeStruct((batch, H_A, DK_A), F32),
                   jax.ShapeDtypeStruct((batch, 1, H_A), F32)],
        compiler_params=_cparams("parallel"),
        name="mlstm_sample",
    )(qk.reshape(batch, steps, wide), v.reshape(batch, steps, wide), og.reshape(batch, steps, wide),
      gt.reshape(batch, steps, LANES), mhg.reshape(H_A, 1, DV_A), c0, n0, m0.reshape(batch, 1, H_A))
    return hg.reshape(batch * steps, wide), c, n, m.reshape(batch, H_A)


def _rows_to_lanes(x16, staging_ref):
    staging_ref[...] = jnp.zeros(staging_ref.shape, F32)
    staging_ref[:, 0:x16.shape[1]] = x16
    return staging_ref[...].T[0:x16.shape[1], :]


def _attn_prompt_body(qt_ref, k0_ref, k1_ref, vt_ref, og_ref, o_ref, s_scr, p_scr, acc_scr, qa_scr,
                      mask_scr, *, seq):
    tq, tk = ATTN_TQ, ATTN_TK
    tpad = qt_ref.shape[2]
    nfull = seq // tq
    nh = ATTN_HEADS
    heads = range(nh)

    def keys(h, rows):
        pair = slice((h // 2) * LANES, (h // 2 + 1) * LANES)
        return (k0_ref if h % 2 == 0 else k1_ref)[0, rows, pair]

    def augmented_queries(qt):
        out = []
        for pp in range(nh // 2):
            blk = qt[pp * LANES:(pp + 1) * LANES, :]
            row = _iota(blk.shape, 0)
            out.append(jnp.where(row < DH_B, blk, jnp.where(row < DH_B + N_SPLIT, -1.0, 0.0)).astype(BF16))
            out.append(jnp.where(row >= DH_B, blk, jnp.where(row < N_SPLIT, -1.0, 0.0)).astype(BF16))
        return out

    def with_ones(vt):
        return jnp.concatenate([vt, jnp.ones((ROWSUM_ROWS, vt.shape[1]), BF16)], axis=0)

    def tail_tile(q0, width, rows_out):
        qa = augmented_queries(qt_ref[0, :, pl.ds(q0, width)].astype(F32))
        causal = _iota((tpad, width), 0) <= q0 + _iota((tpad, width), 1)
        outs = []
        for hh in heads:
            s = jnp.where(causal, _dot(keys(hh, slice(None)), qa[hh]), -jnp.inf)
            p = jnp.exp2(s - jnp.max(s, axis=0, keepdims=True)).astype(BF16)
            full = _dot(with_ones(vt_ref[0, hh * DH_B:(hh + 1) * DH_B, :]), p)
            outs.append(full[:DH_B, :] / full[DH_B:DH_B + 1, :])
        out = jnp.concatenate(outs, axis=0).T[:rows_out, :]
        gate = _sigmoid(og_ref[0, pl.ds(q0, rows_out), :].astype(F32))
        o_ref[0, pl.ds(q0, rows_out), :] = (out * gate).astype(BF16)

    per = tq // tk
    for d in range(per):
        key = d * tk + _iota((tk, tq), 0)
        mask_scr[d] = jnp.where(key <= _iota((tk, tq), 1), 0.0, -jnp.inf)

    def value_product(hh, k0, slot):
        return _dot(with_ones(vt_ref[0, hh * DH_B:(hh + 1) * DH_B, pl.ds(k0, tk)]), p_scr[slot, hh])

    def pipe_step(ms, kidx, slot, diag_idx, issue_next):
        k0 = pl.multiple_of(kidx * tk, tk)
        if issue_next:
            for hh in heads:
                s_scr[1 - slot, hh] = _dot(keys(hh, pl.ds(k0 + tk, tk)), qa_scr[hh])
        kprev = pl.multiple_of(jnp.maximum(kidx - 1, 0) * tk, tk)
        pvs = [value_product(hh, kprev, 1 - slot) for hh in heads]
        def scores(hh):
            s = s_scr[slot, hh]
            return s if diag_idx is None else s + mask_scr[diag_idx]

        new = [jnp.maximum(ms[hh], jnp.max(scores(hh), axis=0, keepdims=True)) for hh in heads]
        for hh in heads:
            p_scr[slot, hh] = jnp.exp2(scores(hh) - new[hh]).astype(BF16)
        for hh in heads:
            acc_scr[hh] = (acc_scr[hh] + pvs[hh]) * jnp.exp2(ms[hh] - new[hh])
        return tuple(new)

    def q_tile_pipelined(i):
        q0 = pl.multiple_of(i * tq, tq)
        qa = augmented_queries(qt_ref[0, :, pl.ds(q0, tq)].astype(F32))
        for hh in heads:
            qa_scr[hh] = qa[hh]
        p_scr[1] = jnp.zeros(p_scr.shape[1:], BF16)
        acc_scr[...] = jnp.zeros(acc_scr.shape, F32)
        for hh in heads:
            s_scr[0, hh] = _dot(keys(hh, pl.ds(0, tk)), qa_scr[hh])
        ms = tuple(jnp.full((1, tq), NEG_INIT, F32) for _ in heads)

        def group(jj, st):
            for d in range(per):
                st = pipe_step(st, jj * per + d, d % 2, None, True)
            return st

        ms = lax.fori_loop(0, i, group, ms)
        for d in range(per):
            ms = pipe_step(ms, i * per + d, d % 2, d, d < per - 1)
        last = (per - 1) % 2
        klast = pl.multiple_of((i * per + per - 1) * tk, tk)
        outs = []
        for hh in heads:
            full = acc_scr[hh] + value_product(hh, klast, last)
            outs.append(full[:DH_B, :] / full[DH_B:DH_B + 1, :])
        out = jnp.concatenate(outs, axis=0).T
        gate = _sigmoid(og_ref[0, pl.ds(q0, tq), :].astype(F32))
        o_ref[0, pl.ds(q0, tq), :] = (out * gate).astype(BF16)

    def qbody(i, _):
        q_tile_pipelined(i)
        return 0

    lax.fori_loop(0, nfull, qbody, 0)
    if seq > nfull * tq:
        q0 = nfull * tq
        tail_tile(q0, tpad - q0, seq - q0)


def _attn_prompt(qt, k0, k1, vt, og, seq):
    batch, wide, tpad = qt.shape
    tail_w = tpad - (seq // ATTN_TQ) * ATTN_TQ
    assert ATTN_TQ % (2 * ATTN_TK) == 0 and 0 <= tail_w and tail_w % LANES == 0
    rows = lambda b, p: (b, 0, p)
    cols = lambda b, p: (b, p, 0)
    nh = ATTN_HEADS
    w = nh * DH_B
    out = pl.pallas_call(
        functools.partial(_attn_prompt_body, seq=seq),
        grid=(batch, H_B // nh),
        in_specs=[pl.BlockSpec((1, w, tpad), cols), pl.BlockSpec((1, tpad, w), rows),
                  pl.BlockSpec((1, tpad, w), rows), pl.BlockSpec((1, w, tpad), cols),
                  pl.BlockSpec((1, seq, w), rows)],
        out_specs=pl.BlockSpec((1, seq, w), rows),
        out_shape=jax.ShapeDtypeStruct((batch, seq, wide), BF16),
        scratch_shapes=[pltpu.VMEM((2, nh, ATTN_TK, ATTN_TQ), F32),
                        pltpu.VMEM((2, nh, ATTN_TK, ATTN_TQ), BF16),
                        pltpu.VMEM((nh, DH_B + ROWSUM_ROWS, ATTN_TQ), F32),
                        pltpu.VMEM((nh, LANES, ATTN_TQ), BF16),
                        pltpu.VMEM((ATTN_TQ // ATTN_TK, ATTN_TK, ATTN_TQ), F32)],
        compiler_params=_cparams("parallel", "parallel"),
        name="attn_prompt",
    )(qt, k0, k1, vt, og)
    return out.reshape(batch * seq, wide)


def _attn_sample_body(pt_ref, q_ref, kn_ref, vn_ref, lfn_ref, og_ref, *rest, steps, npages):
    k_refs, v_refs, lf_refs = rest[:npages], rest[npages:2 * npages], rest[2 * npages:3 * npages]
    o_ref, stage = rest[3 * npages], rest[3 * npages + 1]
    rows = H_B * steps
    wide = H_B * DH_B
    u = _iota((LANES, LANES), 0)
    s_ = _iota((LANES, LANES), 1)

    q = q_ref[0].astype(F32)
    qrep = jnp.concatenate([jnp.broadcast_to(q[t:t + 1, :], (H_B, wide)) for t in range(steps)], axis=0)
    diag = _iota((rows, wide), 0) % H_B == _iota((rows, wide), 1) // DH_B
    qbd = jnp.where(diag, qrep, 0.0).astype(BF16)

    lf_all = jnp.concatenate([r[0] for r in lf_refs], axis=0)
    later_and_ones = jnp.concatenate([jnp.where(u > s_, 1.0, 0.0), jnp.ones((LANES, LANES), F32)],
                                     axis=1).astype(BF16)
    wt = _dot_by_01(lf_all, later_and_ones)
    pr = _iota((npages * H_B, npages * H_B), 0)
    pc = _iota((npages * H_B, npages * H_B), 1)
    later_pages = jnp.where((pc % H_B == pr % H_B) & (pc // H_B > pr // H_B), 1.0, 0.0).astype(BF16)
    hi, mid, lo = _split3(wt[:, LANES:])
    rsum = wt[:, :LANES] + _dot(later_pages, hi) + _dot(later_pages, mid) + _dot(later_pages, lo)
    bias_past = jnp.concatenate(
        [jnp.concatenate([rsum[r * H_B:(r + 1) * H_B, :]] * steps, axis=0) for r in range(npages)], axis=1)

    kcat = jnp.concatenate([r[0].astype(BF16) for r in k_refs], axis=1)
    s_past = _dot(qbd, kcat) + bias_past

    zpad = jnp.zeros((LANES - steps, wide), F32)
    kn = jnp.concatenate([kn_ref[0].astype(F32), zpad], axis=0).astype(BF16)
    vn = jnp.concatenate([vn_ref[0].astype(F32), zpad], axis=0).astype(BF16)
    lfn = jnp.concatenate([lfn_ref[0], jnp.zeros((LANES - steps, H_B), F32)], axis=0)
    incl = jnp.where(u <= s_, 1.0, 0.0).astype(BF16)
    cnew = _dot_by_01(_rows_to_lanes(lfn, stage), incl)
    key = _iota((rows, LANES), 1)
    qry = _iota((rows, LANES), 0) // H_B
    bias_new = jnp.where(key <= qry, -jnp.concatenate([cnew] * steps, axis=0), -jnp.inf)
    s_new = _dot_nt(qbd, kn) + bias_new

    m = jnp.maximum(jnp.max(s_past, axis=1, keepdims=True), jnp.max(s_new, axis=1, keepdims=True))
    p_past = jnp.exp(s_past - m)
    p_new = jnp.exp(s_new - m)
    l = jnp.sum(p_past, axis=1, keepdims=True) + jnp.sum(p_new, axis=1, keepdims=True)
    vcat = jnp.concatenate([r[0].astype(BF16) for r in v_refs], axis=1)
    acc = _dot_nt(p_past.astype(BF16), vcat) + _dot(p_new.astype(BF16), vn)
    full = jnp.where(diag, acc / l, 0.0)
    out = jnp.concatenate([jnp.sum(full[t * H_B:(t + 1) * H_B, :], axis=0, keepdims=True)
                           for t in range(steps)], axis=0)
    o_ref[0] = (out * _sigmoid(og_ref[0].astype(F32))).astype(BF16)


def _attn_sample(q, kn, vn, lfn, og, cache_k, cache_v, cache_logf, page_table, batch, steps):
    wide = H_B * DH_B
    n_phys, page = cache_k.shape[0], cache_k.shape[1]
    npages = page_table.shape[1]
    assert page == LANES
    ck = jnp.transpose(cache_k, (0, 2, 3, 1)).reshape(n_phys, wide, page)
    cv = jnp.transpose(cache_v, (0, 2, 3, 1)).reshape(n_phys, wide, page)
    clf = jnp.transpose(cache_logf, (0, 2, 1))
    tok = lambda b, pt: (b, 0, 0)
    page_of = lambda r: (lambda b, pt: (pt[b, r], 0, 0))
    out = pl.pallas_call(
        functools.partial(_attn_sample_body, steps=steps, npages=npages),
        grid_spec=pltpu.PrefetchScalarGridSpec(
            num_scalar_prefetch=1,
            grid=(batch,),
            in_specs=[pl.BlockSpec((1, steps, wide), tok), pl.BlockSpec((1, steps, wide), tok),
                      pl.BlockSpec((1, steps, wide), tok), pl.BlockSpec((1, steps, H_B), tok),
                      pl.BlockSpec((1, steps, wide), tok)]
            + [pl.BlockSpec((1, wide, page), page_of(r)) for r in range(npages)]
            + [pl.BlockSpec((1, wide, page), page_of(r)) for r in range(npages)]
            + [pl.BlockSpec((1, H_B, page), page_of(r)) for r in range(npages)],
            out_specs=pl.BlockSpec((1, steps, wide), tok),
            scratch_shapes=[pltpu.VMEM((LANES, LANES), F32)]),
        out_shape=jax.ShapeDtypeStruct((batch, steps, wide), BF16),
        compiler_params=_cparams("parallel"),
        name="attn_sample",
    )(page_table, q.reshape(batch, steps, wide), kn.reshape(batch, steps, wide),
      vn.reshape(batch, steps, wide), lfn.reshape(batch, steps, H_B), og.reshape(batch, steps, wide),
      *([ck] * npages), *([cv] * npages), *([clf] * npages))
    return out.reshape(batch * steps, wide)


def _prep_params(norm_a, w_in_a, b_ig_a, b_fg_a, mh_norm_a, w_out_a, norm_kv, w_kvf, b_fg_b,
                 k_norm_b, norm_b, w_qo_b, q_norm_b, w_out_b, norm_ffn, w_gate_up, w_down,
                 norm_final):
    d = w_in_a.shape[1]
    hk, hv, hd = H_A * DK_A, H_A * DV_A, H_B * DH_B
    w_in = w_in_a[0]
    wq = w_in[:, :hk].reshape(d, H_A, DK_A)
    wk = w_in[:, hk:2 * hk].reshape(d, H_A, DK_A)
    row = lambda a: a.reshape(1, -1).astype(F32)
    pad_cols = lambda a: jnp.pad(a, ((0, 0), (0, LANES - a.shape[1])))
    lane = jnp.arange(H_A * LANES) % LANES
    return dict(
        norm_a=row(norm_a[0]),
        wqk=jnp.concatenate([wq, wk], axis=2).reshape(d, H_A * LANES).astype(BF16),
        wv=w_in[:, 2 * hk:2 * hk + hv].astype(BF16),
        wog=w_in[:, 2 * hk + hv:2 * hk + 2 * hv].astype(BF16),
        wg=pad_cols(w_in[:, 2 * hk + 2 * hv:]).astype(BF16),
        bg=pad_cols(jnp.concatenate([b_ig_a[0], b_fg_a[0]]).reshape(1, -1).astype(F32)),
        qs=jnp.where(lane < DK_A, DK_A ** -0.5, 1.0).reshape(1, -1).astype(F32),
        mhg=mh_norm_a[0].astype(F32),
        wo_a=w_out_a[0].astype(BF16),
        gkv=row(norm_kv),
        wk=w_kvf[:, :hd].astype(BF16),
        wvs=w_kvf[:, hd:2 * hd].astype(BF16),
        wf=pad_cols(w_kvf[:, 2 * hd:]).astype(BF16),
        bf=pad_cols(b_fg_b.reshape(1, -1).astype(F32)),
        kg=row(jnp.tile(k_norm_b, H_B)),
        gb=row(norm_b[0]),
        wq=w_qo_b[0][:, :hd].astype(BF16),
        wog_b=w_qo_b[0][:, hd:].astype(BF16),
        qg=row(jnp.tile(q_norm_b[0], H_B)) * DH_B ** -0.5,
        wo_b=w_out_b[0].astype(BF16),
        gf=[row(norm_ffn[l]) for l in range(2)],
        wgu=[w_gate_up[l].astype(BF16) for l in range(2)],
        wd=[w_down[l].astype(BF16) for l in range(2)],
        gfin=row(norm_final),
    )


def _layer0(h, p, tm, mlstm):
    qk, v, og, gt = _proj_in(h, p["norm_a"], p["wqk"], p["wv"], p["wog"], p["wg"], p["bg"], p["qs"], tm)
    hg, c, n, m = mlstm(qk, v, og, gt)
    h2 = _mix_ffn(hg, h, p["wo_a"], p["gf"][0], p["wgu"][0], p["wd"][0], p["gfin"], tm, False)
    return h2, c, n, m


def _shared_and_q(h2, p, tm):
    return _kvq_proj(h2, p["gkv"], p["wk"], p["wvs"], p["wf"], p["bf"], p["kg"], p["gb"],
                     p["wq"], p["wog_b"], p["qg"], tm)


def _layer1_tail(o, h2, p, tm, drop_lead=None):
    return _mix_ffn(o, h2, p["wo_b"], p["gf"][1], p["wgu"][1], p["wd"][1], p["gfin"], tm, True, drop_lead)


def kernel(x_prompt, x_sample, state_C, state_n, state_m, cache_k, cache_v, cache_logf, page_table,
           meta_tokens, norm_a, w_in_a, b_ig_a, b_fg_a, mh_norm_a, w_out_a, norm_kv, w_kvf, b_fg_b,
           k_norm_b, norm_b, w_qo_b, q_norm_b, w_out_b, norm_ffn, w_gate_up, w_down, norm_final):
    assert w_in_a.shape[0] == 1 and w_qo_b.shape[0] == 1 and norm_ffn.shape[0] == 2
    p = _prep_params(norm_a, w_in_a, b_ig_a, b_fg_a, mh_norm_a, w_out_a, norm_kv, w_kvf, b_fg_b,
                     k_norm_b, norm_b, w_qo_b, q_norm_b, w_out_b, norm_ffn, w_gate_up, w_down,
                     norm_final)
    bp, sp, d = x_prompt.shape
    bs, ss, _ = x_sample.shape
    tp = sp + N_META
    hd = H_B * DH_B
    tm = 512

    meta = jnp.broadcast_to(meta_tokens[None].astype(F32), (bp, N_META, d))
    h0 = jnp.concatenate([meta, x_prompt], axis=1).reshape(bp * tp, d)
    h2, p_c, p_n, p_m = _layer0(h0, p, tm, functools.partial(_mlstm_prompt, mhg=p["mhg"], batch=bp, seq=tp))
    kt, vt, lft, k0, k1, vtb, qt, og = _kvq_proj_t(
        h2, p["gkv"], p["wk"], p["wvs"], p["wf"], p["bf"], p["kg"], p["gb"], p["wq"], p["wog_b"],
        p["qg"] * LOG2E, bp, tp)
    o = _attn_prompt(qt, k0, k1, vtb, og, tp)
    y_prompt = _layer1_tail(o, h2, p, tm, drop_lead=(bp, tp, N_META)).reshape(bp, sp, d)
    p_k = jnp.transpose(kt.reshape(bp, H_B, DH_B, tp), (0, 3, 1, 2))
    p_v = jnp.transpose(vt.reshape(bp, H_B, DH_B, tp), (0, 3, 1, 2))
    p_lf = jnp.transpose(lft, (0, 2, 1))

    hs0 = x_sample.reshape(bs * ss, d)
    hs2, s_c, s_n, s_m = _layer0(
        hs0, p, tm, functools.partial(_mlstm_sample, mhg=p["mhg"], c0=state_C[0], n0=state_n[0],
                                      m0=state_m[0], batch=bs, steps=ss))
    ks, vs, lfs, kbs, vbs, qbs, ogs = _shared_and_q(hs2, p, tm)
    os_ = _attn_sample(qbs, kbs, vbs, lfs, ogs, cache_k, cache_v, cache_logf, page_table, bs, ss)
    y_sample = _layer1_tail(os_, hs2, p, tm).reshape(bs, ss, d)

    return (y_prompt, y_sample, p_c[None], p_n[None], p_m[None], p_k, p_v, p_lf,
            s_c[None], s_n[None], s_m[None], ks.reshape(bs, ss, H_B, DH_B),
            vs.reshape(bs, ss, H_B, DH_B), lfs.reshape(bs, ss, H_B))
```

```python
import functools

import jax
import jax.numpy as jnp
from jax import lax
from jax.experimental import pallas as pl
from jax.experimental.pallas import tpu as pltpu

F32 = jnp.float32
BF16 = jnp.bfloat16

N_META = 16
H_A = 8
DK_A = 64
DV_A = 128
GATE_CAP = 15.0
H_B = 16
DH_B = 64
EPS = 1e-6

LANES = 128
SUBLANES = 8
VMEM_LIMIT_BYTES = 56 * 1024 * 1024

MLSTM_CHUNK = 128
MLSTM_HEADS_PER_STEP = 4
MLSTM_SAMPLE_SEQS = 4
SAMPLE_CHUNK_ALIGN = 8
ATTN_TQ = 512
ATTN_TK = 256
ATTN_HEADS = 4
KVQ_T_TILE = 384
N_SPLIT = 3
ROWSUM_ROWS = 16
LOG2E = 1.4426950408889634
FF_CHUNK = 256
MASKED_GATE = -1e30
NEG_INIT = -1e30


def _cparams(*sem):
    return pltpu.CompilerParams(dimension_semantics=sem, vmem_limit_bytes=VMEM_LIMIT_BYTES)


def _const_spec(shape):
    nd = len(shape)
    return pl.BlockSpec(shape, lambda *_: (0,) * nd, pipeline_mode=pl.Buffered(1))


def _rms_scale(x):
    return lax.rsqrt(jnp.mean(x * x, axis=-1, keepdims=True) + EPS)


def _log_sigmoid(x):
    return jnp.minimum(x, 0.0) - jnp.log1p(jnp.exp(-jnp.abs(x)))


def _sigmoid(x):
    return 1.0 / (1.0 + jnp.exp(-x))


def _dot(a, b):
    return jnp.dot(a, b, preferred_element_type=F32)


def _dot_nt(a, b):
    return lax.dot_general(a, b, (((1,), (1,)), ((), ())), preferred_element_type=F32)


def _dot_tn(a, b):
    return lax.dot_general(a, b, (((0,), (0,)), ((), ())), preferred_element_type=F32)


def _split3(x):
    hi = x.astype(BF16)
    r1 = x - hi.astype(F32)
    mid = r1.astype(BF16)
    lo = (r1 - mid.astype(F32)).astype(BF16)
    return hi, mid, lo


def _dot_by_01(x, m01):
    hi, mid, lo = _split3(x)
    return _dot(hi, m01) + _dot(mid, m01) + _dot(lo, m01)


def _iota(shape, dim):
    return lax.broadcasted_iota(jnp.int32, shape, dim)


def _proj_in_body(x_ref, g_ref, wqk_ref, wv_ref, wog_ref, wg_ref, bg_ref, qs_ref,
                  qk_ref, v_ref, og_ref, gt_ref):
    x = x_ref[...]
    xn = (x * _rms_scale(x) * g_ref[...]).astype(BF16)
    qk_ref[...] = (_dot(xn, wqk_ref[...]) * qs_ref[...]).astype(BF16)
    v_ref[...] = _dot(xn, wv_ref[...]).astype(BF16)
    og_ref[...] = _dot(xn, wog_ref[...]).astype(BF16)
    z = _dot(xn, wg_ref[...]) + bg_ref[...]
    cap = GATE_CAP * jnp.tanh(z / GATE_CAP)
    lane = _iota(cap.shape, 1)
    gt_ref[...] = jnp.where(lane < H_A, cap, _log_sigmoid(cap))


def _proj_in(x, g, wqk, wv, wog, wg, bg, qs, tm):
    n, d = x.shape
    row = lambda i: (i, 0)
    return pl.pallas_call(
        _proj_in_body,
        grid=(pl.cdiv(n, tm),),
        in_specs=[pl.BlockSpec((tm, d), row), _const_spec(g.shape), _const_spec(wqk.shape),
                  _const_spec(wv.shape), _const_spec(wog.shape), _const_spec(wg.shape),
                  _const_spec(bg.shape), _const_spec(qs.shape)],
        out_specs=[pl.BlockSpec((tm, wqk.shape[1]), row), pl.BlockSpec((tm, wv.shape[1]), row),
                   pl.BlockSpec((tm, wog.shape[1]), row), pl.BlockSpec((tm, LANES), row)],
        out_shape=[jax.ShapeDtypeStruct((n, wqk.shape[1]), BF16),
                   jax.ShapeDtypeStruct((n, wv.shape[1]), BF16),
                   jax.ShapeDtypeStruct((n, wog.shape[1]), BF16),
                   jax.ShapeDtypeStruct((n, LANES), F32)],
        compiler_params=_cparams("parallel"),
        name="proj_in",
    )(x, g, wqk, wv, wog, wg, bg, qs)


def _mix_ffn_body(a_ref, h_ref, wo_ref, gf_ref, wgu_ref, wd_ref, gout_ref, o_ref, *, d_ff, final):
    h1 = h_ref[...] + _dot(a_ref[...], wo_ref[...])
    xn = (h1 * _rms_scale(h1) * gf_ref[...]).astype(BF16)
    acc = h1
    for c in range(d_ff // FF_CHUNK):
        lo = c * FF_CHUNK
        gate = _dot(xn, wgu_ref[:, lo:lo + FF_CHUNK])
        up = _dot(xn, wgu_ref[:, d_ff + lo:d_ff + lo + FF_CHUNK])
        act = (gate * _sigmoid(gate) * up).astype(BF16)
        acc = acc + _dot(act, wd_ref[lo:lo + FF_CHUNK, :])
    if final:
        acc = acc * _rms_scale(acc) * gout_ref[...]
    o_ref[...] = acc


def _mix_ffn(a, h, wo, gf, wgu, wd, gout, tm, final, drop_lead=None):
    n, d = h.shape
    d_ff = wd.shape[0]
    consts = [_const_spec(w.shape) for w in (wo, gf, wgu, wd, gout)]
    body = functools.partial(_mix_ffn_body, d_ff=d_ff, final=final)
    name = "mix_ffn_final" if final else "mix_ffn"
    if drop_lead is None:
        row = lambda i: (i, 0)
        return pl.pallas_call(
            body, grid=(pl.cdiv(n, tm),),
            in_specs=[pl.BlockSpec((tm, a.shape[1]), row), pl.BlockSpec((tm, d), row)] + consts,
            out_specs=pl.BlockSpec((tm, d), row),
            out_shape=jax.ShapeDtypeStruct((n, d), F32),
            compiler_params=_cparams("parallel"), name=name,
        )(a, h, wo, gf, wgu, wd, gout)
    batch, seq, lead = drop_lead
    nt = (seq - lead) // tm
    align = 2 * SUBLANES
    assert nt * tm == seq - lead and batch * seq == n and seq % align == 0 and lead % align == 0
    src = lambda b, i: (pl.multiple_of(b * seq + lead + i * tm, align), 0)
    return pl.pallas_call(
        body, grid=(batch, nt),
        in_specs=[pl.BlockSpec((pl.Element(tm), pl.Element(a.shape[1])), src),
                  pl.BlockSpec((pl.Element(tm), pl.Element(d)), src)] + consts,
        out_specs=pl.BlockSpec((tm, d), lambda b, i: (b * nt + i, 0)),
        out_shape=jax.ShapeDtypeStruct((batch * nt * tm, d), F32),
        compiler_params=_cparams("parallel", "parallel"), name=name,
    )(a, h, wo, gf, wgu, wd, gout)


def _head_rmsnorm64(x):
    outs = []
    for j in range(x.shape[1] // LANES):
        blk = x[:, j * LANES:(j + 1) * LANES]
        sq = blk * blk
        lane = _iota(blk.shape, 1)
        s_all = jnp.sum(sq, axis=1, keepdims=True)
        s_lo = jnp.sum(jnp.where(lane < DH_B, sq, 0.0), axis=1, keepdims=True)
        ms = jnp.where(lane < DH_B, s_lo, s_all - s_lo) / DH_B
        outs.append(blk * lax.rsqrt(ms + EPS))
    return jnp.concatenate(outs, axis=1)


def _kvq_compute(h, gkv_ref, wk_ref, wv_ref, wf_ref, bf_ref, kg_ref, gb_ref, wq_ref, wog_ref, qg_ref):
    hr = h * _rms_scale(h)
    xs = (hr * gkv_ref[...]).astype(BF16)
    k = _head_rmsnorm64(_dot(xs, wk_ref[...])) * kg_ref[...]
    v = _dot(xs, wv_ref[...])
    lf = _log_sigmoid(_dot(xs, wf_ref[...]) + bf_ref[...])
    xq = (hr * gb_ref[...]).astype(BF16)
    q = _head_rmsnorm64(_dot(xq, wq_ref[...])) * qg_ref[...]
    og = _dot(xq, wog_ref[...])
    return k, v, lf, q, og


def _kvq_body(h_ref, gkv_ref, wk_ref, wv_ref, wf_ref, bf_ref, kg_ref, gb_ref, wq_ref, wog_ref,
              qg_ref, k_ref, v_ref, lf_ref, kb_ref, vb_ref, qb_ref, og_ref):
    k, v, lf, q, og = _kvq_compute(h_ref[...], gkv_ref, wk_ref, wv_ref, wf_ref, bf_ref, kg_ref,
                                   gb_ref, wq_ref, wog_ref, qg_ref)
    k_ref[...] = k
    v_ref[...] = v
    kb_ref[...] = k.astype(BF16)
    vb_ref[...] = v.astype(BF16)
    lf_ref[...] = lf[:, :H_B]
    qb_ref[...] = q.astype(BF16)
    og_ref[...] = og.astype(BF16)


def _placement():
    row = jnp.arange(LANES)[:, None]
    h, part = row % H_B, row // H_B
    col = jnp.arange(H_B * DH_B)[None, :]
    lane = jnp.where(h % 2 == 0, DH_B, 0) + part
    return ((col == (h // 2) * LANES + lane) & (part < N_SPLIT)).astype(BF16)


def _pack3(x):
    hi, mid, lo = (p.astype(F32) for p in _split3(x))
    return (hi + pltpu.roll(mid, H_B, 1) + pltpu.roll(lo, 2 * H_B, 1)).astype(BF16)


def _unpack3_sum(y):
    s = y + pltpu.roll(y, LANES - H_B, 1) + pltpu.roll(y, LANES - 2 * H_B, 1)
    return jnp.where(_iota(y.shape, 1) < H_B, s, 0.0)


def _kvq_t_body(h_ref, gkv_ref, wk_ref, wv_ref, wf_ref, bf_ref, kg_ref, gb_ref, wq_ref, wog_ref,
                qg_ref, place_ref, kt_ref, vt_ref, lft_ref, k0_ref, k1_ref, vtb_ref, qt_ref, og_ref,
                carry, *, seq):
    tm = h_ref.shape[1]
    valid = pl.program_id(1) * tm + _iota((tm, 1), 0) < seq
    h = h_ref[0]
    hr = h * _rms_scale(h)
    xs = (hr * gkv_ref[...]).astype(BF16)
    k = jnp.where(valid, _head_rmsnorm64(_dot(xs, wk_ref[...])) * kg_ref[...], 0.0)
    lf = jnp.where(valid, _log_sigmoid(_dot(xs, wf_ref[...]) + bf_ref[...]), 0.0)

    @pl.when(pl.program_id(1) == 0)
    def _():
        carry[...] = jnp.zeros(carry.shape, F32)

    tril = jnp.where(_iota((LANES, LANES), 0) >= _iota((LANES, LANES), 1), 1.0, 0.0).astype(BF16)
    lo_half = (_iota((LANES, k.shape[1]), 1) & (LANES - 1)) < DH_B
    lf_heads = jnp.where(_iota(lf.shape, 1) < H_B, lf, 0.0)
    for sub in range(tm // LANES):
        rs = slice(sub * LANES, (sub + 1) * LANES)
        c = _unpack3_sum(_dot(tril, _pack3(lf_heads[rs, :]))) + carry[0:1, :]
        carry[0:1, :] = c[LANES - 1:LANES, :]
        bias = _dot(_pack3(c * LOG2E), place_ref[...])
        k0_ref[0, rs, :] = jnp.where(lo_half, k[rs, :], bias).astype(BF16)
        k1_ref[0, rs, :] = jnp.where(lo_half, bias, k[rs, :]).astype(BF16)

    kt_ref[0] = k.T
    lft_ref[0] = lf.T[:H_B, :]
    vt = jnp.where(valid, _dot(xs, wv_ref[...]), 0.0).T
    vt_ref[0] = vt
    vtb_ref[0] = vt.astype(BF16)
    xq = (hr * gb_ref[...]).astype(BF16)
    q = _head_rmsnorm64(_dot(xq, wq_ref[...])) * qg_ref[...]
    qt_ref[0] = jnp.where(valid, q, 0.0).T.astype(BF16)
    og_ref[0] = _dot(xq, wog_ref[...]).astype(BF16)


def _kvq_proj_t(h, gkv, wk, wv, wf, bf, kg, gb, wq, wog, qg, batch, seq):
    d = h.shape[1]
    hd = wk.shape[1]
    tm = KVQ_T_TILE
    nt = pl.cdiv(seq, tm)
    tpad = nt * tm
    rows = lambda b, i: (b, i, 0)
    cols = lambda b, i: (b, 0, i)
    place = _placement()
    return pl.pallas_call(
        functools.partial(_kvq_t_body, seq=seq),
        grid=(batch, nt),
        in_specs=[pl.BlockSpec((1, tm, d), rows)] + [_const_spec(w.shape) for w in
                                                      (gkv, wk, wv, wf, bf, kg, gb, wq, wog, qg, place)],
        out_specs=[pl.BlockSpec((1, hd, tm), cols), pl.BlockSpec((1, hd, tm), cols),
                   pl.BlockSpec((1, H_B, tm), cols), pl.BlockSpec((1, tm, hd), rows),
                   pl.BlockSpec((1, tm, hd), rows), pl.BlockSpec((1, hd, tm), cols),
                   pl.BlockSpec((1, hd, tm), cols), pl.BlockSpec((1, tm, wog.shape[1]), rows)],
        out_shape=[jax.ShapeDtypeStruct((batch, hd, seq), F32), jax.ShapeDtypeStruct((batch, hd, seq), F32),
                   jax.ShapeDtypeStruct((batch, H_B, seq), F32), jax.ShapeDtypeStruct((batch, tpad, hd), BF16),
                   jax.ShapeDtypeStruct((batch, tpad, hd), BF16), jax.ShapeDtypeStruct((batch, hd, tpad), BF16),
                   jax.ShapeDtypeStruct((batch, hd, tpad), BF16),
                   jax.ShapeDtypeStruct((batch, seq, wog.shape[1]), BF16)],
        scratch_shapes=[pltpu.VMEM((SUBLANES, LANES), F32)],
        compiler_params=_cparams("parallel", "arbitrary"),
        name="kvq_proj_t",
    )(h.reshape(batch, seq, d), gkv, wk, wv, wf, bf, kg, gb, wq, wog, qg, place)


def _kvq_proj(h, gkv, wk, wv, wf, bf, kg, gb, wq, wog, qg, tm):
    n, d = h.shape
    hd = wk.shape[1]
    row = lambda i: (i, 0)
    wide = pl.BlockSpec((tm, hd), row)
    return pl.pallas_call(
        _kvq_body,
        grid=(pl.cdiv(n, tm),),
        in_specs=[pl.BlockSpec((tm, d), row)] + [_const_spec(w.shape) for w in
                                                 (gkv, wk, wv, wf, bf, kg, gb, wq, wog, qg)],
        out_specs=[wide, wide, pl.BlockSpec((tm, H_B), row), wide, wide, wide,
                   pl.BlockSpec((tm, wog.shape[1]), row)],
        out_shape=[jax.ShapeDtypeStruct((n, hd), F32), jax.ShapeDtypeStruct((n, hd), F32),
                   jax.ShapeDtypeStruct((n, H_B), F32), jax.ShapeDtypeStruct((n, hd), BF16),
                   jax.ShapeDtypeStruct((n, hd), BF16), jax.ShapeDtypeStruct((n, hd), BF16),
                   jax.ShapeDtypeStruct((n, wog.shape[1]), BF16)],
        compiler_params=_cparams("parallel"),
        name="kvq_proj",
    )(h, gkv, wk, wv, wf, bf, kg, gb, wq, wog, qg)


def _col_to_row(col):
    n = col.shape[0]
    eye = _iota((n, n), 0) == _iota((n, n), 1)
    return jnp.sum(jnp.where(eye, col, 0.0), axis=0, keepdims=True)


def _row_to_col(row):
    n = row.shape[1]
    eye = _iota((n, n), 0) == _iota((n, n), 1)
    return jnp.sum(jnp.where(eye, row, 0.0), axis=1, keepdims=True)


def _mlstm_chunk_heads(qk, v, li, lf, cfull, m):
    heads = range(len(qk))
    L = qk[0].shape[0]
    tril = _iota((L, L), 0) >= _iota((L, L), 1)
    lane = _iota((L, LANES), 1)
    ones_col = jnp.where(lane == 0, 1.0, 0.0)

    q_lo = [jnp.where(lane < DK_A, qk[i], 0.0).astype(BF16) for i in heads]
    kq = [pltpu.roll(qk[i], DK_A, 1) for i in heads]
    vaug = [jnp.concatenate([v[i], ones_col], axis=1).astype(BF16) for i in heads]
    qk_t = [_dot_nt(q_lo[i], kq[i].astype(BF16)) for i in heads]
    q_c = [_dot(q_lo[i], cfull[i].astype(BF16)) for i in heads]

    lf_row = [_col_to_row(lf[i]) for i in heads]
    li_row = [_col_to_row(li[i]) for i in heads]
    b = [jnp.sum(jnp.where(tril, lf_row[i], 0.0), axis=1, keepdims=True) for i in heads]
    b_row = [_col_to_row(b[i]) for i in heads]
    dmat = [jnp.where(tril, b[i] - b_row[i] + li_row[i], -jnp.inf) for i in heads]
    dmax = [jnp.max(dmat[i], axis=1, keepdims=True) for i in heads]

    b_end = [b[i][L - 1:L, :] for i in heads]
    g = [b_end[i] - b[i] + li[i] for i in heads]
    m_new = [jnp.maximum(b_end[i] + m[i], jnp.max(g[i], axis=0, keepdims=True)) for i in heads]
    w_c = [jnp.exp(b_end[i] + m[i] - m_new[i]) for i in heads]
    upd = [_dot_tn((jnp.exp(g[i] - m_new[i]) * kq[i]).astype(BF16), vaug[i]) for i in heads]
    keep = _iota(cfull[0].shape, 0) < DK_A
    cfull_new = [jnp.where(keep, w_c[i] * cfull[i] + upd[i], 0.0) for i in heads]

    inter = [b[i] + m[i] for i in heads]
    m_t = [jnp.maximum(inter[i], dmax[i]) for i in heads]
    s = [(qk_t[i] * jnp.exp(dmat[i] - m_t[i])).astype(BF16) for i in heads]
    tot = [jnp.exp(inter[i] - m_t[i]) * q_c[i] + _dot(s[i], vaug[i]) for i in heads]
    h = [tot[i][:, :DV_A] / jnp.maximum(jnp.abs(tot[i][:, DV_A:DV_A + 1]), jnp.exp(-m_t[i])) for i in heads]
    return h, cfull_new, m_new


def _mlstm_head_out(h, og, gain):
    return h * _rms_scale(h) * gain * _sigmoid(og)


def _mlstm_prompt_body(qk_ref, v_ref, og_ref, gt_ref, mhg_ref, hg_ref, c_ref, n_ref, m_ref,
                       cst, mst, *, seq, hb):
    grp = pl.program_id(1)
    ch = MLSTM_CHUNK
    nfull, tail = seq // ch, seq % ch
    cst[...] = jnp.zeros(cst.shape, F32)
    mst[...] = jnp.zeros(mst.shape, F32)

    def chunk(r0, first_valid):
        gt = gt_ref[0, pl.ds(r0, ch), :]
        lane = _iota(gt.shape, 1)
        rowi = _iota((ch, 1), 0)
        li, lf, qk, v = [], [], [], []
        for hh in range(hb):
            head = grp * hb + hh
            li_h = jnp.sum(jnp.where(lane == head, gt, 0.0), axis=1, keepdims=True)
            lf_h = jnp.sum(jnp.where(lane == head + H_A, gt, 0.0), axis=1, keepdims=True)
            if first_valid:
                li_h = jnp.where(rowi >= first_valid, li_h, MASKED_GATE)
                lf_h = jnp.where(rowi >= first_valid, lf_h, 0.0)
            sl = slice(hh * LANES, (hh + 1) * LANES)
            li.append(li_h)
            lf.append(lf_h)
            qk.append(qk_ref[0, pl.ds(r0, ch), sl].astype(F32))
            v.append(v_ref[0, pl.ds(r0, ch), sl].astype(F32))
        hs, cnew, mnew = _mlstm_chunk_heads(qk, v, li, lf, [cst[hh] for hh in range(hb)],
                                            [mst[hh, 0:1, 0:1] for hh in range(hb)])
        outs = []
        for hh in range(hb):
            cst[hh] = cnew[hh]
            mst[hh] = jnp.broadcast_to(mnew[hh], mst.shape[1:])
            og = og_ref[0, pl.ds(r0, ch), hh * LANES:(hh + 1) * LANES].astype(F32)
            outs.append(_mlstm_head_out(hs[hh], og, mhg_ref[hh]))
        return jnp.concatenate(outs, axis=1).astype(BF16)

    def loop_body(j, carry):
        r0 = pl.multiple_of(j * ch, ch)
        hg_ref[0, pl.ds(r0, ch), :] = chunk(r0, 0)
        return carry

    lax.fori_loop(0, nfull, loop_body, 0)
    if tail:
        out = chunk(seq - ch, ch - tail)
        hg_ref[0, seq - tail:seq, :] = out[ch - tail:, :]

    for hh in range(hb):
        cfull = cst[hh]
        c_ref[0, hh] = cfull[:DK_A, :DV_A]
        n_ref[0, hh] = _col_to_row(cfull[:DK_A, DV_A:DV_A + 1])
        m_ref[0, hh] = mst[hh, 0:1, 0:1]


def _mlstm_prompt(qk, v, og, gt, mhg, batch, seq):
    hb = MLSTM_HEADS_PER_STEP
    w = hb * LANES
    qk3, v3, og3 = (a.reshape(batch, seq, a.shape[-1]) for a in (qk, v, og))
    gt3 = gt.reshape(batch, seq, LANES)
    seq_blk = lambda b, g: (b, 0, g)
    hg, c, n, m = pl.pallas_call(
        functools.partial(_mlstm_prompt_body, seq=seq, hb=hb),
        grid=(batch, H_A // hb),
        in_specs=[pl.BlockSpec((1, seq, w), seq_blk), pl.BlockSpec((1, seq, w), seq_blk),
                  pl.BlockSpec((1, seq, w), seq_blk),
                  pl.BlockSpec((1, seq, LANES), lambda b, g: (b, 0, 0)),
                  pl.BlockSpec((hb, 1, DV_A), lambda b, g: (g, 0, 0))],
        out_specs=[pl.BlockSpec((1, seq, w), seq_blk),
                   pl.BlockSpec((1, hb, DK_A, DV_A), lambda b, g: (b, g, 0, 0)),
                   pl.BlockSpec((1, hb, 1, DK_A), lambda b, g: (b, g, 0, 0)),
                   pl.BlockSpec((1, hb, 1, 1), lambda b, g: (b, g, 0, 0))],
        out_shape=[jax.ShapeDtypeStruct((batch, seq, H_A * DV_A), BF16),
                   jax.ShapeDtypeStruct((batch, H_A, DK_A, DV_A), F32),
                   jax.ShapeDtypeStruct((batch, H_A, 1, DK_A), F32),
                   jax.ShapeDtypeStruct((batch, H_A, 1, 1), F32)],
        scratch_shapes=[pltpu.VMEM((hb, LANES, 2 * LANES), F32), pltpu.VMEM((hb, SUBLANES, LANES), F32)],
        compiler_params=_cparams("parallel", "arbitrary"),
        name="mlstm_prompt",
    )(qk3, v3, og3, gt3, mhg.reshape(H_A, 1, DV_A))
    return (hg.reshape(batch * seq, H_A * DV_A), c, n.reshape(batch, H_A, DK_A),
            m.reshape(batch, H_A))


def _mlstm_sample_body(qk_ref, v_ref, og_ref, gt_ref, mhg_ref, c0_ref, n0_ref, m0_ref,
                       hg_ref, c_ref, n_ref, m_ref, *, steps):
    ch = pl.cdiv(steps, SAMPLE_CHUNK_ALIGN) * SAMPLE_CHUNK_ALIGN
    pad = jnp.zeros((ch - steps, LANES), F32)
    padded = (lambda a: jnp.concatenate([a, pad], axis=0)) if ch > steps else (lambda a: a)
    rowi = _iota((ch, 1), 0)
    lane1 = _iota((1, LANES), 1)
    lane_c = _iota((DK_A, LANES), 1)
    nseq = qk_ref.shape[0]
    li, lf, qk, v, cfull, m0 = [], [], [], [], [], []
    for sq in range(nseq):
        gt = padded(gt_ref[sq])
        lane = _iota(gt.shape, 1)
        for hh in range(H_A):
            sl = slice(hh * LANES, (hh + 1) * LANES)
            li_h = jnp.sum(jnp.where(lane == hh, gt, 0.0), axis=1, keepdims=True)
            lf_h = jnp.sum(jnp.where(lane == hh + H_A, gt, 0.0), axis=1, keepdims=True)
            li.append(jnp.where(rowi < steps, li_h, MASKED_GATE))
            lf.append(jnp.where(rowi < steps, lf_h, 0.0))
            qk.append(padded(qk_ref[sq, :, sl].astype(F32)))
            v.append(padded(v_ref[sq, :, sl].astype(F32)))
            ncol = _row_to_col(n0_ref[sq, hh:hh + 1, :])
            top = jnp.concatenate([c0_ref[sq, hh], jnp.where(lane_c == 0, ncol, 0.0)], axis=1)
            cfull.append(jnp.concatenate([top, jnp.zeros((LANES - DK_A, 2 * LANES), F32)], axis=0))
            m0.append(m0_ref[sq, :, hh:hh + 1])
    hs, cnew, mnew = _mlstm_chunk_heads(qk, v, li, lf, cfull, m0)
    for sq in range(nseq):
        m_out = jnp.zeros((1, LANES), F32)
        outs = []
        for hh in range(H_A):
            i = sq * H_A + hh
            og = og_ref[sq, :, hh * LANES:(hh + 1) * LANES].astype(F32)
            outs.append(_mlstm_head_out(hs[i][:steps, :], og, mhg_ref[hh]))
            c_ref[sq, hh] = cnew[i][:DK_A, :DV_A]
            n_ref[sq, hh:hh + 1, :] = _col_to_row(cnew[i][:DK_A, DV_A:DV_A + 1])
            m_out = jnp.where(lane1 == hh, mnew[i], m_out)
        hg_ref[sq] = jnp.concatenate(outs, axis=1).astype(BF16)
        m_ref[sq] = m_out[:, :H_A]


def _mlstm_sample(qk, v, og, gt, mhg, c0, n0, m0, batch, steps):
    wide = H_A * LANES
    ns = MLSTM_SAMPLE_SEQS if batch % MLSTM_SAMPLE_SEQS == 0 else 1
    blk3 = lambda b: (b, 0, 0)
    hg, c, n, m = pl.pallas_call(
        functools.partial(_mlstm_sample_body, steps=steps),
        grid=(batch // ns,),
        in_specs=[pl.BlockSpec((ns, steps, wide), blk3), pl.BlockSpec((ns, steps, wide), blk3),
                  pl.BlockSpec((ns, steps, wide), blk3), pl.BlockSpec((ns, steps, LANES), blk3),
                  _const_spec((H_A, 1, DV_A)),
                  pl.BlockSpec((ns, H_A, DK_A, DV_A), lambda b: (b, 0, 0, 0)),
                  pl.BlockSpec((ns, H_A, DK_A), blk3), pl.BlockSpec((ns, 1, H_A), blk3)],
        out_specs=[pl.BlockSpec((ns, steps, wide), blk3),
                   pl.BlockSpec((ns, H_A, DK_A, DV_A), lambda b: (b, 0, 0, 0)),
                   pl.BlockSpec((ns, H_A, DK_A), blk3), pl.BlockSpec((ns, 1, H_A), blk3)],
        out_shape=[jax.ShapeDtypeStruct((batch, steps, wide), BF16),
                   jax.ShapeDtypeStruct((batch, H_A, DK_A, DV_A), F32),
                   jax.ShapeDtypeStruct((batch, H_A, DK_A), F32),
                   jax.ShapeDtypeStruct((batch, 1, H_A), F32)],
        compiler_params=_cparams("parallel"),
        name="mlstm_sample",
    )(qk.reshape(batch, steps, wide), v.reshape(batch, steps, wide), og.reshape(batch, steps, wide),
      gt.reshape(batch, steps, LANES), mhg.reshape(H_A, 1, DV_A), c0, n0, m0.reshape(batch, 1, H_A))
    return hg.reshape(batch * steps, wide), c, n, m.reshape(batch, H_A)


def _rows_to_lanes(x16, staging_ref):
    staging_ref[...] = jnp.zeros(staging_ref.shape, F32)
    staging_ref[:, 0:x16.shape[1]] = x16
    return staging_ref[...].T[0:x16.shape[1], :]


def _attn_prompt_body(qt_ref, k0_ref, k1_ref, vt_ref, og_ref, o_ref, s_scr, p_scr, acc_scr, qa_scr,
                      mask_scr, *, seq):
    tq, tk = ATTN_TQ, ATTN_TK
    tpad = qt_ref.shape[2]
    nfull = seq // tq
    nh = ATTN_HEADS
    heads = range(nh)

    def keys(h, rows):
        pair = slice((h // 2) * LANES, (h // 2 + 1) * LANES)
        return (k0_ref if h % 2 == 0 else k1_ref)[0, rows, pair]

    def augmented_queries(qt):
        out = []
        for pp in range(nh // 2):
            blk = qt[pp * LANES:(pp + 1) * LANES, :]
            row = _iota(blk.shape, 0)
            out.append(jnp.where(row < DH_B, blk, jnp.where(row < DH_B + N_SPLIT, -1.0, 0.0)).astype(BF16))
            out.append(jnp.where(row >= DH_B, blk, jnp.where(row < N_SPLIT, -1.0, 0.0)).astype(BF16))
        return out

    def with_ones(vt):
        return jnp.concatenate([vt, jnp.ones((ROWSUM_ROWS, vt.shape[1]), BF16)], axis=0)

    def tail_tile(q0, width, rows_out):
        qa = augmented_queries(qt_ref[0, :, pl.ds(q0, width)].astype(F32))
        causal = _iota((tpad, width), 0) <= q0 + _iota((tpad, width), 1)
        outs = []
        for hh in heads:
            s = jnp.where(causal, _dot(keys(hh, slice(None)), qa[hh]), -jnp.inf)
            p = jnp.exp2(s - jnp.max(s, axis=0, keepdims=True)).astype(BF16)
            full = _dot(with_ones(vt_ref[0, hh * DH_B:(hh + 1) * DH_B, :]), p)
            outs.append(full[:DH_B, :] / full[DH_B:DH_B + 1, :])
        out = jnp.concatenate(outs, axis=0).T[:rows_out, :]
        gate = _sigmoid(og_ref[0, pl.ds(q0, rows_out), :].astype(F32))
        o_ref[0, pl.ds(q0, rows_out), :] = (out * gate).astype(BF16)

    per = tq // tk
    for d in range(per):
        key = d * tk + _iota((tk, tq), 0)
        mask_scr[d] = jnp.where(key <= _iota((tk, tq), 1), 0.0, -jnp.inf)
    p_scr[1] = jnp.zeros(p_scr.shape[1:], BF16)
    acc_scr[...] = jnp.zeros(acc_scr.shape, F32)

    def first_lane(diag_idx):
        return 0 if diag_idx is None else diag_idx * tk

    def left_pad(x, lo, fill):
        return x if lo == 0 else jnp.concatenate([jnp.full((x.shape[0], lo), fill, x.dtype), x], axis=1)

    def value_product(hh, k0, slot, lo):
        vt = with_ones(vt_ref[0, hh * DH_B:(hh + 1) * DH_B, pl.ds(k0, tk)])
        return left_pad(_dot(vt, p_scr[slot, hh, :, lo:]), lo, 0.0)

    def pipe_step(ms, kidx, slot, diag_idx, issue_next):
        k0 = pl.multiple_of(kidx * tk, tk)
        lo = first_lane(diag_idx)
        lo_prev = first_lane(diag_idx - 1 if diag_idx else None)
        lo_next = first_lane(None if diag_idx is None else diag_idx + 1)
        if issue_next:
            for hh in heads:
                s_scr[1 - slot, hh, :, lo_next:] = _dot(keys(hh, pl.ds(k0 + tk, tk)), qa_scr[hh, :, lo_next:])
        kprev = pl.multiple_of(jnp.maximum(kidx - 1, 0) * tk, tk)
        pvs = [value_product(hh, kprev, 1 - slot, lo_prev) for hh in heads]

        def scores(hh):
            s = s_scr[slot, hh, :, lo:]
            return s if diag_idx is None else s + mask_scr[diag_idx, :, lo:]

        new = [jnp.maximum(ms[hh][:, lo:], jnp.max(scores(hh), axis=0, keepdims=True)) for hh in heads]
        for hh in heads:
            p_scr[slot, hh, :, lo:] = jnp.exp2(scores(hh) - new[hh]).astype(BF16)
        new = [n if lo == 0 else jnp.concatenate([ms[hh][:, :lo], n], axis=1) for hh, n in zip(heads, new)]
        for hh in heads:
            acc_scr[hh] = (acc_scr[hh] + pvs[hh]) * jnp.exp2(ms[hh] - new[hh])
        return tuple(new)

    def q_tile_pipelined(i):
        q0 = pl.multiple_of(i * tq, tq)
        qa = augmented_queries(qt_ref[0, :, pl.ds(q0, tq)].astype(F32))
        for hh in heads:
            qa_scr[hh] = qa[hh]
        for hh in heads:
            s_scr[0, hh] = _dot(keys(hh, pl.ds(0, tk)), qa_scr[hh])
        ms = tuple(jnp.full((1, tq), NEG_INIT, F32) for _ in heads)

        def group(jj, st):
            for d in range(per):
                st = pipe_step(st, jj * per + d, d % 2, None, True)
            return st

        ms = lax.fori_loop(0, i, group, ms)
        for d in range(per):
            ms = pipe_step(ms, i * per + d, d % 2, d, d < per - 1)
        last = (per - 1) % 2
        klast = pl.multiple_of((i * per + per - 1) * tk, tk)
        outs = []
        for hh in heads:
            full = acc_scr[hh] + value_product(hh, klast, last, first_lane(per - 1))
            outs.append(full[:DH_B, :] / full[DH_B:DH_B + 1, :])
        out = jnp.concatenate(outs, axis=0).T
        gate = _sigmoid(og_ref[0, pl.ds(q0, tq), :].astype(F32))
        o_ref[0, pl.ds(q0, tq), :] = (out * gate).astype(BF16)

    def qbody(i, _):
        q_tile_pipelined(i)
        return 0

    lax.fori_loop(0, nfull, qbody, 0)
    if seq > nfull * tq:
        q0 = nfull * tq
        tail_tile(q0, tpad - q0, seq - q0)


def _attn_prompt(qt, k0, k1, vt, og, seq):
    batch, wide, tpad = qt.shape
    tail_w = tpad - (seq // ATTN_TQ) * ATTN_TQ
    assert ATTN_TQ % (2 * ATTN_TK) == 0 and 0 <= tail_w and tail_w % LANES == 0
    rows = lambda b, p: (b, 0, p)
    cols = lambda b, p: (b, p, 0)
    nh = ATTN_HEADS
    w = nh * DH_B
    out = pl.pallas_call(
        functools.partial(_attn_prompt_body, seq=seq),
        grid=(batch, H_B // nh),
        in_specs=[pl.BlockSpec((1, w, tpad), cols), pl.BlockSpec((1, tpad, w), rows),
                  pl.BlockSpec((1, tpad, w), rows), pl.BlockSpec((1, w, tpad), cols),
                  pl.BlockSpec((1, seq, w), rows)],
        out_specs=pl.BlockSpec((1, seq, w), rows),
        out_shape=jax.ShapeDtypeStruct((batch, seq, wide), BF16),
        scratch_shapes=[pltpu.VMEM((2, nh, ATTN_TK, ATTN_TQ), F32),
                        pltpu.VMEM((2, nh, ATTN_TK, ATTN_TQ), BF16),
                        pltpu.VMEM((nh, DH_B + ROWSUM_ROWS, ATTN_TQ), F32),
                        pltpu.VMEM((nh, LANES, ATTN_TQ), BF16),
                        pltpu.VMEM((ATTN_TQ // ATTN_TK, ATTN_TK, ATTN_TQ), F32)],
        compiler_params=_cparams("parallel", "parallel"),
        name="attn_prompt",
    )(qt, k0, k1, vt, og)
    return out.reshape(batch * seq, wide)


def _attn_sample_body(pt_ref, q_ref, kn_ref, vn_ref, lfn_ref, og_ref, *rest, steps, npages):
    k_refs, v_refs, lf_refs = rest[:npages], rest[npages:2 * npages], rest[2 * npages:3 * npages]
    o_ref, stage = rest[3 * npages], rest[3 * npages + 1]
    rows = H_B * steps
    wide = H_B * DH_B
    u = _iota((LANES, LANES), 0)
    s_ = _iota((LANES, LANES), 1)

    q = q_ref[0].astype(F32)
    qrep = jnp.concatenate([jnp.broadcast_to(q[t:t + 1, :], (H_B, wide)) for t in range(steps)], axis=0)
    diag = _iota((rows, wide), 0) % H_B == _iota((rows, wide), 1) // DH_B
    qbd = jnp.where(diag, qrep, 0.0).astype(BF16)

    lf_all = jnp.concatenate([r[0] for r in lf_refs], axis=0)
    later_and_ones = jnp.concatenate([jnp.where(u > s_, 1.0, 0.0), jnp.ones((LANES, LANES), F32)],
                                     axis=1).astype(BF16)
    wt = _dot_by_01(lf_all, later_and_ones)
    pr = _iota((npages * H_B, npages * H_B), 0)
    pc = _iota((npages * H_B, npages * H_B), 1)
    later_pages = jnp.where((pc % H_B == pr % H_B) & (pc // H_B > pr // H_B), 1.0, 0.0).astype(BF16)
    hi, mid, lo = _split3(wt[:, LANES:])
    rsum = wt[:, :LANES] + _dot(later_pages, hi) + _dot(later_pages, mid) + _dot(later_pages, lo)
    bias_past = jnp.concatenate(
        [jnp.concatenate([rsum[r * H_B:(r + 1) * H_B, :]] * steps, axis=0) for r in range(npages)], axis=1)

    kcat = jnp.concatenate([r[0].astype(BF16) for r in k_refs], axis=1)
    s_past = _dot(qbd, kcat) + bias_past

    zpad = jnp.zeros((LANES - steps, wide), F32)
    kn = jnp.concatenate([kn_ref[0].astype(F32), zpad], axis=0).astype(BF16)
    vn = jnp.concatenate([vn_ref[0].astype(F32), zpad], axis=0).astype(BF16)
    lfn = jnp.concatenate([lfn_ref[0], jnp.zeros((LANES - steps, H_B), F32)], axis=0)
    incl = jnp.where(u <= s_, 1.0, 0.0).astype(BF16)
    cnew = _dot_by_01(_rows_to_lanes(lfn, stage), incl)
    key = _iota((rows, LANES), 1)
    qry = _iota((rows, LANES), 0) // H_B
    bias_new = jnp.where(key <= qry, -jnp.concatenate([cnew] * steps, axis=0), -jnp.inf)
    s_new = _dot_nt(qbd, kn) + bias_new

    m = jnp.maximum(jnp.max(s_past, axis=1, keepdims=True), jnp.max(s_new, axis=1, keepdims=True))
    p_past = jnp.exp(s_past - m)
    p_new = jnp.exp(s_new - m)
    l = jnp.sum(p_past, axis=1, keepdims=True) + jnp.sum(p_new, axis=1, keepdims=True)
    vcat = jnp.concatenate([r[0].astype(BF16) for r in v_refs], axis=1)
    acc = _dot_nt(p_past.astype(BF16), vcat) + _dot(p_new.astype(BF16), vn)
    full = jnp.where(diag, acc / l, 0.0)
    out = jnp.concatenate([jnp.sum(full[t * H_B:(t + 1) * H_B, :], axis=0, keepdims=True)
                           for t in range(steps)], axis=0)
    o_ref[0] = (out * _sigmoid(og_ref[0].astype(F32))).astype(BF16)


def _attn_sample(q, kn, vn, lfn, og, cache_k, cache_v, cache_logf, page_table, batch, steps):
    wide = H_B * DH_B
    n_phys, page = cache_k.shape[0], cache_k.shape[1]
    npages = page_table.shape[1]
    assert page == LANES
    ck = jnp.transpose(cache_k, (0, 2, 3, 1)).reshape(n_phys, wide, page)
    cv = jnp.transpose(cache_v, (0, 2, 3, 1)).reshape(n_phys, wide, page)
    clf = jnp.transpose(cache_logf, (0, 2, 1))
    tok = lambda b, pt: (b, 0, 0)
    page_of = lambda r: (lambda b, pt: (pt[b, r], 0, 0))
    out = pl.pallas_call(
        functools.partial(_attn_sample_body, steps=steps, npages=npages),
        grid_spec=pltpu.PrefetchScalarGridSpec(
            num_scalar_prefetch=1,
            grid=(batch,),
            in_specs=[pl.BlockSpec((1, steps, wide), tok), pl.BlockSpec((1, steps, wide), tok),
                      pl.BlockSpec((1, steps, wide), tok), pl.BlockSpec((1, steps, H_B), tok),
                      pl.BlockSpec((1, steps, wide), tok)]
            + [pl.BlockSpec((1, wide, page), page_of(r)) for r in range(npages)]
            + [pl.BlockSpec((1, wide, page), page_of(r)) for r in range(npages)]
            + [pl.BlockSpec((1, H_B, page), page_of(r)) for r in range(npages)],
            out_specs=pl.BlockSpec((1, steps, wide), tok),
            scratch_shapes=[pltpu.VMEM((LANES, LANES), F32)]),
        out_shape=jax.ShapeDtypeStruct((batch, steps, wide), BF16),
        compiler_params=_cparams("parallel"),
        name="attn_sample",
    )(page_table, q.reshape(batch, steps, wide), kn.reshape(batch, steps, wide),
      vn.reshape(batch, steps, wide), lfn.reshape(batch, steps, H_B), og.reshape(batch, steps, wide),
      *([ck] * npages), *([cv] * npages), *([clf] * npages))
    return out.reshape(batch * steps, wide)


def _prep_params(norm_a, w_in_a, b_ig_a, b_fg_a, mh_norm_a, w_out_a, norm_kv, w_kvf, b_fg_b,
                 k_norm_b, norm_b, w_qo_b, q_norm_b, w_out_b, norm_ffn, w_gate_up, w_down,
                 norm_final):
    d = w_in_a.shape[1]
    hk, hv, hd = H_A * DK_A, H_A * DV_A, H_B * DH_B
    w_in = w_in_a[0]
    wq = w_in[:, :hk].reshape(d, H_A, DK_A)
    wk = w_in[:, hk:2 * hk].reshape(d, H_A, DK_A)
    row = lambda a: a.reshape(1, -1).astype(F32)
    pad_cols = lambda a: jnp.pad(a, ((0, 0), (0, LANES - a.shape[1])))
    lane = jnp.arange(H_A * LANES) % LANES
    return dict(
        norm_a=row(norm_a[0]),
        wqk=jnp.concatenate([wq, wk], axis=2).reshape(d, H_A * LANES).astype(BF16),
        wv=w_in[:, 2 * hk:2 * hk + hv].astype(BF16),
        wog=w_in[:, 2 * hk + hv:2 * hk + 2 * hv].astype(BF16),
        wg=pad_cols(w_in[:, 2 * hk + 2 * hv:]).astype(BF16),
        bg=pad_cols(jnp.concatenate([b_ig_a[0], b_fg_a[0]]).reshape(1, -1).astype(F32)),
        qs=jnp.where(lane < DK_A, DK_A ** -0.5, 1.0).reshape(1, -1).astype(F32),
        mhg=mh_norm_a[0].astype(F32),
        wo_a=w_out_a[0].astype(BF16),
        gkv=row(norm_kv),
        wk=w_kvf[:, :hd].astype(BF16),
        wvs=w_kvf[:, hd:2 * hd].astype(BF16),
        wf=pad_cols(w_kvf[:, 2 * hd:]).astype(BF16),
        bf=pad_cols(b_fg_b.reshape(1, -1).astype(F32)),
        kg=row(jnp.tile(k_norm_b, H_B)),
        gb=row(norm_b[0]),
        wq=w_qo_b[0][:, :hd].astype(BF16),
        wog_b=w_qo_b[0][:, hd:].astype(BF16),
        qg=row(jnp.tile(q_norm_b[0], H_B)) * DH_B ** -0.5,
        wo_b=w_out_b[0].astype(BF16),
        gf=[row(norm_ffn[l]) for l in range(2)],
        wgu=[w_gate_up[l].astype(BF16) for l in range(2)],
        wd=[w_down[l].astype(BF16) for l in range(2)],
        gfin=row(norm_final),
    )


def _layer0(h, p, tm, mlstm):
    qk, v, og, gt = _proj_in(h, p["norm_a"], p["wqk"], p["wv"], p["wog"], p["wg"], p["bg"], p["qs"], tm)
    hg, c, n, m = mlstm(qk, v, og, gt)
    h2 = _mix_ffn(hg, h, p["wo_a"], p["gf"][0], p["wgu"][0], p["wd"][0], p["gfin"], tm, False)
    return h2, c, n, m


def _shared_and_q(h2, p, tm):
    return _kvq_proj(h2, p["gkv"], p["wk"], p["wvs"], p["wf"], p["bf"], p["kg"], p["gb"],
                     p["wq"], p["wog_b"], p["qg"], tm)


def _layer1_tail(o, h2, p, tm, drop_lead=None):
    return _mix_ffn(o, h2, p["wo_b"], p["gf"][1], p["wgu"][1], p["wd"][1], p["gfin"], tm, True, drop_lead)


def kernel(x_prompt, x_sample, state_C, state_n, state_m, cache_k, cache_v, cache_logf, page_table,
           meta_tokens, norm_a, w_in_a, b_ig_a, b_fg_a, mh_norm_a, w_out_a, norm_kv, w_kvf, b_fg_b,
           k_norm_b, norm_b, w_qo_b, q_norm_b, w_out_b, norm_ffn, w_gate_up, w_down, norm_final):
    assert w_in_a.shape[0] == 1 and w_qo_b.shape[0] == 1 and norm_ffn.shape[0] == 2
    p = _prep_params(norm_a, w_in_a, b_ig_a, b_fg_a, mh_norm_a, w_out_a, norm_kv, w_kvf, b_fg_b,
                     k_norm_b, norm_b, w_qo_b, q_norm_b, w_out_b, norm_ffn, w_gate_up, w_down,
                     norm_final)
    bp, sp, d = x_prompt.shape
    bs, ss, _ = x_sample.shape
    tp = sp + N_META
    hd = H_B * DH_B
    tm = 512

    meta = jnp.broadcast_to(meta_tokens[None].astype(F32), (bp, N_META, d))
    h0 = jnp.concatenate([meta, x_prompt], axis=1).reshape(bp * tp, d)
    h2, p_c, p_n, p_m = _layer0(h0, p, tm, functools.partial(_mlstm_prompt, mhg=p["mhg"], batch=bp, seq=tp))
    kt, vt, lft, k0, k1, vtb, qt, og = _kvq_proj_t(
        h2, p["gkv"], p["wk"], p["wvs"], p["wf"], p["bf"], p["kg"], p["gb"], p["wq"], p["wog_b"],
        p["qg"] * LOG2E, bp, tp)
    o = _attn_prompt(qt, k0, k1, vtb, og, tp)
    y_prompt = _layer1_tail(o, h2, p, tm, drop_lead=(bp, tp, N_META)).reshape(bp, sp, d)
    p_k = jnp.transpose(kt.reshape(bp, H_B, DH_B, tp), (0, 3, 1, 2))
    p_v = jnp.transpose(vt.reshape(bp, H_B, DH_B, tp), (0, 3, 1, 2))
    p_lf = jnp.transpose(lft, (0, 2, 1))

    hs0 = x_sample.reshape(bs * ss, d)
    hs2, s_c, s_n, s_m = _layer0(
        hs0, p, tm, functools.partial(_mlstm_sample, mhg=p["mhg"], c0=state_C[0], n0=state_n[0],
                                      m0=state_m[0], batch=bs, steps=ss))
    ks, vs, lfs, kbs, vbs, qbs, ogs = _shared_and_q(hs2, p, tm)
    os_ = _attn_sample(qbs, kbs, vbs, lfs, ogs, cache_k, cache_v, cache_logf, page_table, bs, ss)
    y_sample = _layer1_tail(os_, hs2, p, tm).reshape(bs, ss, d)

    return (y_prompt, y_sample, p_c[None], p_n[None], p_m[None], p_k, p_v, p_lf,
            s_c[None], s_n[None], s_m[None], ks.reshape(bs, ss, H_B, DH_B),
            vs.reshape(bs, ss, H_B, DH_B), lfs.reshape(bs, ss, H_B))
```

```python
import functools

import jax
import jax.numpy as jnp
from jax import lax
from jax.experimental import pallas as pl
from jax.experimental.pallas import tpu as pltpu

F32 = jnp.float32
BF16 = jnp.bfloat16

N_META = 16
H_A = 8
DK_A = 64
DV_A = 128
GATE_CAP = 15.0
H_B = 16
DH_B = 64
EPS = 1e-6

LANES = 128
SUBLANES = 8
VMEM_LIMIT_BYTES = 56 * 1024 * 1024

MLSTM_CHUNK = 128
MLSTM_HEADS_PER_STEP = 4
MLSTM_SAMPLE_SEQS = 4
SAMPLE_CHUNK_ALIGN = 8
ATTN_TQ = 512
ATTN_TK = 256
ATTN_HEADS = 4
KVQ_T_TILE = 384
N_SPLIT = 3
ROWSUM_ROWS = 16
LOG2E = 1.4426950408889634
FF_CHUNK = 256
MASKED_GATE = -1e30
NEG_INIT = -1e30


def _cparams(*sem):
    return pltpu.CompilerParams(dimension_semantics=sem, vmem_limit_bytes=VMEM_LIMIT_BYTES)


def _const_spec(shape):
    nd = len(shape)
    return pl.BlockSpec(shape, lambda *_: (0,) * nd, pipeline_mode=pl.Buffered(1))


def _rms_scale(x):
    return lax.rsqrt(jnp.mean(x * x, axis=-1, keepdims=True) + EPS)


def _log_sigmoid(x):
    return jnp.minimum(x, 0.0) - jnp.log1p(jnp.exp(-jnp.abs(x)))


def _sigmoid(x):
    return 1.0 / (1.0 + jnp.exp(-x))


def _dot(a, b):
    return jnp.dot(a, b, preferred_element_type=F32)


def _dot_nt(a, b):
    return lax.dot_general(a, b, (((1,), (1,)), ((), ())), preferred_element_type=F32)


def _dot_tn(a, b):
    return lax.dot_general(a, b, (((0,), (0,)), ((), ())), preferred_element_type=F32)


def _split3(x):
    hi = x.astype(BF16)
    r1 = x - hi.astype(F32)
    mid = r1.astype(BF16)
    lo = (r1 - mid.astype(F32)).astype(BF16)
    return hi, mid, lo


def _dot_by_01(x, m01):
    hi, mid, lo = _split3(x)
    return _dot(hi, m01) + _dot(mid, m01) + _dot(lo, m01)


def _iota(shape, dim):
    return lax.broadcasted_iota(jnp.int32, shape, dim)


def _proj_in_body(x_ref, g_ref, wqk_ref, wv_ref, wog_ref, wg_ref, bg_ref, qs_ref,
                  qk_ref, v_ref, og_ref, gt_ref):
    x = x_ref[...]
    xn = (x * _rms_scale(x) * g_ref[...]).astype(BF16)
    qk_ref[...] = (_dot(xn, wqk_ref[...]) * qs_ref[...]).astype(BF16)
    v_ref[...] = _dot(xn, wv_ref[...]).astype(BF16)
    og_ref[...] = _dot(xn, wog_ref[...]).astype(BF16)
    z = _dot(xn, wg_ref[...]) + bg_ref[...]
    cap = GATE_CAP * jnp.tanh(z / GATE_CAP)
    lane = _iota(cap.shape, 1)
    gt_ref[...] = jnp.where(lane < H_A, cap, _log_sigmoid(cap))


def _proj_in(x, g, wqk, wv, wog, wg, bg, qs, tm):
    n, d = x.shape
    row = lambda i: (i, 0)
    return pl.pallas_call(
        _proj_in_body,
        grid=(pl.cdiv(n, tm),),
        in_specs=[pl.BlockSpec((tm, d), row), _const_spec(g.shape), _const_spec(wqk.shape),
                  _const_spec(wv.shape), _const_spec(wog.shape), _const_spec(wg.shape),
                  _const_spec(bg.shape), _const_spec(qs.shape)],
        out_specs=[pl.BlockSpec((tm, wqk.shape[1]), row), pl.BlockSpec((tm, wv.shape[1]), row),
                   pl.BlockSpec((tm, wog.shape[1]), row), pl.BlockSpec((tm, LANES), row)],
        out_shape=[jax.ShapeDtypeStruct((n, wqk.shape[1]), BF16),
                   jax.ShapeDtypeStruct((n, wv.shape[1]), BF16),
                   jax.ShapeDtypeStruct((n, wog.shape[1]), BF16),
                   jax.ShapeDtypeStruct((n, LANES), F32)],
        compiler_params=_cparams("parallel"),
        name="proj_in",
    )(x, g, wqk, wv, wog, wg, bg, qs)


def _mix_ffn_body(a_ref, h_ref, wo_ref, gf_ref, wgu_ref, wd_ref, gout_ref, o_ref, *, d_ff, final):
    h1 = h_ref[...] + _dot(a_ref[...], wo_ref[...])
    xn = (h1 * _rms_scale(h1) * gf_ref[...]).astype(BF16)
    acc = h1
    for c in range(d_ff // FF_CHUNK):
        lo = c * FF_CHUNK
        gate = _dot(xn, wgu_ref[:, lo:lo + FF_CHUNK])
        up = _dot(xn, wgu_ref[:, d_ff + lo:d_ff + lo + FF_CHUNK])
        act = (gate * _sigmoid(gate) * up).astype(BF16)
        acc = acc + _dot(act, wd_ref[lo:lo + FF_CHUNK, :])
    if final:
        acc = acc * _rms_scale(acc) * gout_ref[...]
    o_ref[...] = acc


def _mix_ffn(a, h, wo, gf, wgu, wd, gout, tm, final, drop_lead=None):
    n, d = h.shape
    d_ff = wd.shape[0]
    consts = [_const_spec(w.shape) for w in (wo, gf, wgu, wd, gout)]
    body = functools.partial(_mix_ffn_body, d_ff=d_ff, final=final)
    name = "mix_ffn_final" if final else "mix_ffn"
    if drop_lead is None:
        row = lambda i: (i, 0)
        return pl.pallas_call(
            body, grid=(pl.cdiv(n, tm),),
            in_specs=[pl.BlockSpec((tm, a.shape[1]), row), pl.BlockSpec((tm, d), row)] + consts,
            out_specs=pl.BlockSpec((tm, d), row),
            out_shape=jax.ShapeDtypeStruct((n, d), F32),
            compiler_params=_cparams("parallel"), name=name,
        )(a, h, wo, gf, wgu, wd, gout)
    batch, seq, lead = drop_lead
    nt = (seq - lead) // tm
    align = 2 * SUBLANES
    assert nt * tm == seq - lead and batch * seq == n and seq % align == 0 and lead % align == 0
    src = lambda b, i: (pl.multiple_of(b * seq + lead + i * tm, align), 0)
    return pl.pallas_call(
        body, grid=(batch, nt),
        in_specs=[pl.BlockSpec((pl.Element(tm), pl.Element(a.shape[1])), src),
                  pl.BlockSpec((pl.Element(tm), pl.Element(d)), src)] + consts,
        out_specs=pl.BlockSpec((tm, d), lambda b, i: (b * nt + i, 0)),
        out_shape=jax.ShapeDtypeStruct((batch * nt * tm, d), F32),
        compiler_params=_cparams("parallel", "parallel"), name=name,
    )(a, h, wo, gf, wgu, wd, gout)


def _head_rmsnorm64(x):
    outs = []
    for j in range(x.shape[1] // LANES):
        blk = x[:, j * LANES:(j + 1) * LANES]
        sq = blk * blk
        lane = _iota(blk.shape, 1)
        s_all = jnp.sum(sq, axis=1, keepdims=True)
        s_lo = jnp.sum(jnp.where(lane < DH_B, sq, 0.0), axis=1, keepdims=True)
        ms = jnp.where(lane < DH_B, s_lo, s_all - s_lo) / DH_B
        outs.append(blk * lax.rsqrt(ms + EPS))
    return jnp.concatenate(outs, axis=1)


def _kvq_compute(h, gkv_ref, wk_ref, wv_ref, wf_ref, bf_ref, kg_ref, gb_ref, wq_ref, wog_ref, qg_ref):
    hr = h * _rms_scale(h)
    xs = (hr * gkv_ref[...]).astype(BF16)
    k = _head_rmsnorm64(_dot(xs, wk_ref[...])) * kg_ref[...]
    v = _dot(xs, wv_ref[...])
    lf = _log_sigmoid(_dot(xs, wf_ref[...]) + bf_ref[...])
    xq = (hr * gb_ref[...]).astype(BF16)
    q = _head_rmsnorm64(_dot(xq, wq_ref[...])) * qg_ref[...]
    og = _dot(xq, wog_ref[...])
    return k, v, lf, q, og


def _kvq_body(h_ref, gkv_ref, wk_ref, wv_ref, wf_ref, bf_ref, kg_ref, gb_ref, wq_ref, wog_ref,
              qg_ref, k_ref, v_ref, lf_ref, kb_ref, vb_ref, qb_ref, og_ref):
    k, v, lf, q, og = _kvq_compute(h_ref[...], gkv_ref, wk_ref, wv_ref, wf_ref, bf_ref, kg_ref,
                                   gb_ref, wq_ref, wog_ref, qg_ref)
    k_ref[...] = k
    v_ref[...] = v
    kb_ref[...] = k.astype(BF16)
    vb_ref[...] = v.astype(BF16)
    lf_ref[...] = lf[:, :H_B]
    qb_ref[...] = q.astype(BF16)
    og_ref[...] = og.astype(BF16)


def _placement():
    row = jnp.arange(LANES)[:, None]
    h, part = row % H_B, row // H_B
    col = jnp.arange(H_B * DH_B)[None, :]
    lane = jnp.where(h % 2 == 0, DH_B, 0) + part
    return ((col == (h // 2) * LANES + lane) & (part < N_SPLIT)).astype(BF16)


def _pack3(x):
    hi, mid, lo = (p.astype(F32) for p in _split3(x))
    return (hi + pltpu.roll(mid, H_B, 1) + pltpu.roll(lo, 2 * H_B, 1)).astype(BF16)


def _unpack3_sum(y):
    s = y + pltpu.roll(y, LANES - H_B, 1) + pltpu.roll(y, LANES - 2 * H_B, 1)
    return jnp.where(_iota(y.shape, 1) < H_B, s, 0.0)


def _kvq_t_body(h_ref, gkv_ref, wk_ref, wv_ref, wf_ref, bf_ref, kg_ref, gb_ref, wq_ref, wog_ref,
                qg_ref, place_ref, kt_ref, vt_ref, lft_ref, k0_ref, k1_ref, vtb_ref, qt_ref, og_ref,
                carry, *, seq):
    tm = h_ref.shape[1]
    valid = pl.program_id(1) * tm + _iota((tm, 1), 0) < seq
    h = h_ref[0]
    hr = h * _rms_scale(h)
    xs = (hr * gkv_ref[...]).astype(BF16)
    k = jnp.where(valid, _head_rmsnorm64(_dot(xs, wk_ref[...])) * kg_ref[...], 0.0)
    lf = jnp.where(valid, _log_sigmoid(_dot(xs, wf_ref[...]) + bf_ref[...]), 0.0)

    @pl.when(pl.program_id(1) == 0)
    def _():
        carry[...] = jnp.zeros(carry.shape, F32)

    subs = [slice(s * LANES, (s + 1) * LANES) for s in range(tm // LANES)]
    tril = jnp.where(_iota((LANES, LANES), 0) >= _iota((LANES, LANES), 1), 1.0, 0.0).astype(BF16)
    lo_half = (_iota((LANES, k.shape[1]), 1) & (LANES - 1)) < DH_B
    lf_heads = jnp.where(_iota(lf.shape, 1) < H_B, lf, 0.0)
    local = [_unpack3_sum(_dot(tril, _pack3(lf_heads[rs, :]))) for rs in subs]

    vt = jnp.where(valid, _dot(xs, wv_ref[...]), 0.0).T
    vt_ref[0] = vt
    vtb_ref[0] = vt.astype(BF16)

    run = carry[0:1, :]
    for rs, loc in zip(subs, local):
        c = loc + run
        run = c[LANES - 1:LANES, :]
        bias = _dot(_pack3(c * LOG2E), place_ref[...])
        k0_ref[0, rs, :] = jnp.where(lo_half, k[rs, :], bias).astype(BF16)
        k1_ref[0, rs, :] = jnp.where(lo_half, bias, k[rs, :]).astype(BF16)
    carry[0:1, :] = run

    kt_ref[0] = k.T
    lft_ref[0] = lf.T[:H_B, :]
    xq = (hr * gb_ref[...]).astype(BF16)
    q = _head_rmsnorm64(_dot(xq, wq_ref[...])) * qg_ref[...]
    qt_ref[0] = jnp.where(valid, q, 0.0).T.astype(BF16)
    og_ref[0] = _dot(xq, wog_ref[...]).astype(BF16)


def _kvq_proj_t(h, gkv, wk, wv, wf, bf, kg, gb, wq, wog, qg, batch, seq):
    d = h.shape[1]
    hd = wk.shape[1]
    tm = KVQ_T_TILE
    nt = pl.cdiv(seq, tm)
    tpad = nt * tm
    rows = lambda b, i: (b, i, 0)
    cols = lambda b, i: (b, 0, i)
    place = _placement()
    return pl.pallas_call(
        functools.partial(_kvq_t_body, seq=seq),
        grid=(batch, nt),
        in_specs=[pl.BlockSpec((1, tm, d), rows)] + [_const_spec(w.shape) for w in
                                                      (gkv, wk, wv, wf, bf, kg, gb, wq, wog, qg, place)],
        out_specs=[pl.BlockSpec((1, hd, tm), cols), pl.BlockSpec((1, hd, tm), cols),
                   pl.BlockSpec((1, H_B, tm), cols), pl.BlockSpec((1, tm, hd), rows),
                   pl.BlockSpec((1, tm, hd), rows), pl.BlockSpec((1, hd, tm), cols),
                   pl.BlockSpec((1, hd, tm), cols), pl.BlockSpec((1, tm, wog.shape[1]), rows)],
        out_shape=[jax.ShapeDtypeStruct((batch, hd, seq), F32), jax.ShapeDtypeStruct((batch, hd, seq), F32),
                   jax.ShapeDtypeStruct((batch, H_B, seq), F32), jax.ShapeDtypeStruct((batch, tpad, hd), BF16),
                   jax.ShapeDtypeStruct((batch, tpad, hd), BF16), jax.ShapeDtypeStruct((batch, hd, tpad), BF16),
                   jax.ShapeDtypeStruct((batch, hd, tpad), BF16),
                   jax.ShapeDtypeStruct((batch, seq, wog.shape[1]), BF16)],
        scratch_shapes=[pltpu.VMEM((SUBLANES, LANES), F32)],
        compiler_params=_cparams("parallel", "arbitrary"),
        name="kvq_proj_t",
    )(h.reshape(batch, seq, d), gkv, wk, wv, wf, bf, kg, gb, wq, wog, qg, place)


def _kvq_proj(h, gkv, wk, wv, wf, bf, kg, gb, wq, wog, qg, tm):
    n, d = h.shape
    hd = wk.shape[1]
    row = lambda i: (i, 0)
    wide = pl.BlockSpec((tm, hd), row)
    return pl.pallas_call(
        _kvq_body,
        grid=(pl.cdiv(n, tm),),
        in_specs=[pl.BlockSpec((tm, d), row)] + [_const_spec(w.shape) for w in
                                                 (gkv, wk, wv, wf, bf, kg, gb, wq, wog, qg)],
        out_specs=[wide, wide, pl.BlockSpec((tm, H_B), row), wide, wide, wide,
                   pl.BlockSpec((tm, wog.shape[1]), row)],
        out_shape=[jax.ShapeDtypeStruct((n, hd), F32), jax.ShapeDtypeStruct((n, hd), F32),
                   jax.ShapeDtypeStruct((n, H_B), F32), jax.ShapeDtypeStruct((n, hd), BF16),
                   jax.ShapeDtypeStruct((n, hd), BF16), jax.ShapeDtypeStruct((n, hd), BF16),
                   jax.ShapeDtypeStruct((n, wog.shape[1]), BF16)],
        compiler_params=_cparams("parallel"),
        name="kvq_proj",
    )(h, gkv, wk, wv, wf, bf, kg, gb, wq, wog, qg)


def _col_to_row(col):
    n = col.shape[0]
    eye = _iota((n, n), 0) == _iota((n, n), 1)
    return jnp.sum(jnp.where(eye, col, 0.0), axis=0, keepdims=True)


def _row_to_col(row):
    n = row.shape[1]
    eye = _iota((n, n), 0) == _iota((n, n), 1)
    return jnp.sum(jnp.where(eye, row, 0.0), axis=1, keepdims=True)


def _mlstm_chunk_heads(qk, v, li, lf, cfull, m):
    heads = range(len(qk))
    L = qk[0].shape[0]
    tril = _iota((L, L), 0) >= _iota((L, L), 1)
    lane = _iota((L, LANES), 1)
    ones_col = jnp.where(lane == 0, 1.0, 0.0)

    q_lo = [jnp.where(lane < DK_A, qk[i], 0.0).astype(BF16) for i in heads]
    kq = [pltpu.roll(qk[i], DK_A, 1) for i in heads]
    vaug = [jnp.concatenate([v[i], ones_col], axis=1).astype(BF16) for i in heads]
    qk_t = [_dot_nt(q_lo[i], kq[i].astype(BF16)) for i in heads]
    q_c = [_dot(q_lo[i], cfull[i].astype(BF16)) for i in heads]

    lf_row = [_col_to_row(lf[i]) for i in heads]
    li_row = [_col_to_row(li[i]) for i in heads]
    b = [jnp.sum(jnp.where(tril, lf_row[i], 0.0), axis=1, keepdims=True) for i in heads]
    b_row = [_col_to_row(b[i]) for i in heads]
    dmat = [jnp.where(tril, b[i] - b_row[i] + li_row[i], -jnp.inf) for i in heads]
    dmax = [jnp.max(dmat[i], axis=1, keepdims=True) for i in heads]

    b_end = [b[i][L - 1:L, :] for i in heads]
    g = [b_end[i] - b[i] + li[i] for i in heads]
    m_new = [jnp.maximum(b_end[i] + m[i], jnp.max(g[i], axis=0, keepdims=True)) for i in heads]
    w_c = [jnp.exp(b_end[i] + m[i] - m_new[i]) for i in heads]
    upd = [_dot_tn((jnp.exp(g[i] - m_new[i]) * kq[i]).astype(BF16), vaug[i]) for i in heads]
    keep = _iota(cfull[0].shape, 0) < DK_A
    cfull_new = [jnp.where(keep, w_c[i] * cfull[i] + upd[i], 0.0) for i in heads]

    inter = [b[i] + m[i] for i in heads]
    m_t = [jnp.maximum(inter[i], dmax[i]) for i in heads]
    s = [(qk_t[i] * jnp.exp(dmat[i] - m_t[i])).astype(BF16) for i in heads]
    tot = [jnp.exp(inter[i] - m_t[i]) * q_c[i] + _dot(s[i], vaug[i]) for i in heads]
    h = [tot[i][:, :DV_A] / jnp.maximum(jnp.abs(tot[i][:, DV_A:DV_A + 1]), jnp.exp(-m_t[i])) for i in heads]
    return h, cfull_new, m_new


def _mlstm_head_out(h, og, gain):
    return h * _rms_scale(h) * gain * _sigmoid(og)


def _mlstm_prompt_body(qk_ref, v_ref, og_ref, gt_ref, mhg_ref, hg_ref, c_ref, n_ref, m_ref,
                       cst, mst, *, seq, hb):
    grp = pl.program_id(1)
    ch = MLSTM_CHUNK
    nfull, tail = seq // ch, seq % ch
    cst[...] = jnp.zeros(cst.shape, F32)
    mst[...] = jnp.zeros(mst.shape, F32)

    def chunk(r0, first_valid):
        gt = gt_ref[0, pl.ds(r0, ch), :]
        lane = _iota(gt.shape, 1)
        rowi = _iota((ch, 1), 0)
        li, lf, qk, v = [], [], [], []
        for hh in range(hb):
            head = grp * hb + hh
            li_h = jnp.sum(jnp.where(lane == head, gt, 0.0), axis=1, keepdims=True)
            lf_h = jnp.sum(jnp.where(lane == head + H_A, gt, 0.0), axis=1, keepdims=True)
            if first_valid:
                li_h = jnp.where(rowi >= first_valid, li_h, MASKED_GATE)
                lf_h = jnp.where(rowi >= first_valid, lf_h, 0.0)
            sl = slice(hh * LANES, (hh + 1) * LANES)
            li.append(li_h)
            lf.append(lf_h)
            qk.append(qk_ref[0, pl.ds(r0, ch), sl].astype(F32))
            v.append(v_ref[0, pl.ds(r0, ch), sl].astype(F32))
        hs, cnew, mnew = _mlstm_chunk_heads(qk, v, li, lf, [cst[hh] for hh in range(hb)],
                                            [mst[hh, 0:1, 0:1] for hh in range(hb)])
        outs = []
        for hh in range(hb):
            cst[hh] = cnew[hh]
            mst[hh] = jnp.broadcast_to(mnew[hh], mst.shape[1:])
            og = og_ref[0, pl.ds(r0, ch), hh * LANES:(hh + 1) * LANES].astype(F32)
            outs.append(_mlstm_head_out(hs[hh], og, mhg_ref[hh]))
        return jnp.concatenate(outs, axis=1).astype(BF16)

    def loop_body(j, carry):
        r0 = pl.multiple_of(j * ch, ch)
        hg_ref[0, pl.ds(r0, ch), :] = chunk(r0, 0)
        return carry

    lax.fori_loop(0, nfull, loop_body, 0)
    if tail:
        out = chunk(seq - ch, ch - tail)
        hg_ref[0, seq - tail:seq, :] = out[ch - tail:, :]

    for hh in range(hb):
        cfull = cst[hh]
        c_ref[0, hh] = cfull[:DK_A, :DV_A]
        n_ref[0, hh] = _col_to_row(cfull[:DK_A, DV_A:DV_A + 1])
        m_ref[0, hh] = mst[hh, 0:1, 0:1]


def _mlstm_prompt(qk, v, og, gt, mhg, batch, seq):
    hb = MLSTM_HEADS_PER_STEP
    w = hb * LANES
    qk3, v3, og3 = (a.reshape(batch, seq, a.shape[-1]) for a in (qk, v, og))
    gt3 = gt.reshape(batch, seq, LANES)
    seq_blk = lambda b, g: (b, 0, g)
    hg, c, n, m = pl.pallas_call(
        functools.partial(_mlstm_prompt_body, seq=seq, hb=hb),
        grid=(batch, H_A // hb),
        in_specs=[pl.BlockSpec((1, seq, w), seq_blk), pl.BlockSpec((1, seq, w), seq_blk),
                  pl.BlockSpec((1, seq, w), seq_blk),
                  pl.BlockSpec((1, seq, LANES), lambda b, g: (b, 0, 0)),
                  pl.BlockSpec((hb, 1, DV_A), lambda b, g: (g, 0, 0))],
        out_specs=[pl.BlockSpec((1, seq, w), seq_blk),
                   pl.BlockSpec((1, hb, DK_A, DV_A), lambda b, g: (b, g, 0, 0)),
                   pl.BlockSpec((1, hb, 1, DK_A), lambda b, g: (b, g, 0, 0)),
                   pl.BlockSpec((1, hb, 1, 1), lambda b, g: (b, g, 0, 0))],
        out_shape=[jax.ShapeDtypeStruct((batch, seq, H_A * DV_A), BF16),
                   jax.ShapeDtypeStruct((batch, H_A, DK_A, DV_A), F32),
                   jax.ShapeDtypeStruct((batch, H_A, 1, DK_A), F32),
                   jax.ShapeDtypeStruct((batch, H_A, 1, 1), F32)],
        scratch_shapes=[pltpu.VMEM((hb, LANES, 2 * LANES), F32), pltpu.VMEM((hb, SUBLANES, LANES), F32)],
        compiler_params=_cparams("parallel", "arbitrary"),
        name="mlstm_prompt",
    )(qk3, v3, og3, gt3, mhg.reshape(H_A, 1, DV_A))
    return (hg.reshape(batch * seq, H_A * DV_A), c, n.reshape(batch, H_A, DK_A),
            m.reshape(batch, H_A))


def _mlstm_sample_body(qk_ref, v_ref, og_ref, gt_ref, mhg_ref, c0_ref, n0_ref, m0_ref,
                       hg_ref, c_ref, n_ref, m_ref, *, steps):
    ch = pl.cdiv(steps, SAMPLE_CHUNK_ALIGN) * SAMPLE_CHUNK_ALIGN
    pad = jnp.zeros((ch - steps, LANES), F32)
    padded = (lambda a: jnp.concatenate([a, pad], axis=0)) if ch > steps else (lambda a: a)
    rowi = _iota((ch, 1), 0)
    lane1 = _iota((1, LANES), 1)
    lane_c = _iota((DK_A, LANES), 1)
    nseq = qk_ref.shape[0]
    li, lf, qk, v, cfull, m0 = [], [], [], [], [], []
    for sq in range(nseq):
        gt = padded(gt_ref[sq])
        lane = _iota(gt.shape, 1)
        for hh in range(H_A):
            sl = slice(hh * LANES, (hh + 1) * LANES)
            li_h = jnp.sum(jnp.where(lane == hh, gt, 0.0), axis=1, keepdims=True)
            lf_h = jnp.sum(jnp.where(lane == hh + H_A, gt, 0.0), axis=1, keepdims=True)
            li.append(jnp.where(rowi < steps, li_h, MASKED_GATE))
            lf.append(jnp.where(rowi < steps, lf_h, 0.0))
            qk.append(padded(qk_ref[sq, :, sl].astype(F32)))
            v.append(padded(v_ref[sq, :, sl].astype(F32)))
            ncol = _row_to_col(n0_ref[sq, hh:hh + 1, :])
            top = jnp.concatenate([c0_ref[sq, hh], jnp.where(lane_c == 0, ncol, 0.0)], axis=1)
            cfull.append(jnp.concatenate([top, jnp.zeros((LANES - DK_A, 2 * LANES), F32)], axis=0))
            m0.append(m0_ref[sq, :, hh:hh + 1])
    hs, cnew, mnew = _mlstm_chunk_heads(qk, v, li, lf, cfull, m0)
    for sq in range(nseq):
        m_out = jnp.zeros((1, LANES), F32)
        outs = []
        for hh in range(H_A):
            i = sq * H_A + hh
            og = og_ref[sq, :, hh * LANES:(hh + 1) * LANES].astype(F32)
            outs.append(_mlstm_head_out(hs[i][:steps, :], og, mhg_ref[hh]))
            c_ref[sq, hh] = cnew[i][:DK_A, :DV_A]
            n_ref[sq, hh:hh + 1, :] = _col_to_row(cnew[i][:DK_A, DV_A:DV_A + 1])
            m_out = jnp.where(lane1 == hh, mnew[i], m_out)
        hg_ref[sq] = jnp.concatenate(outs, axis=1).astype(BF16)
        m_ref[sq] = m_out[:, :H_A]


def _mlstm_sample(qk, v, og, gt, mhg, c0, n0, m0, batch, steps):
    wide = H_A * LANES
    ns = MLSTM_SAMPLE_SEQS if batch % MLSTM_SAMPLE_SEQS == 0 else 1
    blk3 = lambda b: (b, 0, 0)
    hg, c, n, m = pl.pallas_call(
        functools.partial(_mlstm_sample_body, steps=steps),
        grid=(batch // ns,),
        in_specs=[pl.BlockSpec((ns, steps, wide), blk3), pl.BlockSpec((ns, steps, wide), blk3),
                  pl.BlockSpec((ns, steps, wide), blk3), pl.BlockSpec((ns, steps, LANES), blk3),
                  _const_spec((H_A, 1, DV_A)),
                  pl.BlockSpec((ns, H_A, DK_A, DV_A), lambda b: (b, 0, 0, 0)),
                  pl.BlockSpec((ns, H_A, DK_A), blk3), pl.BlockSpec((ns, 1, H_A), blk3)],
        out_specs=[pl.BlockSpec((ns, steps, wide), blk3),
                   pl.BlockSpec((ns, H_A, DK_A, DV_A), lambda b: (b, 0, 0, 0)),
                   pl.BlockSpec((ns, H_A, DK_A), blk3), pl.BlockSpec((ns, 1, H_A), blk3)],
        out_shape=[jax.ShapeDtypeStruct((batch, steps, wide), BF16),
                   jax.ShapeDtypeStruct((batch, H_A, DK_A, DV_A), F32),
                   jax.ShapeDtypeStruct((batch, H_A, DK_A), F32),
                   jax.ShapeDtypeStruct((batch, 1, H_A), F32)],
        compiler_params=_cparams("parallel"),
        name="mlstm_sample",
    )(qk.reshape(batch, steps, wide), v.reshape(batch, steps, wide), og.reshape(batch, steps, wide),
      gt.reshape(batch, steps, LANES), mhg.reshape(H_A, 1, DV_A), c0, n0, m0.reshape(batch, 1, H_A))
    return hg.reshape(batch * steps, wide), c, n, m.reshape(batch, H_A)


def _rows_to_lanes(x16, staging_ref):
    staging_ref[...] = jnp.zeros(staging_ref.shape, F32)
    staging_ref[:, 0:x16.shape[1]] = x16
    return staging_ref[...].T[0:x16.shape[1], :]


def _attn_prompt_body(qt_ref, k0_ref, k1_ref, vt_ref, og_ref, o_ref, s_scr, p_scr, acc_scr, qa_scr,
                      mask_scr, *, seq):
    tq, tk = ATTN_TQ, ATTN_TK
    tpad = qt_ref.shape[2]
    nfull = seq // tq
    nh = ATTN_HEADS
    heads = range(nh)

    def keys(h, rows):
        pair = slice((h // 2) * LANES, (h // 2 + 1) * LANES)
        return (k0_ref if h % 2 == 0 else k1_ref)[0, rows, pair]

    def augmented_queries(qt):
        out = []
        for pp in range(nh // 2):
            blk = qt[pp * LANES:(pp + 1) * LANES, :]
            row = _iota(blk.shape, 0)
            out.append(jnp.where(row < DH_B, blk, jnp.where(row < DH_B + N_SPLIT, -1.0, 0.0)).astype(BF16))
            out.append(jnp.where(row >= DH_B, blk, jnp.where(row < N_SPLIT, -1.0, 0.0)).astype(BF16))
        return out

    def with_ones(vt):
        return jnp.concatenate([vt, jnp.ones((ROWSUM_ROWS, vt.shape[1]), BF16)], axis=0)

    def tail_tile(q0, width, rows_out):
        qa = augmented_queries(qt_ref[0, :, pl.ds(q0, width)].astype(F32))
        causal = _iota((tpad, width), 0) <= q0 + _iota((tpad, width), 1)
        outs = []
        for hh in heads:
            s = jnp.where(causal, _dot(keys(hh, slice(None)), qa[hh]), -jnp.inf)
            p = jnp.exp2(s - jnp.max(s, axis=0, keepdims=True)).astype(BF16)
            full = _dot(with_ones(vt_ref[0, hh * DH_B:(hh + 1) * DH_B, :]), p)
            outs.append(full[:DH_B, :] / full[DH_B:DH_B + 1, :])
        out = jnp.concatenate(outs, axis=0).T[:rows_out, :]
        gate = _sigmoid(og_ref[0, pl.ds(q0, rows_out), :].astype(F32))
        o_ref[0, pl.ds(q0, rows_out), :] = (out * gate).astype(BF16)

    per = tq // tk
    for d in range(per):
        key = d * tk + _iota((tk, tq), 0)
        mask_scr[d] = jnp.where(key <= _iota((tk, tq), 1), 0.0, -jnp.inf)
    p_scr[1] = jnp.zeros(p_scr.shape[1:], BF16)
    acc_scr[...] = jnp.zeros(acc_scr.shape, F32)

    def first_lane(diag_idx):
        return 0 if diag_idx is None else diag_idx * tk

    def left_pad(x, lo, fill):
        return x if lo == 0 else jnp.concatenate([jnp.full((x.shape[0], lo), fill, x.dtype), x], axis=1)

    def value_product(hh, k0, slot, lo):
        vt = with_ones(vt_ref[0, hh * DH_B:(hh + 1) * DH_B, pl.ds(k0, tk)])
        return left_pad(_dot(vt, p_scr[slot, hh, :, lo:]), lo, 0.0)

    def pipe_step(ms, kidx, slot, diag_idx, issue_next):
        k0 = pl.multiple_of(kidx * tk, tk)
        lo = first_lane(diag_idx)
        lo_prev = first_lane(diag_idx - 1 if diag_idx else None)
        lo_next = first_lane(None if diag_idx is None else diag_idx + 1)
        if issue_next:
            for hh in heads:
                s_scr[1 - slot, hh, :, lo_next:] = _dot(keys(hh, pl.ds(k0 + tk, tk)), qa_scr[hh, :, lo_next:])
        kprev = pl.multiple_of(jnp.maximum(kidx - 1, 0) * tk, tk)
        pvs = [value_product(hh, kprev, 1 - slot, lo_prev) for hh in heads]

        def scores(hh):
            s = s_scr[slot, hh, :, lo:]
            return s if diag_idx is None else s + mask_scr[diag_idx, :, lo:]

        new = [jnp.maximum(ms[hh][:, lo:], jnp.max(scores(hh), axis=0, keepdims=True)) for hh in heads]
        for hh in heads:
            p_scr[slot, hh, :, lo:] = jnp.exp2(scores(hh) - new[hh]).astype(BF16)
        new = [n if lo == 0 else jnp.concatenate([ms[hh][:, :lo], n], axis=1) for hh, n in zip(heads, new)]
        for hh in heads:
            acc_scr[hh] = (acc_scr[hh] + pvs[hh]) * jnp.exp2(ms[hh] - new[hh])
        return tuple(new)

    def q_tile_pipelined(i):
        q0 = pl.multiple_of(i * tq, tq)
        qa = augmented_queries(qt_ref[0, :, pl.ds(q0, tq)].astype(F32))
        for hh in heads:
            qa_scr[hh] = qa[hh]
        for hh in heads:
            s_scr[0, hh] = _dot(keys(hh, pl.ds(0, tk)), qa_scr[hh])
        ms = tuple(jnp.full((1, tq), NEG_INIT, F32) for _ in heads)

        def group(jj, st):
            for d in range(per):
                st = pipe_step(st, jj * per + d, d % 2, None, True)
            return st

        ms = lax.fori_loop(0, i, group, ms)
        for d in range(per):
            ms = pipe_step(ms, i * per + d, d % 2, d, d < per - 1)
        last = (per - 1) % 2
        klast = pl.multiple_of((i * per + per - 1) * tk, tk)
        outs = []
        for hh in heads:
            full = acc_scr[hh] + value_product(hh, klast, last, first_lane(per - 1))
            outs.append(full[:DH_B, :] / full[DH_B:DH_B + 1, :])
        out = jnp.concatenate(outs, axis=0).T
        gate = _sigmoid(og_ref[0, pl.ds(q0, tq), :].astype(F32))
        o_ref[0, pl.ds(q0, tq), :] = (out * gate).astype(BF16)

    def qbody(i, _):
        q_tile_pipelined(i)
        return 0

    lax.fori_loop(0, nfull, qbody, 0)
    if seq > nfull * tq:
        q0 = nfull * tq
        tail_tile(q0, tpad - q0, seq - q0)


def _attn_prompt(qt, k0, k1, vt, og, seq):
    batch, wide, tpad = qt.shape
    tail_w = tpad - (seq // ATTN_TQ) * ATTN_TQ
    assert ATTN_TQ % (2 * ATTN_TK) == 0 and 0 <= tail_w and tail_w % LANES == 0
    rows = lambda b, p: (b, 0, p)
    cols = lambda b, p: (b, p, 0)
    nh = ATTN_HEADS
    w = nh * DH_B
    out = pl.pallas_call(
        functools.partial(_attn_prompt_body, seq=seq),
        grid=(batch, H_B // nh),
        in_specs=[pl.BlockSpec((1, w, tpad), cols), pl.BlockSpec((1, tpad, w), rows),
                  pl.BlockSpec((1, tpad, w), rows), pl.BlockSpec((1, w, tpad), cols),
                  pl.BlockSpec((1, seq, w), rows)],
        out_specs=pl.BlockSpec((1, seq, w), rows),
        out_shape=jax.ShapeDtypeStruct((batch, seq, wide), BF16),
        scratch_shapes=[pltpu.VMEM((2, nh, ATTN_TK, ATTN_TQ), F32),
                        pltpu.VMEM((2, nh, ATTN_TK, ATTN_TQ), BF16),
                        pltpu.VMEM((nh, DH_B + ROWSUM_ROWS, ATTN_TQ), F32),
                        pltpu.VMEM((nh, LANES, ATTN_TQ), BF16),
                        pltpu.VMEM((ATTN_TQ // ATTN_TK, ATTN_TK, ATTN_TQ), F32)],
        compiler_params=_cparams("parallel", "parallel"),
        name="attn_prompt",
    )(qt, k0, k1, vt, og)
    return out.reshape(batch * seq, wide)


def _attn_sample_body(pt_ref, q_ref, kn_ref, vn_ref, lfn_ref, og_ref, *rest, steps, npages):
    k_refs, v_refs, lf_refs = rest[:npages], rest[npages:2 * npages], rest[2 * npages:3 * npages]
    o_ref, stage = rest[3 * npages], rest[3 * npages + 1]
    rows = H_B * steps
    wide = H_B * DH_B
    u = _iota((LANES, LANES), 0)
    s_ = _iota((LANES, LANES), 1)

    q = q_ref[0].astype(F32)
    qrep = jnp.concatenate([jnp.broadcast_to(q[t:t + 1, :], (H_B, wide)) for t in range(steps)], axis=0)
    diag = _iota((rows, wide), 0) % H_B == _iota((rows, wide), 1) // DH_B
    qbd = jnp.where(diag, qrep, 0.0).astype(BF16)

    lf_all = jnp.concatenate([r[0] for r in lf_refs], axis=0)
    later_and_ones = jnp.concatenate([jnp.where(u > s_, 1.0, 0.0), jnp.ones((LANES, LANES), F32)],
                                     axis=1).astype(BF16)
    wt = _dot_by_01(lf_all, later_and_ones)
    pr = _iota((npages * H_B, npages * H_B), 0)
    pc = _iota((npages * H_B, npages * H_B), 1)
    later_pages = jnp.where((pc % H_B == pr % H_B) & (pc // H_B > pr // H_B), 1.0, 0.0).astype(BF16)
    hi, mid, lo = _split3(wt[:, LANES:])
    rsum = wt[:, :LANES] + _dot(later_pages, hi) + _dot(later_pages, mid) + _dot(later_pages, lo)
    bias_past = jnp.concatenate(
        [jnp.concatenate([rsum[r * H_B:(r + 1) * H_B, :]] * steps, axis=0) for r in range(npages)], axis=1)

    kcat = jnp.concatenate([r[0].astype(BF16) for r in k_refs], axis=1)
    s_past = _dot(qbd, kcat) + bias_past

    zpad = jnp.zeros((LANES - steps, wide), F32)
    kn = jnp.concatenate([kn_ref[0].astype(F32), zpad], axis=0).astype(BF16)
    vn = jnp.concatenate([vn_ref[0].astype(F32), zpad], axis=0).astype(BF16)
    lfn = jnp.concatenate([lfn_ref[0], jnp.zeros((LANES - steps, H_B), F32)], axis=0)
    incl = jnp.where(u <= s_, 1.0, 0.0).astype(BF16)
    cnew = _dot_by_01(_rows_to_lanes(lfn, stage), incl)
    key = _iota((rows, LANES), 1)
    qry = _iota((rows, LANES), 0) // H_B
    bias_new = jnp.where(key <= qry, -jnp.concatenate([cnew] * steps, axis=0), -jnp.inf)
    s_new = _dot_nt(qbd, kn) + bias_new

    m = jnp.maximum(jnp.max(s_past, axis=1, keepdims=True), jnp.max(s_new, axis=1, keepdims=True))
    p_past = jnp.exp(s_past - m)
    p_new = jnp.exp(s_new - m)
    l = jnp.sum(p_past, axis=1, keepdims=True) + jnp.sum(p_new, axis=1, keepdims=True)
    vcat = jnp.concatenate([r[0].astype(BF16) for r in v_refs], axis=1)
    acc = _dot_nt(p_past.astype(BF16), vcat) + _dot(p_new.astype(BF16), vn)
    full = jnp.where(diag, acc / l, 0.0)
    out = jnp.concatenate([jnp.sum(full[t * H_B:(t + 1) * H_B, :], axis=0, keepdims=True)
                           for t in range(steps)], axis=0)
    o_ref[0] = (out * _sigmoid(og_ref[0].astype(F32))).astype(BF16)


def _attn_sample(q, kn, vn, lfn, og, cache_k, cache_v, cache_logf, page_table, batch, steps):
    wide = H_B * DH_B
    n_phys, page = cache_k.shape[0], cache_k.shape[1]
    npages = page_table.shape[1]
    assert page == LANES
    ck = jnp.transpose(cache_k, (0, 2, 3, 1)).reshape(n_phys, wide, page)
    cv = jnp.transpose(cache_v, (0, 2, 3, 1)).reshape(n_phys, wide, page)
    clf = jnp.transpose(cache_logf, (0, 2, 1))
    tok = lambda b, pt: (b, 0, 0)
    page_of = lambda r: (lambda b, pt: (pt[b, r], 0, 0))
    out = pl.pallas_call(
        functools.partial(_attn_sample_body, steps=steps, npages=npages),
        grid_spec=pltpu.PrefetchScalarGridSpec(
            num_scalar_prefetch=1,
            grid=(batch,),
            in_specs=[pl.BlockSpec((1, steps, wide), tok), pl.BlockSpec((1, steps, wide), tok),
                      pl.BlockSpec((1, steps, wide), tok), pl.BlockSpec((1, steps, H_B), tok),
                      pl.BlockSpec((1, steps, wide), tok)]
            + [pl.BlockSpec((1, wide, page), page_of(r)) for r in range(npages)]
            + [pl.BlockSpec((1, wide, page), page_of(r)) for r in range(npages)]
            + [pl.BlockSpec((1, H_B, page), page_of(r)) for r in range(npages)],
            out_specs=pl.BlockSpec((1, steps, wide), tok),
            scratch_shapes=[pltpu.VMEM((LANES, LANES), F32)]),
        out_shape=jax.ShapeDtypeStruct((batch, steps, wide), BF16),
        compiler_params=_cparams("parallel"),
        name="attn_sample",
    )(page_table, q.reshape(batch, steps, wide), kn.reshape(batch, steps, wide),
      vn.reshape(batch, steps, wide), lfn.reshape(batch, steps, H_B), og.reshape(batch, steps, wide),
      *([ck] * npages), *([cv] * npages), *([clf] * npages))
    return out.reshape(batch * steps, wide)


def _prep_params(norm_a, w_in_a, b_ig_a, b_fg_a, mh_norm_a, w_out_a, norm_kv, w_kvf, b_fg_b,
                 k_norm_b, norm_b, w_qo_b, q_norm_b, w_out_b, norm_ffn, w_gate_up, w_down,
                 norm_final):
    d = w_in_a.shape[1]
    hk, hv, hd = H_A * DK_A, H_A * DV_A, H_B * DH_B
    w_in = w_in_a[0]
    wq = w_in[:, :hk].reshape(d, H_A, DK_A)
    wk = w_in[:, hk:2 * hk].reshape(d, H_A, DK_A)
    row = lambda a: a.reshape(1, -1).astype(F32)
    pad_cols = lambda a: jnp.pad(a, ((0, 0), (0, LANES - a.shape[1])))
    lane = jnp.arange(H_A * LANES) % LANES
    return dict(
        norm_a=row(norm_a[0]),
        wqk=jnp.concatenate([wq, wk], axis=2).reshape(d, H_A * LANES).astype(BF16),
        wv=w_in[:, 2 * hk:2 * hk + hv].astype(BF16),
        wog=w_in[:, 2 * hk + hv:2 * hk + 2 * hv].astype(BF16),
        wg=pad_cols(w_in[:, 2 * hk + 2 * hv:]).astype(BF16),
        bg=pad_cols(jnp.concatenate([b_ig_a[0], b_fg_a[0]]).reshape(1, -1).astype(F32)),
        qs=jnp.where(lane < DK_A, DK_A ** -0.5, 1.0).reshape(1, -1).astype(F32),
        mhg=mh_norm_a[0].astype(F32),
        wo_a=w_out_a[0].astype(BF16),
        gkv=row(norm_kv),
        wk=w_kvf[:, :hd].astype(BF16),
        wvs=w_kvf[:, hd:2 * hd].astype(BF16),
        wf=pad_cols(w_kvf[:, 2 * hd:]).astype(BF16),
        bf=pad_cols(b_fg_b.reshape(1, -1).astype(F32)),
        kg=row(jnp.tile(k_norm_b, H_B)),
        gb=row(norm_b[0]),
        wq=w_qo_b[0][:, :hd].astype(BF16),
        wog_b=w_qo_b[0][:, hd:].astype(BF16),
        qg=row(jnp.tile(q_norm_b[0], H_B)) * DH_B ** -0.5,
        wo_b=w_out_b[0].astype(BF16),
        gf=[row(norm_ffn[l]) for l in range(2)],
        wgu=[w_gate_up[l].astype(BF16) for l in range(2)],
        wd=[w_down[l].astype(BF16) for l in range(2)],
        gfin=row(norm_final),
    )


def _layer0(h, p, tm, mlstm):
    qk, v, og, gt = _proj_in(h, p["norm_a"], p["wqk"], p["wv"], p["wog"], p["wg"], p["bg"], p["qs"], tm)
    hg, c, n, m = mlstm(qk, v, og, gt)
    h2 = _mix_ffn(hg, h, p["wo_a"], p["gf"][0], p["wgu"][0], p["wd"][0], p["gfin"], tm, False)
    return h2, c, n, m


def _shared_and_q(h2, p, tm):
    return _kvq_proj(h2, p["gkv"], p["wk"], p["wvs"], p["wf"], p["bf"], p["kg"], p["gb"],
                     p["wq"], p["wog_b"], p["qg"], tm)


def _layer1_tail(o, h2, p, tm, drop_lead=None):
    return _mix_ffn(o, h2, p["wo_b"], p["gf"][1], p["wgu"][1], p["wd"][1], p["gfin"], tm, True, drop_lead)


def kernel(x_prompt, x_sample, state_C, state_n, state_m, cache_k, cache_v, cache_logf, page_table,
           meta_tokens, norm_a, w_in_a, b_ig_a, b_fg_a, mh_norm_a, w_out_a, norm_kv, w_kvf, b_fg_b,
           k_norm_b, norm_b, w_qo_b, q_norm_b, w_out_b, norm_ffn, w_gate_up, w_down, norm_final):
    assert w_in_a.shape[0] == 1 and w_qo_b.shape[0] == 1 and norm_ffn.shape[0] == 2
    p = _prep_params(norm_a, w_in_a, b_ig_a, b_fg_a, mh_norm_a, w_out_a, norm_kv, w_kvf, b_fg_b,
                     k_norm_b, norm_b, w_qo_b, q_norm_b, w_out_b, norm_ffn, w_gate_up, w_down,
                     norm_final)
    bp, sp, d = x_prompt.shape
    bs, ss, _ = x_sample.shape
    tp = sp + N_META
    hd = H_B * DH_B
    tm = 512

    meta = jnp.broadcast_to(meta_tokens[None].astype(F32), (bp, N_META, d))
    h0 = jnp.concatenate([meta, x_prompt], axis=1).reshape(bp * tp, d)
    h2, p_c, p_n, p_m = _layer0(h0, p, tm, functools.partial(_mlstm_prompt, mhg=p["mhg"], batch=bp, seq=tp))
    kt, vt, lft, k0, k1, vtb, qt, og = _kvq_proj_t(
        h2, p["gkv"], p["wk"], p["wvs"], p["wf"], p["bf"], p["kg"], p["gb"], p["wq"], p["wog_b"],
        p["qg"] * LOG2E, bp, tp)
    o = _attn_prompt(qt, k0, k1, vtb, og, tp)
    y_prompt = _layer1_tail(o, h2, p, tm, drop_lead=(bp, tp, N_META)).reshape(bp, sp, d)
    p_k = jnp.transpose(kt.reshape(bp, H_B, DH_B, tp), (0, 3, 1, 2))
    p_v = jnp.transpose(vt.reshape(bp, H_B, DH_B, tp), (0, 3, 1, 2))
    p_lf = jnp.transpose(lft, (0, 2, 1))

    hs0 = x_sample.reshape(bs * ss, d)
    hs2, s_c, s_n, s_m = _layer0(
        hs0, p, tm, functools.partial(_mlstm_sample, mhg=p["mhg"], c0=state_C[0], n0=state_n[0],
                                      m0=state_m[0], batch=bs, steps=ss))
    ks, vs, lfs, kbs, vbs, qbs, ogs = _shared_and_q(hs2, p, tm)
    os_ = _attn_sample(qbs, kbs, vbs, lfs, ogs, cache_k, cache_v, cache_logf, page_table, bs, ss)
    y_sample = _layer1_tail(os_, hs2, p, tm).reshape(bs, ss, d)

    return (y_prompt, y_sample, p_c[None], p_n[None], p_m[None], p_k, p_v, p_lf,
            s_c[None], s_n[None], s_m[None], ks.reshape(bs, ss, H_B, DH_B),
            vs.reshape(bs, ss, H_B, DH_B), lfs.reshape(bs, ss, H_B))
```

```python
import functools

import jax
import jax.numpy as jnp
from jax import lax
from jax.experimental import pallas as pl
from jax.experimental.pallas import tpu as pltpu

F32 = jnp.float32
BF16 = jnp.bfloat16

N_META = 16
H_A = 8
DK_A = 64
DV_A = 128
GATE_CAP = 15.0
H_B = 16
DH_B = 64
EPS = 1e-6

LANES = 128
SUBLANES = 8
VMEM_LIMIT_BYTES = 56 * 1024 * 1024

MLSTM_CHUNK = 128
MLSTM_HEADS_PER_STEP = 4
MLSTM_SAMPLE_SEQS = 4
SAMPLE_CHUNK_ALIGN = 8
ATTN_TQ = 512
ATTN_TK = 256
ATTN_HEADS = 4
KVQ_T_TILE = 384
N_SPLIT = 3
ROWSUM_ROWS = 16
LOG2E = 1.4426950408889634
FF_CHUNK = 256
MASKED_GATE = -1e30
NEG_INIT = -1e30


def _cparams(*sem):
    return pltpu.CompilerParams(dimension_semantics=sem, vmem_limit_bytes=VMEM_LIMIT_BYTES)


def _const_spec(shape):
    nd = len(shape)
    return pl.BlockSpec(shape, lambda *_: (0,) * nd, pipeline_mode=pl.Buffered(1))


def _rms_scale(x):
    return lax.rsqrt(jnp.mean(x * x, axis=-1, keepdims=True) + EPS)


def _log_sigmoid(x):
    return jnp.minimum(x, 0.0) - jnp.log1p(jnp.exp(-jnp.abs(x)))


def _sigmoid(x):
    return 1.0 / (1.0 + jnp.exp(-x))


def _dot(a, b):
    return jnp.dot(a, b, preferred_element_type=F32)


def _dot_nt(a, b):
    return lax.dot_general(a, b, (((1,), (1,)), ((), ())), preferred_element_type=F32)


def _dot_tn(a, b):
    return lax.dot_general(a, b, (((0,), (0,)), ((), ())), preferred_element_type=F32)


def _split3(x):
    hi = x.astype(BF16)
    r1 = x - hi.astype(F32)
    mid = r1.astype(BF16)
    lo = (r1 - mid.astype(F32)).astype(BF16)
    return hi, mid, lo


def _dot_by_01(x, m01):
    hi, mid, lo = _split3(x)
    return _dot(hi, m01) + _dot(mid, m01) + _dot(lo, m01)


def _iota(shape, dim):
    return lax.broadcasted_iota(jnp.int32, shape, dim)


def _proj_in_body(x_ref, g_ref, wqk_ref, wv_ref, wog_ref, wg_ref, bg_ref, qs_ref,
                  qk_ref, v_ref, og_ref, gt_ref):
    x = x_ref[...]
    xn = (x * _rms_scale(x) * g_ref[...]).astype(BF16)
    qk_ref[...] = (_dot(xn, wqk_ref[...]) * qs_ref[...]).astype(BF16)
    v_ref[...] = _dot(xn, wv_ref[...]).astype(BF16)
    og_ref[...] = _dot(xn, wog_ref[...]).astype(BF16)
    z = _dot(xn, wg_ref[...]) + bg_ref[...]
    cap = GATE_CAP * jnp.tanh(z / GATE_CAP)
    lane = _iota(cap.shape, 1)
    gt_ref[...] = jnp.where(lane < H_A, cap, _log_sigmoid(cap))


def _proj_in(x, g, wqk, wv, wog, wg, bg, qs, tm):
    n, d = x.shape
    row = lambda i: (i, 0)
    return pl.pallas_call(
        _proj_in_body,
        grid=(pl.cdiv(n, tm),),
        in_specs=[pl.BlockSpec((tm, d), row), _const_spec(g.shape), _const_spec(wqk.shape),
                  _const_spec(wv.shape), _const_spec(wog.shape), _const_spec(wg.shape),
                  _const_spec(bg.shape), _const_spec(qs.shape)],
        out_specs=[pl.BlockSpec((tm, wqk.shape[1]), row), pl.BlockSpec((tm, wv.shape[1]), row),
                   pl.BlockSpec((tm, wog.shape[1]), row), pl.BlockSpec((tm, LANES), row)],
        out_shape=[jax.ShapeDtypeStruct((n, wqk.shape[1]), BF16),
                   jax.ShapeDtypeStruct((n, wv.shape[1]), BF16),
                   jax.ShapeDtypeStruct((n, wog.shape[1]), BF16),
                   jax.ShapeDtypeStruct((n, LANES), F32)],
        compiler_params=_cparams("parallel"),
        name="proj_in",
    )(x, g, wqk, wv, wog, wg, bg, qs)


def _mix_ffn_body(a_ref, h_ref, wo_ref, gf_ref, wgu_ref, wd_ref, gout_ref, o_ref, *, d_ff, final):
    h1 = h_ref[...] + _dot(a_ref[...], wo_ref[...])
    xn = (h1 * _rms_scale(h1) * gf_ref[...]).astype(BF16)
    acc = h1
    for c in range(d_ff // FF_CHUNK):
        lo = c * FF_CHUNK
        gate = _dot(xn, wgu_ref[:, lo:lo + FF_CHUNK])
        up = _dot(xn, wgu_ref[:, d_ff + lo:d_ff + lo + FF_CHUNK])
        act = (gate * _sigmoid(gate) * up).astype(BF16)
        acc = acc + _dot(act, wd_ref[lo:lo + FF_CHUNK, :])
    if final:
        acc = acc * _rms_scale(acc) * gout_ref[...]
    o_ref[...] = acc


def _mix_ffn(a, h, wo, gf, wgu, wd, gout, tm, final, drop_lead=None):
    n, d = h.shape
    d_ff = wd.shape[0]
    consts = [_const_spec(w.shape) for w in (wo, gf, wgu, wd, gout)]
    body = functools.partial(_mix_ffn_body, d_ff=d_ff, final=final)
    name = "mix_ffn_final" if final else "mix_ffn"
    if drop_lead is None:
        row = lambda i: (i, 0)
        return pl.pallas_call(
            body, grid=(pl.cdiv(n, tm),),
            in_specs=[pl.BlockSpec((tm, a.shape[1]), row), pl.BlockSpec((tm, d), row)] + consts,
            out_specs=pl.BlockSpec((tm, d), row),
            out_shape=jax.ShapeDtypeStruct((n, d), F32),
            compiler_params=_cparams("parallel"), name=name,
        )(a, h, wo, gf, wgu, wd, gout)
    batch, seq, lead = drop_lead
    nt = (seq - lead) // tm
    align = 2 * SUBLANES
    assert nt * tm == seq - lead and batch * seq == n and seq % align == 0 and lead % align == 0
    src = lambda b, i: (pl.multiple_of(b * seq + lead + i * tm, align), 0)
    return pl.pallas_call(
        body, grid=(batch, nt),
        in_specs=[pl.BlockSpec((pl.Element(tm), pl.Element(a.shape[1])), src),
                  pl.BlockSpec((pl.Element(tm), pl.Element(d)), src)] + consts,
        out_specs=pl.BlockSpec((tm, d), lambda b, i: (b * nt + i, 0)),
        out_shape=jax.ShapeDtypeStruct((batch * nt * tm, d), F32),
        compiler_params=_cparams("parallel", "parallel"), name=name,
    )(a, h, wo, gf, wgu, wd, gout)


def _head_rmsnorm64(x):
    outs = []
    for j in range(x.shape[1] // LANES):
        blk = x[:, j * LANES:(j + 1) * LANES]
        sq = blk * blk
        lane = _iota(blk.shape, 1)
        s_all = jnp.sum(sq, axis=1, keepdims=True)
        s_lo = jnp.sum(jnp.where(lane < DH_B, sq, 0.0), axis=1, keepdims=True)
        ms = jnp.where(lane < DH_B, s_lo, s_all - s_lo) / DH_B
        outs.append(blk * lax.rsqrt(ms + EPS))
    return jnp.concatenate(outs, axis=1)


def _kvq_compute(h, gkv_ref, wk_ref, wv_ref, wf_ref, bf_ref, kg_ref, gb_ref, wq_ref, wog_ref, qg_ref):
    hr = h * _rms_scale(h)
    xs = (hr * gkv_ref[...]).astype(BF16)
    k = _head_rmsnorm64(_dot(xs, wk_ref[...])) * kg_ref[...]
    v = _dot(xs, wv_ref[...])
    lf = _log_sigmoid(_dot(xs, wf_ref[...]) + bf_ref[...])
    xq = (hr * gb_ref[...]).astype(BF16)
    q = _head_rmsnorm64(_dot(xq, wq_ref[...])) * qg_ref[...]
    og = _dot(xq, wog_ref[...])
    return k, v, lf, q, og


def _kvq_body(h_ref, gkv_ref, wk_ref, wv_ref, wf_ref, bf_ref, kg_ref, gb_ref, wq_ref, wog_ref,
              qg_ref, k_ref, v_ref, lf_ref, kb_ref, vb_ref, qb_ref, og_ref):
    k, v, lf, q, og = _kvq_compute(h_ref[...], gkv_ref, wk_ref, wv_ref, wf_ref, bf_ref, kg_ref,
                                   gb_ref, wq_ref, wog_ref, qg_ref)
    k_ref[...] = k
    v_ref[...] = v
    kb_ref[...] = k.astype(BF16)
    vb_ref[...] = v.astype(BF16)
    lf_ref[...] = lf[:, :H_B]
    qb_ref[...] = q.astype(BF16)
    og_ref[...] = og.astype(BF16)


def _placement():
    row = jnp.arange(LANES)[:, None]
    h, part = row % H_B, row // H_B
    col = jnp.arange(H_B * DH_B)[None, :]
    lane = jnp.where(h % 2 == 0, DH_B, 0) + part
    return ((col == (h // 2) * LANES + lane) & (part < N_SPLIT)).astype(BF16)


def _pack3(x):
    hi, mid, lo = (p.astype(F32) for p in _split3(x))
    return (hi + pltpu.roll(mid, H_B, 1) + pltpu.roll(lo, 2 * H_B, 1)).astype(BF16)


def _unpack3_sum(y):
    s = y + pltpu.roll(y, LANES - H_B, 1) + pltpu.roll(y, LANES - 2 * H_B, 1)
    return jnp.where(_iota(y.shape, 1) < H_B, s, 0.0)


def _kvq_t_body(h_ref, gkv_ref, wk_ref, wv_ref, wf_ref, bf_ref, kg_ref, gb_ref, wq_ref, wog_ref,
                qg_ref, place_ref, kt_ref, vt_ref, lft_ref, k0_ref, k1_ref, vtb_ref, qt_ref, og_ref,
                carry, *, seq):
    tm = h_ref.shape[1]
    valid = pl.program_id(1) * tm + _iota((tm, 1), 0) < seq
    h = h_ref[0]
    hr = h * _rms_scale(h)
    xs = (hr * gkv_ref[...]).astype(BF16)
    k = jnp.where(valid, _head_rmsnorm64(_dot(xs, wk_ref[...])) * kg_ref[...], 0.0)
    lf = jnp.where(valid, _log_sigmoid(_dot(xs, wf_ref[...]) + bf_ref[...]), 0.0)

    @pl.when(pl.program_id(1) == 0)
    def _():
        carry[...] = jnp.zeros(carry.shape, F32)

    subs = [slice(s * LANES, (s + 1) * LANES) for s in range(tm // LANES)]
    tril = jnp.where(_iota((LANES, LANES), 0) >= _iota((LANES, LANES), 1), 1.0, 0.0).astype(BF16)
    lo_half = (_iota((LANES, k.shape[1]), 1) & (LANES - 1)) < DH_B
    lf_heads = jnp.where(_iota(lf.shape, 1) < H_B, lf, 0.0)
    local = [_unpack3_sum(_dot(tril, _pack3(lf_heads[rs, :]))) for rs in subs]

    vt = jnp.where(valid, _dot(xs, wv_ref[...]), 0.0).T
    vt_ref[0] = vt
    vtb_ref[0] = vt.astype(BF16)

    run = carry[0:1, :]
    for rs, loc in zip(subs, local):
        c = loc + run
        run = c[LANES - 1:LANES, :]
        bias = _dot(_pack3(c * LOG2E), place_ref[...])
        k0_ref[0, rs, :] = jnp.where(lo_half, k[rs, :], bias).astype(BF16)
        k1_ref[0, rs, :] = jnp.where(lo_half, bias, k[rs, :]).astype(BF16)
    carry[0:1, :] = run

    kt_ref[0] = k.T
    lft_ref[0] = lf.T[:H_B, :]
    xq = (hr * gb_ref[...]).astype(BF16)
    q = _head_rmsnorm64(_dot(xq, wq_ref[...])) * qg_ref[...]
    qt_ref[0] = jnp.where(valid, q, 0.0).T.astype(BF16)
    og_ref[0] = _dot(xq, wog_ref[...]).astype(BF16)


def _kvq_proj_t(h, gkv, wk, wv, wf, bf, kg, gb, wq, wog, qg, batch, seq):
    d = h.shape[1]
    hd = wk.shape[1]
    tm = KVQ_T_TILE
    nt = pl.cdiv(seq, tm)
    tpad = nt * tm
    rows = lambda b, i: (b, i, 0)
    cols = lambda b, i: (b, 0, i)
    place = _placement()
    return pl.pallas_call(
        functools.partial(_kvq_t_body, seq=seq),
        grid=(batch, nt),
        in_specs=[pl.BlockSpec((1, tm, d), rows)] + [_const_spec(w.shape) for w in
                                                      (gkv, wk, wv, wf, bf, kg, gb, wq, wog, qg, place)],
        out_specs=[pl.BlockSpec((1, hd, tm), cols), pl.BlockSpec((1, hd, tm), cols),
                   pl.BlockSpec((1, H_B, tm), cols), pl.BlockSpec((1, tm, hd), rows),
                   pl.BlockSpec((1, tm, hd), rows), pl.BlockSpec((1, hd, tm), cols),
                   pl.BlockSpec((1, hd, tm), cols), pl.BlockSpec((1, tm, wog.shape[1]), rows)],
        out_shape=[jax.ShapeDtypeStruct((batch, hd, seq), F32), jax.ShapeDtypeStruct((batch, hd, seq), F32),
                   jax.ShapeDtypeStruct((batch, H_B, seq), F32), jax.ShapeDtypeStruct((batch, tpad, hd), BF16),
                   jax.ShapeDtypeStruct((batch, tpad, hd), BF16), jax.ShapeDtypeStruct((batch, hd, tpad), BF16),
                   jax.ShapeDtypeStruct((batch, hd, tpad), BF16),
                   jax.ShapeDtypeStruct((batch, seq, wog.shape[1]), BF16)],
        scratch_shapes=[pltpu.VMEM((SUBLANES, LANES), F32)],
        compiler_params=_cparams("parallel", "arbitrary"),
        name="kvq_proj_t",
    )(h.reshape(batch, seq, d), gkv, wk, wv, wf, bf, kg, gb, wq, wog, qg, place)


def _kvq_proj(h, gkv, wk, wv, wf, bf, kg, gb, wq, wog, qg, tm):
    n, d = h.shape
    hd = wk.shape[1]
    row = lambda i: (i, 0)
    wide = pl.BlockSpec((tm, hd), row)
    return pl.pallas_call(
        _kvq_body,
        grid=(pl.cdiv(n, tm),),
        in_specs=[pl.BlockSpec((tm, d), row)] + [_const_spec(w.shape) for w in
                                                 (gkv, wk, wv, wf, bf, kg, gb, wq, wog, qg)],
        out_specs=[wide, wide, pl.BlockSpec((tm, H_B), row), wide, wide, wide,
                   pl.BlockSpec((tm, wog.shape[1]), row)],
        out_shape=[jax.ShapeDtypeStruct((n, hd), F32), jax.ShapeDtypeStruct((n, hd), F32),
                   jax.ShapeDtypeStruct((n, H_B), F32), jax.ShapeDtypeStruct((n, hd), BF16),
                   jax.ShapeDtypeStruct((n, hd), BF16), jax.ShapeDtypeStruct((n, hd), BF16),
                   jax.ShapeDtypeStruct((n, wog.shape[1]), BF16)],
        compiler_params=_cparams("parallel"),
        name="kvq_proj",
    )(h, gkv, wk, wv, wf, bf, kg, gb, wq, wog, qg)


def _col_to_row(col):
    n = col.shape[0]
    eye = _iota((n, n), 0) == _iota((n, n), 1)
    return jnp.sum(jnp.where(eye, col, 0.0), axis=0, keepdims=True)


def _row_to_col(row):
    n = row.shape[1]
    eye = _iota((n, n), 0) == _iota((n, n), 1)
    return jnp.sum(jnp.where(eye, row, 0.0), axis=1, keepdims=True)


def _mlstm_chunk_heads(qk, v, li, lf, cfull, m):
    heads = range(len(qk))
    L = qk[0].shape[0]
    tril = _iota((L, L), 0) >= _iota((L, L), 1)
    lane = _iota((L, LANES), 1)
    ones_col = jnp.where(lane == 0, 1.0, 0.0)

    q_lo = [jnp.where(lane < DK_A, qk[i], 0.0).astype(BF16) for i in heads]
    kq = [pltpu.roll(qk[i], DK_A, 1) for i in heads]
    vaug = [jnp.concatenate([v[i], ones_col], axis=1).astype(BF16) for i in heads]
    qk_t = [_dot_nt(q_lo[i], kq[i].astype(BF16)) for i in heads]
    q_c = [_dot(q_lo[i], cfull[i].astype(BF16)) for i in heads]

    lf_row = [_col_to_row(lf[i]) for i in heads]
    li_row = [_col_to_row(li[i]) for i in heads]
    b = [jnp.sum(jnp.where(tril, lf_row[i], 0.0), axis=1, keepdims=True) for i in heads]
    b_row = [_col_to_row(b[i]) for i in heads]
    dmat = [jnp.where(tril, b[i] - b_row[i] + li_row[i], -jnp.inf) for i in heads]
    dmax = [jnp.max(dmat[i], axis=1, keepdims=True) for i in heads]

    b_end = [b[i][L - 1:L, :] for i in heads]
    g = [b_end[i] - b[i] + li[i] for i in heads]
    m_new = [jnp.maximum(b_end[i] + m[i], jnp.max(g[i], axis=0, keepdims=True)) for i in heads]
    w_c = [jnp.exp(b_end[i] + m[i] - m_new[i]) for i in heads]
    upd = [_dot_tn((jnp.exp(g[i] - m_new[i]) * kq[i]).astype(BF16), vaug[i]) for i in heads]
    keep = _iota(cfull[0].shape, 0) < DK_A
    cfull_new = [jnp.where(keep, w_c[i] * cfull[i] + upd[i], 0.0) for i in heads]

    inter = [b[i] + m[i] for i in heads]
    m_t = [jnp.maximum(inter[i], dmax[i]) for i in heads]
    s = [(qk_t[i] * jnp.exp(dmat[i] - m_t[i])).astype(BF16) for i in heads]
    tot = [jnp.exp(inter[i] - m_t[i]) * q_c[i] + _dot(s[i], vaug[i]) for i in heads]
    h = [tot[i][:, :DV_A] / jnp.maximum(jnp.abs(tot[i][:, DV_A:DV_A + 1]), jnp.exp(-m_t[i])) for i in heads]
    return h, cfull_new, m_new


def _mlstm_head_out(h, og, gain):
    return h * _rms_scale(h) * gain * _sigmoid(og)


def _mlstm_prompt_body(qk_ref, v_ref, og_ref, gt_ref, mhg_ref, hg_ref, c_ref, n_ref, m_ref,
                       cst, mst, *, seq, hb):
    grp = pl.program_id(1)
    ch = MLSTM_CHUNK
    nfull, tail = seq // ch, seq % ch
    cst[...] = jnp.zeros(cst.shape, F32)
    mst[...] = jnp.zeros(mst.shape, F32)

    def chunk(r0, first_valid):
        gt = gt_ref[0, pl.ds(r0, ch), :]
        lane = _iota(gt.shape, 1)
        rowi = _iota((ch, 1), 0)
        li, lf, qk, v = [], [], [], []
        for hh in range(hb):
            head = grp * hb + hh
            li_h = jnp.sum(jnp.where(lane == head, gt, 0.0), axis=1, keepdims=True)
            lf_h = jnp.sum(jnp.where(lane == head + H_A, gt, 0.0), axis=1, keepdims=True)
            if first_valid:
                li_h = jnp.where(rowi >= first_valid, li_h, MASKED_GATE)
                lf_h = jnp.where(rowi >= first_valid, lf_h, 0.0)
            sl = slice(hh * LANES, (hh + 1) * LANES)
            li.append(li_h)
            lf.append(lf_h)
            qk.append(qk_ref[0, pl.ds(r0, ch), sl].astype(F32))
            v.append(v_ref[0, pl.ds(r0, ch), sl].astype(F32))
        hs, cnew, mnew = _mlstm_chunk_heads(qk, v, li, lf, [cst[hh] for hh in range(hb)],
                                            [mst[hh, 0:1, 0:1] for hh in range(hb)])
        outs = []
        for hh in range(hb):
            cst[hh] = cnew[hh]
            mst[hh] = jnp.broadcast_to(mnew[hh], mst.shape[1:])
            og = og_ref[0, pl.ds(r0, ch), hh * LANES:(hh + 1) * LANES].astype(F32)
            outs.append(_mlstm_head_out(hs[hh], og, mhg_ref[hh]))
        return jnp.concatenate(outs, axis=1).astype(BF16)

    def loop_body(j, carry):
        r0 = pl.multiple_of(j * ch, ch)
        hg_ref[0, pl.ds(r0, ch), :] = chunk(r0, 0)
        return carry

    lax.fori_loop(0, nfull, loop_body, 0)
    if tail:
        out = chunk(seq - ch, ch - tail)
        hg_ref[0, seq - tail:seq, :] = out[ch - tail:, :]

    for hh in range(hb):
        cfull = cst[hh]
        c_ref[0, hh] = cfull[:DK_A, :DV_A]
        n_ref[0, hh] = _col_to_row(cfull[:DK_A, DV_A:DV_A + 1])
        m_ref[0, hh] = mst[hh, 0:1, 0:1]


def _mlstm_prompt(qk, v, og, gt, mhg, batch, seq):
    hb = MLSTM_HEADS_PER_STEP
    w = hb * LANES
    qk3, v3, og3 = (a.reshape(batch, seq, a.shape[-1]) for a in (qk, v, og))
    gt3 = gt.reshape(batch, seq, LANES)
    seq_blk = lambda b, g: (b, 0, g)
    hg, c, n, m = pl.pallas_call(
        functools.partial(_mlstm_prompt_body, seq=seq, hb=hb),
        grid=(batch, H_A // hb),
        in_specs=[pl.BlockSpec((1, seq, w), seq_blk), pl.BlockSpec((1, seq, w), seq_blk),
                  pl.BlockSpec((1, seq, w), seq_blk),
                  pl.BlockSpec((1, seq, LANES), lambda b, g: (b, 0, 0)),
                  pl.BlockSpec((hb, 1, DV_A), lambda b, g: (g, 0, 0))],
        out_specs=[pl.BlockSpec((1, seq, w), seq_blk),
                   pl.BlockSpec((1, hb, DK_A, DV_A), lambda b, g: (b, g, 0, 0)),
                   pl.BlockSpec((1, hb, 1, DK_A), lambda b, g: (b, g, 0, 0)),
                   pl.BlockSpec((1, hb, 1, 1), lambda b, g: (b, g, 0, 0))],
        out_shape=[jax.ShapeDtypeStruct((batch, seq, H_A * DV_A), BF16),
                   jax.ShapeDtypeStruct((batch, H_A, DK_A, DV_A), F32),
                   jax.ShapeDtypeStruct((batch, H_A, 1, DK_A), F32),
                   jax.ShapeDtypeStruct((batch, H_A, 1, 1), F32)],
        scratch_shapes=[pltpu.VMEM((hb, LANES, 2 * LANES), F32), pltpu.VMEM((hb, SUBLANES, LANES), F32)],
        compiler_params=_cparams("parallel", "arbitrary"),
        name="mlstm_prompt",
    )(qk3, v3, og3, gt3, mhg.reshape(H_A, 1, DV_A))
    return (hg.reshape(batch * seq, H_A * DV_A), c, n.reshape(batch, H_A, DK_A),
            m.reshape(batch, H_A))


def _mlstm_sample_body(qk_ref, v_ref, og_ref, gt_ref, mhg_ref, c0_ref, n0_ref, m0_ref,
                       hg_ref, c_ref, n_ref, m_ref, *, steps):
    ch = pl.cdiv(steps, SAMPLE_CHUNK_ALIGN) * SAMPLE_CHUNK_ALIGN
    pad = jnp.zeros((ch - steps, LANES), F32)
    padded = (lambda a: jnp.concatenate([a, pad], axis=0)) if ch > steps else (lambda a: a)
    rowi = _iota((ch, 1), 0)
    lane1 = _iota((1, LANES), 1)
    lane_c = _iota((DK_A, LANES), 1)
    nseq = qk_ref.shape[0]
    li, lf, qk, v, cfull, m0 = [], [], [], [], [], []
    for sq in range(nseq):
        gt = padded(gt_ref[sq])
        lane = _iota(gt.shape, 1)
        for hh in range(H_A):
            sl = slice(hh * LANES, (hh + 1) * LANES)
            li_h = jnp.sum(jnp.where(lane == hh, gt, 0.0), axis=1, keepdims=True)
            lf_h = jnp.sum(jnp.where(lane == hh + H_A, gt, 0.0), axis=1, keepdims=True)
            li.append(jnp.where(rowi < steps, li_h, MASKED_GATE))
            lf.append(jnp.where(rowi < steps, lf_h, 0.0))
            qk.append(padded(qk_ref[sq, :, sl].astype(F32)))
            v.append(padded(v_ref[sq, :, sl].astype(F32)))
            ncol = _row_to_col(n0_ref[sq, hh:hh + 1, :])
            top = jnp.concatenate([c0_ref[sq, hh], jnp.where(lane_c == 0, ncol, 0.0)], axis=1)
            cfull.append(jnp.concatenate([top, jnp.zeros((LANES - DK_A, 2 * LANES), F32)], axis=0))
            m0.append(m0_ref[sq, :, hh:hh + 1])
    hs, cnew, mnew = _mlstm_chunk_heads(qk, v, li, lf, cfull, m0)
    for sq in range(nseq):
        m_out = jnp.zeros((1, LANES), F32)
        outs = []
        for hh in range(H_A):
            i = sq * H_A + hh
            og = og_ref[sq, :, hh * LANES:(hh + 1) * LANES].astype(F32)
            outs.append(_mlstm_head_out(hs[i][:steps, :], og, mhg_ref[hh]))
            c_ref[sq, hh] = cnew[i][:DK_A, :DV_A]
            n_ref[sq, hh:hh + 1, :] = _col_to_row(cnew[i][:DK_A, DV_A:DV_A + 1])
            m_out = jnp.where(lane1 == hh, mnew[i], m_out)
        hg_ref[sq] = jnp.concatenate(outs, axis=1).astype(BF16)
        m_ref[sq] = m_out[:, :H_A]


def _mlstm_sample(qk, v, og, gt, mhg, c0, n0, m0, batch, steps):
    wide = H_A * LANES
    ns = MLSTM_SAMPLE_SEQS if batch % MLSTM_SAMPLE_SEQS == 0 else 1
    blk3 = lambda b: (b, 0, 0)
    hg, c, n, m = pl.pallas_call(
        functools.partial(_mlstm_sample_body, steps=steps),
        grid=(batch // ns,),
        in_specs=[pl.BlockSpec((ns, steps, wide), blk3), pl.BlockSpec((ns, steps, wide), blk3),
                  pl.BlockSpec((ns, steps, wide), blk3), pl.BlockSpec((ns, steps, LANES), blk3),
                  _const_spec((H_A, 1, DV_A)),
                  pl.BlockSpec((ns, H_A, DK_A, DV_A), lambda b: (b, 0, 0, 0)),
                  pl.BlockSpec((ns, H_A, DK_A), blk3), pl.BlockSpec((ns, 1, H_A), blk3)],
        out_specs=[pl.BlockSpec((ns, steps, wide), blk3),
                   pl.BlockSpec((ns, H_A, DK_A, DV_A), lambda b: (b, 0, 0, 0)),
                   pl.BlockSpec((ns, H_A, DK_A), blk3), pl.BlockSpec((ns, 1, H_A), blk3)],
        out_shape=[jax.ShapeDtypeStruct((batch, steps, wide), BF16),
                   jax.ShapeDtypeStruct((batch, H_A, DK_A, DV_A), F32),
                   jax.ShapeDtypeStruct((batch, H_A, DK_A), F32),
                   jax.ShapeDtypeStruct((batch, 1, H_A), F32)],
        compiler_params=_cparams("parallel"),
        name="mlstm_sample",
    )(qk.reshape(batch, steps, wide), v.reshape(batch, steps, wide), og.reshape(batch, steps, wide),
      gt.reshape(batch, steps, LANES), mhg.reshape(H_A, 1, DV_A), c0, n0, m0.reshape(batch, 1, H_A))
    return hg.reshape(batch * steps, wide), c, n, m.reshape(batch, H_A)


def _rows_to_lanes(x16, staging_ref):
    staging_ref[...] = jnp.zeros(staging_ref.shape, F32)
    staging_ref[:, 0:x16.shape[1]] = x16
    return staging_ref[...].T[0:x16.shape[1], :]


def _attn_prompt_body(qt_ref, k0_ref, k1_ref, vt_ref, og_ref, o_ref, s_scr, p_scr, acc_scr, qa_scr,
                      mask_scr, *, seq):
    tq, tk = ATTN_TQ, ATTN_TK
    tpad = qt_ref.shape[2]
    nfull = seq // tq
    nh = ATTN_HEADS
    heads = range(nh)

    def keys(h, rows):
        pair = slice((h // 2) * LANES, (h // 2 + 1) * LANES)
        return (k0_ref if h % 2 == 0 else k1_ref)[0, rows, pair]

    def augmented_queries(qt):
        out = []
        for pp in range(nh // 2):
            blk = qt[pp * LANES:(pp + 1) * LANES, :]
            row = _iota(blk.shape, 0)
            out.append(jnp.where(row < DH_B, blk, jnp.where(row < DH_B + N_SPLIT, -1.0, 0.0)).astype(BF16))
            out.append(jnp.where(row >= DH_B, blk, jnp.where(row < N_SPLIT, -1.0, 0.0)).astype(BF16))
        return out

    def with_ones(vt):
        return jnp.concatenate([vt, jnp.ones((ROWSUM_ROWS, vt.shape[1]), BF16)], axis=0)

    def tail_tile(q0, width, rows_out):
        qa = augmented_queries(qt_ref[0, :, pl.ds(q0, width)].astype(F32))
        causal = _iota((tpad, width), 0) <= q0 + _iota((tpad, width), 1)
        scores = [_dot(keys(hh, slice(None)), qa[hh]) for hh in heads]
        probs = []
        for hh in heads:
            s = jnp.where(causal, scores[hh], -jnp.inf)
            probs.append(jnp.exp2(s - jnp.max(s, axis=0, keepdims=True)).astype(BF16))
        fulls = [_dot(with_ones(vt_ref[0, hh * DH_B:(hh + 1) * DH_B, :]), probs[hh]) for hh in heads]
        outs = [full[:DH_B, :] / full[DH_B:DH_B + 1, :] for full in fulls]
        out = jnp.concatenate(outs, axis=0).T[:rows_out, :]
        gate = _sigmoid(og_ref[0, pl.ds(q0, rows_out), :].astype(F32))
        o_ref[0, pl.ds(q0, rows_out), :] = (out * gate).astype(BF16)

    per = tq // tk
    for d in range(per):
        key = d * tk + _iota((tk, tq), 0)
        mask_scr[d] = jnp.where(key <= _iota((tk, tq), 1), 0.0, -jnp.inf)
    p_scr[1] = jnp.zeros(p_scr.shape[1:], BF16)
    acc_scr[...] = jnp.zeros(acc_scr.shape, F32)

    def first_lane(diag_idx):
        return 0 if diag_idx is None else diag_idx * tk

    def left_pad(x, lo, fill):
        return x if lo == 0 else jnp.concatenate([jnp.full((x.shape[0], lo), fill, x.dtype), x], axis=1)

    def value_product(hh, k0, slot, lo):
        vt = with_ones(vt_ref[0, hh * DH_B:(hh + 1) * DH_B, pl.ds(k0, tk)])
        return left_pad(_dot(vt, p_scr[slot, hh, :, lo:]), lo, 0.0)

    def pipe_step(ms, kidx, slot, diag_idx, issue_next):
        k0 = pl.multiple_of(kidx * tk, tk)
        lo = first_lane(diag_idx)
        lo_prev = first_lane(diag_idx - 1 if diag_idx else None)
        lo_next = first_lane(None if diag_idx is None else diag_idx + 1)
        if issue_next:
            for hh in heads:
                s_scr[1 - slot, hh, :, lo_next:] = _dot(keys(hh, pl.ds(k0 + tk, tk)), qa_scr[hh, :, lo_next:])
        kprev = pl.multiple_of(jnp.maximum(kidx - 1, 0) * tk, tk)
        pvs = [value_product(hh, kprev, 1 - slot, lo_prev) for hh in heads]

        def scores(hh):
            s = s_scr[slot, hh, :, lo:]
            return s if diag_idx is None else s + mask_scr[diag_idx, :, lo:]

        new = [jnp.maximum(ms[hh][:, lo:], jnp.max(scores(hh), axis=0, keepdims=True)) for hh in heads]
        for hh in heads:
            p_scr[slot, hh, :, lo:] = jnp.exp2(scores(hh) - new[hh]).astype(BF16)
        new = [n if lo == 0 else jnp.concatenate([ms[hh][:, :lo], n], axis=1) for hh, n in zip(heads, new)]
        for hh in heads:
            acc_scr[hh] = (acc_scr[hh] + pvs[hh]) * jnp.exp2(ms[hh] - new[hh])
        return tuple(new)

    def q_tile_pipelined(i):
        q0 = pl.multiple_of(i * tq, tq)
        qa = augmented_queries(qt_ref[0, :, pl.ds(q0, tq)].astype(F32))
        for hh in heads:
            qa_scr[hh] = qa[hh]
        for hh in heads:
            s_scr[0, hh] = _dot(keys(hh, pl.ds(0, tk)), qa_scr[hh])
        ms = tuple(jnp.full((1, tq), NEG_INIT, F32) for _ in heads)

        def group(jj, st):
            for d in range(per):
                st = pipe_step(st, jj * per + d, d % 2, None, True)
            return st

        ms = lax.fori_loop(0, i, group, ms)
        for d in range(per):
            ms = pipe_step(ms, i * per + d, d % 2, d, d < per - 1)
        last = (per - 1) % 2
        klast = pl.multiple_of((i * per + per - 1) * tk, tk)
        outs = []
        for hh in heads:
            full = acc_scr[hh] + value_product(hh, klast, last, first_lane(per - 1))
            outs.append(full[:DH_B, :] / full[DH_B:DH_B + 1, :])
        out = jnp.concatenate(outs, axis=0).T
        gate = _sigmoid(og_ref[0, pl.ds(q0, tq), :].astype(F32))
        o_ref[0, pl.ds(q0, tq), :] = (out * gate).astype(BF16)

    def qbody(i, _):
        q_tile_pipelined(i)
        return 0

    lax.fori_loop(0, nfull, qbody, 0)
    if seq > nfull * tq:
        q0 = nfull * tq
        tail_tile(q0, tpad - q0, seq - q0)


def _attn_prompt(qt, k0, k1, vt, og, seq):
    batch, wide, tpad = qt.shape
    tail_w = tpad - (seq // ATTN_TQ) * ATTN_TQ
    assert ATTN_TQ % (2 * ATTN_TK) == 0 and 0 <= tail_w and tail_w % LANES == 0
    rows = lambda b, p: (b, 0, p)
    cols = lambda b, p: (b, p, 0)
    nh = ATTN_HEADS
    w = nh * DH_B
    out = pl.pallas_call(
        functools.partial(_attn_prompt_body, seq=seq),
        grid=(batch, H_B // nh),
        in_specs=[pl.BlockSpec((1, w, tpad), cols), pl.BlockSpec((1, tpad, w), rows),
                  pl.BlockSpec((1, tpad, w), rows), pl.BlockSpec((1, w, tpad), cols),
                  pl.BlockSpec((1, seq, w), rows)],
        out_specs=pl.BlockSpec((1, seq, w), rows),
        out_shape=jax.ShapeDtypeStruct((batch, seq, wide), BF16),
        scratch_shapes=[pltpu.VMEM((2, nh, ATTN_TK, ATTN_TQ), F32),
                        pltpu.VMEM((2, nh, ATTN_TK, ATTN_TQ), BF16),
                        pltpu.VMEM((nh, DH_B + ROWSUM_ROWS, ATTN_TQ), F32),
                        pltpu.VMEM((nh, LANES, ATTN_TQ), BF16),
                        pltpu.VMEM((ATTN_TQ // ATTN_TK, ATTN_TK, ATTN_TQ), F32)],
        compiler_params=_cparams("parallel", "parallel"),
        name="attn_prompt",
    )(qt, k0, k1, vt, og)
    return out.reshape(batch * seq, wide)


def _attn_sample_body(pt_ref, q_ref, kn_ref, vn_ref, lfn_ref, og_ref, *rest, steps, npages):
    k_refs, v_refs, lf_refs = rest[:npages], rest[npages:2 * npages], rest[2 * npages:3 * npages]
    o_ref, stage = rest[3 * npages], rest[3 * npages + 1]
    rows = H_B * steps
    wide = H_B * DH_B
    u = _iota((LANES, LANES), 0)
    s_ = _iota((LANES, LANES), 1)

    q = q_ref[0].astype(F32)
    qrep = jnp.concatenate([jnp.broadcast_to(q[t:t + 1, :], (H_B, wide)) for t in range(steps)], axis=0)
    diag = _iota((rows, wide), 0) % H_B == _iota((rows, wide), 1) // DH_B
    qbd = jnp.where(diag, qrep, 0.0).astype(BF16)

    lf_all = jnp.concatenate([r[0] for r in lf_refs], axis=0)
    later_and_ones = jnp.concatenate([jnp.where(u > s_, 1.0, 0.0), jnp.ones((LANES, LANES), F32)],
                                     axis=1).astype(BF16)
    wt = _dot_by_01(lf_all, later_and_ones)
    pr = _iota((npages * H_B, npages * H_B), 0)
    pc = _iota((npages * H_B, npages * H_B), 1)
    later_pages = jnp.where((pc % H_B == pr % H_B) & (pc // H_B > pr // H_B), 1.0, 0.0).astype(BF16)
    hi, mid, lo = _split3(wt[:, LANES:])
    rsum = wt[:, :LANES] + _dot(later_pages, hi) + _dot(later_pages, mid) + _dot(later_pages, lo)
    bias_past = jnp.concatenate(
        [jnp.concatenate([rsum[r * H_B:(r + 1) * H_B, :]] * steps, axis=0) for r in range(npages)], axis=1)

    kcat = jnp.concatenate([r[0].astype(BF16) for r in k_refs], axis=1)
    s_past = _dot(qbd, kcat) + bias_past

    zpad = jnp.zeros((LANES - steps, wide), F32)
    kn = jnp.concatenate([kn_ref[0].astype(F32), zpad], axis=0).astype(BF16)
    vn = jnp.concatenate([vn_ref[0].astype(F32), zpad], axis=0).astype(BF16)
    lfn = jnp.concatenate([lfn_ref[0], jnp.zeros((LANES - steps, H_B), F32)], axis=0)
    incl = jnp.where(u <= s_, 1.0, 0.0).astype(BF16)
    cnew = _dot_by_01(_rows_to_lanes(lfn, stage), incl)
    key = _iota((rows, LANES), 1)
    qry = _iota((rows, LANES), 0) // H_B
    bias_new = jnp.where(key <= qry, -jnp.concatenate([cnew] * steps, axis=0), -jnp.inf)
    s_new = _dot_nt(qbd, kn) + bias_new

    m = jnp.maximum(jnp.max(s_past, axis=1, keepdims=True), jnp.max(s_new, axis=1, keepdims=True))
    p_past = jnp.exp(s_past - m)
    p_new = jnp.exp(s_new - m)
    l = jnp.sum(p_past, axis=1, keepdims=True) + jnp.sum(p_new, axis=1, keepdims=True)
    vcat = jnp.concatenate([r[0].astype(BF16) for r in v_refs], axis=1)
    acc = _dot_nt(p_past.astype(BF16), vcat) + _dot(p_new.astype(BF16), vn)
    full = jnp.where(diag, acc / l, 0.0)
    out = jnp.concatenate([jnp.sum(full[t * H_B:(t + 1) * H_B, :], axis=0, keepdims=True)
                           for t in range(steps)], axis=0)
    o_ref[0] = (out * _sigmoid(og_ref[0].astype(F32))).astype(BF16)


def _attn_sample(q, kn, vn, lfn, og, cache_k, cache_v, cache_logf, page_table, batch, steps):
    wide = H_B * DH_B
    n_phys, page = cache_k.shape[0], cache_k.shape[1]
    npages = page_table.shape[1]
    assert page == LANES
    ck = jnp.transpose(cache_k, (0, 2, 3, 1)).reshape(n_phys, wide, page)
    cv = jnp.transpose(cache_v, (0, 2, 3, 1)).reshape(n_phys, wide, page)
    clf = jnp.transpose(cache_logf, (0, 2, 1))
    tok = lambda b, pt: (b, 0, 0)
    page_of = lambda r: (lambda b, pt: (pt[b, r], 0, 0))
    out = pl.pallas_call(
        functools.partial(_attn_sample_body, steps=steps, npages=npages),
        grid_spec=pltpu.PrefetchScalarGridSpec(
            num_scalar_prefetch=1,
            grid=(batch,),
            in_specs=[pl.BlockSpec((1, steps, wide), tok), pl.BlockSpec((1, steps, wide), tok),
                      pl.BlockSpec((1, steps, wide), tok), pl.BlockSpec((1, steps, H_B), tok),
                      pl.BlockSpec((1, steps, wide), tok)]
            + [pl.BlockSpec((1, wide, page), page_of(r)) for r in range(npages)]
            + [pl.BlockSpec((1, wide, page), page_of(r)) for r in range(npages)]
            + [pl.BlockSpec((1, H_B, page), page_of(r)) for r in range(npages)],
            out_specs=pl.BlockSpec((1, steps, wide), tok),
            scratch_shapes=[pltpu.VMEM((LANES, LANES), F32)]),
        out_shape=jax.ShapeDtypeStruct((batch, steps, wide), BF16),
        compiler_params=_cparams("parallel"),
        name="attn_sample",
    )(page_table, q.reshape(batch, steps, wide), kn.reshape(batch, steps, wide),
      vn.reshape(batch, steps, wide), lfn.reshape(batch, steps, H_B), og.reshape(batch, steps, wide),
      *([ck] * npages), *([cv] * npages), *([clf] * npages))
    return out.reshape(batch * steps, wide)


def _prep_params(norm_a, w_in_a, b_ig_a, b_fg_a, mh_norm_a, w_out_a, norm_kv, w_kvf, b_fg_b,
                 k_norm_b, norm_b, w_qo_b, q_norm_b, w_out_b, norm_ffn, w_gate_up, w_down,
                 norm_final):
    d = w_in_a.shape[1]
    hk, hv, hd = H_A * DK_A, H_A * DV_A, H_B * DH_B
    w_in = w_in_a[0]
    wq = w_in[:, :hk].reshape(d, H_A, DK_A)
    wk = w_in[:, hk:2 * hk].reshape(d, H_A, DK_A)
    row = lambda a: a.reshape(1, -1).astype(F32)
    pad_cols = lambda a: jnp.pad(a, ((0, 0), (0, LANES - a.shape[1])))
    lane = jnp.arange(H_A * LANES) % LANES
    return dict(
        norm_a=row(norm_a[0]),
        wqk=jnp.concatenate([wq, wk], axis=2).reshape(d, H_A * LANES).astype(BF16),
        wv=w_in[:, 2 * hk:2 * hk + hv].astype(BF16),
        wog=w_in[:, 2 * hk + hv:2 * hk + 2 * hv].astype(BF16),
        wg=pad_cols(w_in[:, 2 * hk + 2 * hv:]).astype(BF16),
        bg=pad_cols(jnp.concatenate([b_ig_a[0], b_fg_a[0]]).reshape(1, -1).astype(F32)),
        qs=jnp.where(lane < DK_A, DK_A ** -0.5, 1.0).reshape(1, -1).astype(F32),
        mhg=mh_norm_a[0].astype(F32),
        wo_a=w_out_a[0].astype(BF16),
        gkv=row(norm_kv),
        wk=w_kvf[:, :hd].astype(BF16),
        wvs=w_kvf[:, hd:2 * hd].astype(BF16),
        wf=pad_cols(w_kvf[:, 2 * hd:]).astype(BF16),
        bf=pad_cols(b_fg_b.reshape(1, -1).astype(F32)),
        kg=row(jnp.tile(k_norm_b, H_B)),
        gb=row(norm_b[0]),
        wq=w_qo_b[0][:, :hd].astype(BF16),
        wog_b=w_qo_b[0][:, hd:].astype(BF16),
        qg=row(jnp.tile(q_norm_b[0], H_B)) * DH_B ** -0.5,
        wo_b=w_out_b[0].astype(BF16),
        gf=[row(norm_ffn[l]) for l in range(2)],
        wgu=[w_gate_up[l].astype(BF16) for l in range(2)],
        wd=[w_down[l].astype(BF16) for l in range(2)],
        gfin=row(norm_final),
    )


def _layer0(h, p, tm, mlstm):
    qk, v, og, gt = _proj_in(h, p["norm_a"], p["wqk"], p["wv"], p["wog"], p["wg"], p["bg"], p["qs"], tm)
    hg, c, n, m = mlstm(qk, v, og, gt)
    h2 = _mix_ffn(hg, h, p["wo_a"], p["gf"][0], p["wgu"][0], p["wd"][0], p["gfin"], tm, False)
    return h2, c, n, m


def _shared_and_q(h2, p, tm):
    return _kvq_proj(h2, p["gkv"], p["wk"], p["wvs"], p["wf"], p["bf"], p["kg"], p["gb"],
                     p["wq"], p["wog_b"], p["qg"], tm)


def _layer1_tail(o, h2, p, tm, drop_lead=None):
    return _mix_ffn(o, h2, p["wo_b"], p["gf"][1], p["wgu"][1], p["wd"][1], p["gfin"], tm, True, drop_lead)


def kernel(x_prompt, x_sample, state_C, state_n, state_m, cache_k, cache_v, cache_logf, page_table,
           meta_tokens, norm_a, w_in_a, b_ig_a, b_fg_a, mh_norm_a, w_out_a, norm_kv, w_kvf, b_fg_b,
           k_norm_b, norm_b, w_qo_b, q_norm_b, w_out_b, norm_ffn, w_gate_up, w_down, norm_final):
    assert w_in_a.shape[0] == 1 and w_qo_b.shape[0] == 1 and norm_ffn.shape[0] == 2
    p = _prep_params(norm_a, w_in_a, b_ig_a, b_fg_a, mh_norm_a, w_out_a, norm_kv, w_kvf, b_fg_b,
                     k_norm_b, norm_b, w_qo_b, q_norm_b, w_out_b, norm_ffn, w_gate_up, w_down,
                     norm_final)
    bp, sp, d = x_prompt.shape
    bs, ss, _ = x_sample.shape
    tp = sp + N_META
    hd = H_B * DH_B
    tm = 512

    meta = jnp.broadcast_to(meta_tokens[None].astype(F32), (bp, N_META, d))
    h0 = jnp.concatenate([meta, x_prompt], axis=1).reshape(bp * tp, d)
    h2, p_c, p_n, p_m = _layer0(h0, p, tm, functools.partial(_mlstm_prompt, mhg=p["mhg"], batch=bp, seq=tp))
    kt, vt, lft, k0, k1, vtb, qt, og = _kvq_proj_t(
        h2, p["gkv"], p["wk"], p["wvs"], p["wf"], p["bf"], p["kg"], p["gb"], p["wq"], p["wog_b"],
        p["qg"] * LOG2E, bp, tp)
    o = _attn_prompt(qt, k0, k1, vtb, og, tp)
    y_prompt = _layer1_tail(o, h2, p, tm, drop_lead=(bp, tp, N_META)).reshape(bp, sp, d)
    p_k = jnp.transpose(kt.reshape(bp, H_B, DH_B, tp), (0, 3, 1, 2))
    p_v = jnp.transpose(vt.reshape(bp, H_B, DH_B, tp), (0, 3, 1, 2))
    p_lf = jnp.transpose(lft, (0, 2, 1))

    hs0 = x_sample.reshape(bs * ss, d)
    hs2, s_c, s_n, s_m = _layer0(
        hs0, p, tm, functools.partial(_mlstm_sample, mhg=p["mhg"], c0=state_C[0], n0=state_n[0],
                                      m0=state_m[0], batch=bs, steps=ss))
    ks, vs, lfs, kbs, vbs, qbs, ogs = _shared_and_q(hs2, p, tm)
    os_ = _attn_sample(qbs, kbs, vbs, lfs, ogs, cache_k, cache_v, cache_logf, page_table, bs, ss)
    y_sample = _layer1_tail(os_, hs2, p, tm).reshape(bs, ss, d)

    return (y_prompt, y_sample, p_c[None], p_n[None], p_m[None], p_k, p_v, p_lf,
            s_c[None], s_n[None], s_m[None], ks.reshape(bs, ss, H_B, DH_B),
            vs.reshape(bs, ss, H_B, DH_B), lfs.reshape(bs, ss, H_B))
```
